```python
import functools
import jax, jax.numpy as jnp
from jax import lax
import numpy as np

D_MODEL = 1024
BATCH = 4
SEQ = 4096
DEPTH = 1
DEC_BATCH = 128
DEC_SEQ = 4
PAST_LEN = 2048
PAGE_SIZE = 128

HEAD_DIM = 64
WINDOWS = (128, 512, 2048)
DILATIONS = (1, 4, 16)
N_GROUPS = 3
HEADS_PER_GROUP = 4
N_HEADS = N_GROUPS * HEADS_PER_GROUP
ATTN_WIDTH = N_HEADS * HEAD_DIM
ATTN_OUT_WIDTH = HEADS_PER_GROUP * HEAD_DIM
ATTN_SCALE = HEAD_DIM ** -0.5
ROT_DIM = HEAD_DIM // 4
ROPE_THETA = 500000.0
CONV_CH = D_MODEL // 2
CONV_WIDTH = 31
D_FF = 256 * ((8 * D_MODEL // 3 + 255) // 256)
NORM_EPS = 1e-6
IN_SPLITS = (CONV_CH, 2 * CONV_CH, 2 * CONV_CH + ATTN_WIDTH, 2 * CONV_CH + 2 * ATTN_WIDTH, 2 * CONV_CH + 3 * ATTN_WIDTH)
IN_WIDTH = 2 * CONV_CH + 3 * ATTN_WIDTH + 2 * D_MODEL

kernel_name = "macaron_conv_dilated_window_hybrid_step"


def rmsnorm(x, g):
    xf = x.astype(jnp.float32)
    y = xf * lax.rsqrt(jnp.mean(xf * xf, axis=-1, keepdims=True) + NORM_EPS)
    return (y * g.astype(jnp.float32)).astype(x.dtype)


def swiglu_ffn(x, w_in, w_out):
    gate, up = jnp.split(x @ w_in, 2, axis=-1)
    return (jax.nn.silu(gate) * up) @ w_out


def rotary(x, pos):
    half = ROT_DIM // 2
    inv = jnp.float32(ROPE_THETA) ** (-jnp.arange(half, dtype=jnp.float32) * (2.0 / ROT_DIM))
    ang = pos.astype(jnp.float32)[:, None] * inv[None, :]
    cos = jnp.cos(ang)[:, None, :]
    sin = jnp.sin(ang)[:, None, :]
    xf = x.astype(jnp.float32)
    x1 = xf[..., :half]
    x2 = xf[..., half:ROT_DIM]
    out = jnp.concatenate([x1 * cos - x2 * sin, x2 * cos + x1 * sin, xf[..., ROT_DIM:]], axis=-1)
    return out.astype(x.dtype)


def mix_project(h, pos, w_in, b_gate, q_norm, k_norm):
    u_a, u_b, q, k, v, gates = jnp.split(h @ w_in, list(IN_SPLITS), axis=-1)
    u = u_a * jax.nn.sigmoid(u_b)
    shp = h.shape[:-1] + (N_HEADS, HEAD_DIM)
    q = rotary(rmsnorm(q.reshape(shp), q_norm), pos)
    k = rotary(rmsnorm(k.reshape(shp), k_norm), pos)
    v = v.reshape(shp)
    gates = jax.nn.sigmoid((gates + b_gate).astype(jnp.float32)).astype(h.dtype)
    return u, q, k, v, gates


def conv_branch(conv_in, conv_w, conv_b, ln_g, ln_b, w_conv_out):
    y = lax.conv_general_dilated(conv_in, conv_w[:, None, :], window_strides=(1,), padding="VALID",
                                 dimension_numbers=("NWC", "WIO", "NWC"), feature_group_count=CONV_CH) + conv_b
    yf = y.astype(jnp.float32)
    mu = jnp.mean(yf, axis=-1, keepdims=True)
    var = jnp.mean(jnp.square(yf - mu), axis=-1, keepdims=True)
    yn = ((yf - mu) * lax.rsqrt(var + NORM_EPS) * ln_g.astype(jnp.float32) + ln_b.astype(jnp.float32)).astype(y.dtype)
    return jax.nn.silu(yn) @ w_conv_out


def _with_prev_block(xb):
    prev = jnp.concatenate([jnp.zeros_like(xb[:, :1]), xb[:, :-1]], axis=1)
    return jnp.concatenate([prev, xb], axis=2)


def _band_scores(q, k, window, dil):
    b, tp, h, e = q.shape
    blk = window // dil
    nb = tp // window
    qb = q.reshape(b, nb, blk, dil, h, e)
    kb = _with_prev_block(k.reshape(b, nb, blk, dil, h, e))
    s = jnp.einsum("bnidhe,bnjdhe->bnidhj", qb, kb).astype(jnp.float32) * ATTN_SCALE
    i = jnp.arange(blk)[:, None]
    j = jnp.arange(2 * blk)[None, :]
    n = jnp.arange(nb)[:, None, None]
    rel = i + blk - j
    valid = (rel >= 0) & (rel <= blk) & (n * blk + j - blk >= 0)
    s = jnp.where(valid[None, :, :, None, None, :], s, -jnp.inf)
    return s.reshape(b, tp, h, 2 * blk)


def _band_values(p, v, window, dil):
    b, tp, h, e = v.shape
    blk = window // dil
    nb = tp // window
    vb = _with_prev_block(v.reshape(b, nb, blk, dil, h, e))
    pb = p.reshape(b, nb, blk, dil, h, 2 * blk).astype(v.dtype)
    return jnp.einsum("bnidhj,bnjdhe->bnidhe", pb, vb).reshape(b, tp, h, e)


def dilated_attention_prompt(q, k, v):
    b, t = q.shape[:2]
    w_max = max(WINDOWS)
    tp = -(-t // w_max) * w_max
    pad = ((0, 0), (0, tp - t), (0, 0), (0, 0))
    q, k, v = jnp.pad(q, pad), jnp.pad(k, pad), jnp.pad(v, pad)
    groups = list(zip(WINDOWS, DILATIONS))
    scores = [_band_scores(q[:, :, g * HEADS_PER_GROUP:(g + 1) * HEADS_PER_GROUP],
                           k[:, :, g * HEADS_PER_GROUP:(g + 1) * HEADS_PER_GROUP], w, d)
              for g, (w, d) in enumerate(groups)]
    p = jax.nn.softmax(jnp.concatenate(scores, axis=-1), axis=-1)
    outs = []
    off = 0
    for g, (w, d) in enumerate(groups):
        nk = scores[g].shape[-1]
        outs.append(_band_values(p[..., off:off + nk], v[:, :, g * HEADS_PER_GROUP:(g + 1) * HEADS_PER_GROUP], w, d))
        off += nk
    out = functools.reduce(jnp.add, outs)
    return out[:, :t].reshape(b, t, ATTN_OUT_WIDTH)


def dilated_attention_sample(q, k_full, v_full):
    b, s_len = q.shape[:2]
    scores, vals = [], []
    for g, (w, d) in enumerate(zip(WINDOWS, DILATIONS)):
        kf, vf = k_full[g], v_full[g]
        past = kf.shape[1] - s_len
        nk = w // d + 1
        idx = past + jnp.arange(s_len)[:, None] - d * jnp.arange(nk)[None, :]
        valid = idx >= 0
        idxc = jnp.maximum(idx, 0)
        kg = jnp.take(kf, idxc, axis=1)
        vals.append(jnp.take(vf, idxc, axis=1))
        sc = jnp.einsum("bshe,bsjhe->bshj", q[:, :, g * HEADS_PER_GROUP:(g + 1) * HEADS_PER_GROUP], kg)
        sc = sc.astype(jnp.float32) * ATTN_SCALE
        scores.append(jnp.where(valid[None, :, None, :], sc, -jnp.inf))
    p = jax.nn.softmax(jnp.concatenate(scores, axis=-1), axis=-1)
    outs = []
    off = 0
    for g in range(N_GROUPS):
        nk = scores[g].shape[-1]
        outs.append(jnp.einsum("bshj,bsjhe->bshe", p[..., off:off + nk].astype(vals[g].dtype), vals[g]))
        off += nk
    out = functools.reduce(jnp.add, outs)
    return out.reshape(b, s_len, ATTN_OUT_WIDTH)


def merge_branches(a, b_attn, gates, w_attn_out, w_out):
    g_a, g_b = jnp.split(gates, 2, axis=-1)
    return (g_a * a + g_b * (b_attn @ w_attn_out)) @ w_out


def setup_inputs(seed: int = 0) -> dict:
    key = jax.random.key(seed)
    ks = iter(jax.random.split(key, 40))
    nrm = lambda shape, scale: scale * jax.random.normal(next(ks), shape, jnp.float32)
    gain = lambda shape: 1.0 + nrm(shape, 0.02)
    L = DEPTH
    lens = [min(w, PAST_LEN) for w in WINDOWS]
    cshape = lambda n: (L, DEC_BATCH, n, HEADS_PER_GROUP, HEAD_DIM)
    return {
        "x_prompt": nrm((BATCH, SEQ, D_MODEL), 1.0),
        "x_sample": nrm((DEC_BATCH, DEC_SEQ, D_MODEL), 1.0),
        "cache_k_w128": nrm(cshape(lens[0]), 1.0),
        "cache_v_w128": nrm(cshape(lens[0]), 1.0),
        "cache_k_w512": nrm(cshape(lens[1]), 1.0),
        "cache_v_w512": nrm(cshape(lens[1]), 1.0),
        "cache_k_w2048": nrm(cshape(lens[2]), 1.0),
        "cache_v_w2048": nrm(cshape(lens[2]), 1.0),
        "state_conv": nrm((L, DEC_BATCH, CONV_WIDTH - 1, CONV_CH), 0.5),
        "ffn1_norm": gain((L, D_MODEL)),
        "ffn1_w_in": nrm((L, D_MODEL, 2 * D_FF), D_MODEL ** -0.5),
        "ffn1_w_out": nrm((L, D_FF, D_MODEL), D_FF ** -0.5),
        "mix_norm": gain((L, D_MODEL)),
        "w_in": nrm((L, D_MODEL, IN_WIDTH), D_MODEL ** -0.5),
        "b_gate": nrm((L, 2 * D_MODEL), 0.02),
        "q_norm": gain((L, N_HEADS, HEAD_DIM)),
        "k_norm": gain((L, N_HEADS, HEAD_DIM)),
        "conv_w": nrm((L, CONV_WIDTH, CONV_CH), CONV_WIDTH ** -0.5),
        "conv_b": nrm((L, CONV_CH), 0.02),
        "conv_ln_g": gain((L, CONV_CH)),
        "conv_ln_b": nrm((L, CONV_CH), 0.02),
        "w_conv_out": nrm((L, CONV_CH, D_MODEL), CONV_CH ** -0.5),
        "w_attn_out": nrm((L, ATTN_OUT_WIDTH, D_MODEL), ATTN_OUT_WIDTH ** -0.5),
        "w_out": nrm((L, D_MODEL, D_MODEL), D_MODEL ** -0.5),
        "ffn2_norm": gain((L, D_MODEL)),
        "ffn2_w_in": nrm((L, D_MODEL, 2 * D_FF), D_MODEL ** -0.5),
        "ffn2_w_out": nrm((L, D_FF, D_MODEL), D_FF ** -0.5),
    }


def reference(x_prompt, x_sample, cache_k_w128, cache_v_w128, cache_k_w512, cache_v_w512, cache_k_w2048, cache_v_w2048,
              state_conv, ffn1_norm, ffn1_w_in, ffn1_w_out, mix_norm, w_in, b_gate, q_norm, k_norm, conv_w, conv_b,
              conv_ln_g, conv_ln_b, w_conv_out, w_attn_out, w_out, ffn2_norm, ffn2_w_in, ffn2_w_out):
    t = x_prompt.shape[1]
    s_len = x_sample.shape[1]
    pos_p = jnp.arange(t)
    pos_s = PAST_LEN + jnp.arange(s_len)
    cache_k = (cache_k_w128, cache_k_w512, cache_k_w2048)
    cache_v = (cache_v_w128, cache_v_w512, cache_v_w2048)
    new_k_p = [[] for _ in range(N_GROUPS)]
    new_v_p = [[] for _ in range(N_GROUPS)]
    new_k_s = [[] for _ in range(N_GROUPS)]
    new_v_s = [[] for _ in range(N_GROUPS)]
    new_conv_p, new_conv_s = [], []
    xp, xs = x_prompt, x_sample
    for l in range(DEPTH):
        xp = xp + 0.5 * swiglu_ffn(rmsnorm(xp, ffn1_norm[l]), ffn1_w_in[l], ffn1_w_out[l])
        xs = xs + 0.5 * swiglu_ffn(rmsnorm(xs, ffn1_norm[l]), ffn1_w_in[l], ffn1_w_out[l])
        hp = rmsnorm(xp, mix_norm[l])
        hs = rmsnorm(xs, mix_norm[l])
        up, qp, kp, vp, gp = mix_project(hp, pos_p, w_in[l], b_gate[l], q_norm[l], k_norm[l])
        us, qs, ks, vs, gs = mix_project(hs, pos_s, w_in[l], b_gate[l], q_norm[l], k_norm[l])
        conv_in_p = jnp.concatenate([jnp.zeros((up.shape[0], CONV_WIDTH - 1, CONV_CH), up.dtype), up], axis=1)
        conv_in_s = jnp.concatenate([state_conv[l].astype(us.dtype), us], axis=1)
        a_p = conv_branch(conv_in_p, conv_w[l], conv_b[l], conv_ln_g[l], conv_ln_b[l], w_conv_out[l])
        a_s = conv_branch(conv_in_s, conv_w[l], conv_b[l], conv_ln_g[l], conv_ln_b[l], w_conv_out[l])
        new_conv_p.append(conv_in_p[:, -(CONV_WIDTH - 1):])
        new_conv_s.append(conv_in_s[:, -(CONV_WIDTH - 1):])
        b_p = dilated_attention_prompt(qp, kp, vp)
        k_full, v_full = [], []
        for g, w in enumerate(WINDOWS):
            hsl = slice(g * HEADS_PER_GROUP, (g + 1) * HEADS_PER_GROUP)
            kf = jnp.concatenate([cache_k[g][l].astype(ks.dtype), ks[:, :, hsl]], axis=1)
            vf = jnp.concatenate([cache_v[g][l].astype(vs.dtype), vs[:, :, hsl]], axis=1)
            k_full.append(kf)
            v_full.append(vf)
            keep_p = min(w, t)
            keep_s = min(w, PAST_LEN + s_len)
            new_k_p[g].append(kp[:, t - keep_p:, hsl])
            new_v_p[g].append(vp[:, t - keep_p:, hsl])
            new_k_s[g].append(kf[:, kf.shape[1] - keep_s:])
            new_v_s[g].append(vf[:, vf.shape[1] - keep_s:])
        b_s = dilated_attention_sample(qs, k_full, v_full)
        xp = xp + merge_branches(a_p, b_p, gp, w_attn_out[l], w_out[l])
        xs = xs + merge_branches(a_s, b_s, gs, w_attn_out[l], w_out[l])
        xp = xp + 0.5 * swiglu_ffn(rmsnorm(xp, ffn2_norm[l]), ffn2_w_in[l], ffn2_w_out[l])
        xs = xs + 0.5 * swiglu_ffn(rmsnorm(xs, ffn2_norm[l]), ffn2_w_in[l], ffn2_w_out[l])
    return (xp, xs,
            jnp.stack(new_k_p[0]), jnp.stack(new_v_p[0]), jnp.stack(new_k_p[1]), jnp.stack(new_v_p[1]),
            jnp.stack(new_k_p[2]), jnp.stack(new_v_p[2]), jnp.stack(new_conv_p),
            jnp.stack(new_k_s[0]), jnp.stack(new_v_s[0]), jnp.stack(new_k_s[1]), jnp.stack(new_v_s[1]),
            jnp.stack(new_k_s[2]), jnp.stack(new_v_s[2]), jnp.stack(new_conv_s))
```

```python
import functools

import jax
import jax.numpy as jnp
from jax import lax
from jax.experimental import pallas as pl
from jax.experimental.pallas import tpu as pltpu

F32 = jnp.float32
BF16 = jnp.bfloat16

HEAD_DIM = 64
WINDOWS = (128, 512, 2048)
DILATIONS = (1, 4, 16)
N_GROUPS = 3
HEADS_PER_GROUP = 4
GROUP_W = HEADS_PER_GROUP * HEAD_DIM
ATTN_W = N_GROUPS * GROUP_W
BLK = WINDOWS[0] // DILATIONS[0]
assert all(w // d == BLK for w, d in zip(WINDOWS, DILATIONS))
ATTN_TILE = max(WINDOWS)
UNITS = ATTN_TILE // BLK
ATTN_SCALE = HEAD_DIM ** -0.5
ROT_DIM = HEAD_DIM // 4
ROPE_THETA = 500000.0
CONV_WIDTH = 31
CONV_PAD = 32
NORM_EPS = 1e-6
PAST_LEN = 2048
NEG = -1e30
LANES = 128
FF_CHUNK = 256
TOKEN_TILE = 256
VMEM_LIMIT = 56 * 1024 * 1024


def _resident(shape):
    nd = len(shape)
    return pl.BlockSpec(shape, lambda *_: (0,) * nd, pipeline_mode=pl.Buffered(1))


def _rms(x, g):
    ms = jnp.mean(x * x, axis=-1, keepdims=True)
    return x * lax.rsqrt(ms + NORM_EPS) * g


def _swiglu(h, wgu_ref, wo_ref):
    nc = wo_ref.shape[0] // FF_CHUNK
    acc = None
    for c in range(nc):
        gu = jnp.dot(h, wgu_ref[:, c * 2 * FF_CHUNK:(c + 1) * 2 * FF_CHUNK], preferred_element_type=F32)
        g = gu[:, :FF_CHUNK]
        u = gu[:, FF_CHUNK:]
        a = (g * jax.nn.sigmoid(g) * u).astype(BF16)
        d = jnp.dot(a, wo_ref[c * FF_CHUNK:(c + 1) * FF_CHUNK, :], preferred_element_type=F32)
        acc = d if acc is None else acc + d
    return acc


def _head_norm(y, hm_ref, gain):
    parts = []
    for g in range(N_GROUPS):
        yg = y[:, g * GROUP_W:(g + 1) * GROUP_W]
        ms = jnp.dot((yg * yg).astype(BF16), hm_ref[...], preferred_element_type=F32)
        parts.append(yg * lax.rsqrt(ms + NORM_EPS))
    return jnp.concatenate(parts, axis=1) * gain


def _rope(y, c, s1, s2):
    half = ROT_DIM // 2
    parts = []
    for i in range(y.shape[1] // LANES):
        yc = y[:, i * LANES:(i + 1) * LANES]
        parts.append(yc * c + pltpu.roll(yc, half, 1) * s1 + pltpu.roll(yc, LANES - half, 1) * s2)
    return jnp.concatenate(parts, axis=1)


def _layernorm_silu(y, g, b):
    mu = jnp.mean(y, axis=-1, keepdims=True)
    yc = y - mu
    var = jnp.mean(yc * yc, axis=-1, keepdims=True)
    yn = yc * lax.rsqrt(var + NORM_EPS) * g + b
    return yn * jax.nn.sigmoid(yn)


def _ffn_mix_body(x_ref, rc_ref, rs1_ref, rs2_ref, g1_ref, wgu_ref, wo_ref, gm_ref, win_ref, hm_ref, qg_ref, kg_ref,
                  x1_ref, q_ref, k_ref, v_ref, k32_ref, v32_ref):
    x = x_ref[...]
    h = _rms(x, g1_ref[...]).astype(BF16)
    x1 = x + 0.5 * _swiglu(h, wgu_ref, wo_ref)
    x1_ref[...] = x1
    h2 = _rms(x1, gm_ref[...]).astype(BF16)
    z = jnp.dot(h2, win_ref[...], preferred_element_type=F32)
    cc = (win_ref.shape[1] - 3 * ATTN_W) // 2
    u = z[:, :cc] * jax.nn.sigmoid(z[:, cc:2 * cc])
    q = z[:, 2 * cc:2 * cc + ATTN_W]
    k = z[:, 2 * cc + ATTN_W:2 * cc + 2 * ATTN_W]
    v = z[:, 2 * cc + 2 * ATTN_W:]
    c, s1, s2 = rc_ref[...], rs1_ref[...], rs2_ref[...]
    q = _rope(_head_norm(q, hm_ref, qg_ref[...]), c, s1, s2)
    k = _rope(_head_norm(k, hm_ref, kg_ref[...]), c, s1, s2)
    q_ref[...] = q.astype(BF16)
    k_ref[...] = k.astype(BF16)
    v_ref[...] = v.astype(BF16)
    k32_ref[...] = k
    v32_ref[...] = v
    return u


def _ffn_mix_conv_kernel(x_ref, rc_ref, rs1_ref, rs2_ref, g1_ref, wgu_ref, wo_ref, gm_ref, win_ref, hm_ref, qg_ref,
                         kg_ref, cw_ref, cb_ref, lng_ref, lnb_ref,
                         x1_ref, q_ref, k_ref, v_ref, k32_ref, v32_ref, a_ref, utail_ref, ext_ref, *, tiles_per_seq):
    tm = x_ref.shape[0]
    u = _ffn_mix_body(x_ref, rc_ref, rs1_ref, rs2_ref, g1_ref, wgu_ref, wo_ref, gm_ref, win_ref, hm_ref, qg_ref,
                      kg_ref, x1_ref, q_ref, k_ref, v_ref, k32_ref, v32_ref)

    @pl.when(pl.program_id(0) % tiles_per_seq == 0)
    def _():
        ext_ref[0:CONV_PAD, :] = jnp.zeros((CONV_PAD, ext_ref.shape[1]), F32)

    ext_ref[CONV_PAD:CONV_PAD + tm, :] = u
    first = CONV_PAD - (CONV_WIDTH - 1)
    rows = 64
    for r0 in range(0, tm, rows):
        acc = None
        for j in range(CONV_WIDTH):
            term = ext_ref[r0 + first + j:r0 + first + j + rows, :] * cw_ref[j:j + 1, :]
            acc = term if acc is None else acc + term
        y = acc + cb_ref[...]
        a_ref[r0:r0 + rows, :] = _layernorm_silu(y, lng_ref[...], lnb_ref[...]).astype(BF16)
    tail = ext_ref[tm:tm + CONV_PAD, :]
    utail_ref[...] = tail
    ext_ref[0:CONV_PAD, :] = tail


def _ffn_mix_plain_kernel(x_ref, rc_ref, rs1_ref, rs2_ref, g1_ref, wgu_ref, wo_ref, gm_ref, win_ref, hm_ref, qg_ref,
                          kg_ref, x1_ref, q_ref, k_ref, v_ref, k32_ref, v32_ref, u_ref):
    u_ref[...] = _ffn_mix_body(x_ref, rc_ref, rs1_ref, rs2_ref, g1_ref, wgu_ref, wo_ref, gm_ref, win_ref, hm_ref,
                               qg_ref, kg_ref, x1_ref, q_ref, k_ref, v_ref, k32_ref, v32_ref)


def _ffn_mix(x, rope, w, *, conv, seq_len):
    n, d = x.shape
    tm = TOKEN_TILE
    assert n % tm == 0 and seq_len % tm == 0
    tiles_per_seq = seq_len // tm
    rope_tiles = rope[0].shape[0] // tm
    cc = w["conv_w"].shape[1]
    tok = lambda width: pl.BlockSpec((tm, width), lambda i: (i, 0))
    rope_spec = pl.BlockSpec((tm, LANES), lambda i: (i % rope_tiles, 0))
    weights = [w["ffn1_norm"], w["ffn1_wgu"], w["ffn1_wo"], w["mix_norm"], w["w_in_a"], w["head_mean"],
               w["q_gain"], w["k_gain"]]
    in_specs = [tok(d), rope_spec, rope_spec, rope_spec] + [_resident(a.shape) for a in weights]
    out_shape = [jax.ShapeDtypeStruct((n, d), F32)] + [jax.ShapeDtypeStruct((n, ATTN_W), BF16)] * 3 \
        + [jax.ShapeDtypeStruct((n, ATTN_W), F32)] * 2
    out_specs = [tok(d)] + [tok(ATTN_W)] * 5
    scratch = []
    if conv:
        conv_w = [w["conv_w"], w["conv_b"], w["conv_ln_g"], w["conv_ln_b"]]
        weights = weights + conv_w
        in_specs += [_resident(a.shape) for a in conv_w]
        out_shape += [jax.ShapeDtypeStruct((n, cc), BF16), jax.ShapeDtypeStruct((n // seq_len, CONV_PAD, cc), F32)]
        out_specs += [tok(cc), pl.BlockSpec((None, CONV_PAD, cc), lambda i: (i // tiles_per_seq, 0, 0))]
        scratch = [pltpu.VMEM((tm + CONV_PAD, cc), F32)]
        kern = functools.partial(_ffn_mix_conv_kernel, tiles_per_seq=tiles_per_seq)
    else:
        out_shape += [jax.ShapeDtypeStruct((n, cc), F32)]
        out_specs += [tok(cc)]
        kern = _ffn_mix_plain_kernel
    return pl.pallas_call(
        kern, grid=(n // tm,), in_specs=in_specs, out_specs=out_specs, out_shape=out_shape, scratch_shapes=scratch,
        compiler_params=pltpu.CompilerParams(dimension_semantics=("arbitrary",), vmem_limit_bytes=VMEM_LIMIT),
        name="ffn_mix_conv" if conv else "ffn_mix",
    )(x, *rope, *weights)


def _attn_unit(q, kp, kc, vp, vc, first):
    row = lax.broadcasted_iota(jnp.int32, (BLK, BLK), 0)
    col = lax.broadcasted_iota(jnp.int32, (BLK, BLK), 1)
    bias_p = jnp.where(col >= row, 0.0, NEG) + jnp.where(first, NEG, 0.0)
    bias_c = jnp.where(col <= row, 0.0, NEG)
    head = lax.broadcasted_iota(jnp.int32, (1, GROUP_W), 1) // HEAD_DIM
    contract_last = (((1,), (1,)), ((), ()))
    out = jnp.zeros((BLK, GROUP_W), F32)
    lse = jnp.zeros((BLK, GROUP_W), F32)
    zero = jnp.zeros((), BF16)
    for h in range(HEADS_PER_GROUP):
        hm = head == h
        qh = jnp.where(hm, q, zero)
        sp = lax.dot_general(qh, kp, contract_last, preferred_element_type=F32) * ATTN_SCALE + bias_p
        sc = lax.dot_general(qh, kc, contract_last, preferred_element_type=F32) * ATTN_SCALE + bias_c
        m = jnp.maximum(jnp.max(sp, axis=1, keepdims=True), jnp.max(sc, axis=1, keepdims=True))
        pp = jnp.exp(sp - m)
        pc = jnp.exp(sc - m)
        l = jnp.sum(pp, axis=1, keepdims=True) + jnp.sum(pc, axis=1, keepdims=True)
        o = jnp.dot(pp.astype(BF16), jnp.where(hm, vp, zero), preferred_element_type=F32) \
            + jnp.dot(pc.astype(BF16), jnp.where(hm, vc, zero), preferred_element_type=F32)
        out = out + o / l
        lse = lse + jnp.where(hm, m + jnp.log(l), 0.0)
    return out, lse


def _attn_prompt_kernel(*refs):
    ins, outs = refs[:5 * N_GROUPS], refs[5 * N_GROUPS:]
    t = pl.program_id(1)
    u = pl.program_id(2)
    for g in range(N_GROUPS):
        q_ref, kc_ref, kp_ref, vc_ref, vp_ref = ins[5 * g:5 * g + 5]
        d = DILATIONS[g]
        first = (t * (UNITS // d) + u // d) == 0
        o, lse = _attn_unit(q_ref[...], kp_ref[...], kc_ref[...], vp_ref[...], vc_ref[...], first)
        outs[2 * g][...] = o.astype(BF16)
        outs[2 * g + 1][...] = lse


def _attn_prompt(q, k, v, batch, seq_len):
    assert seq_len % ATTN_TILE == 0
    args, in_specs, out_shape, out_specs = [], [], [], []
    for g in range(N_GROUPS):
        d = DILATIONS[g]
        nb = UNITS // d
        cur = lambda b, t, u, d=d, nb=nb, g=g: (b, t * nb + u // d, (u % d) * N_GROUPS + g)
        prev = lambda b, t, u, d=d, nb=nb, g=g: (b, jnp.maximum(t * nb + u // d - 1, 0), (u % d) * N_GROUPS + g)
        outi = lambda b, t, u, d=d, nb=nb: (b, t * nb + u // d, u % d)
        view = lambda a, d=d: a.reshape(batch, seq_len // d, d * ATTN_W)
        blk = lambda im: pl.BlockSpec((None, BLK, GROUP_W), im)
        args += [view(q), view(k), view(k), view(v), view(v)]
        in_specs += [blk(cur), blk(cur), blk(prev), blk(cur), blk(prev)]
        out_shape += [jax.ShapeDtypeStruct((batch, seq_len // d, d * GROUP_W), BF16),
                      jax.ShapeDtypeStruct((batch, seq_len // d, d * GROUP_W), F32)]
        out_specs += [blk(outi), blk(outi)]
    res = pl.pallas_call(
        _attn_prompt_kernel, grid=(batch, seq_len // ATTN_TILE, UNITS), in_specs=in_specs, out_specs=out_specs,
        out_shape=out_shape,
        compiler_params=pltpu.CompilerParams(dimension_semantics=("arbitrary",) * 3),
        name="attn_prompt",
    )(*args)
    return [r.reshape(batch * seq_len, GROUP_W) for r in res]


def _attn_sample_kernel(q_ref, kn_ref, vn_ref, kc0_ref, vc0_ref, kc1_ref, vc1_ref, kc2_ref, vc2_ref, seg_ref,
                        exp_ref, ext_ref, cw_ref, cb_ref, lng_ref, lnb_ref, b_ref, a_ref, *, s_len):
    q = q_ref[...]
    kn = kn_ref[...]
    vn = vn_ref[...]
    seg = seg_ref[...]
    ex = exp_ref[...]
    rows_c = lax.broadcasted_iota(jnp.int32, (BLK, LANES), 0)
    rows_n = lax.broadcasted_iota(jnp.int32, (kn.shape[0], LANES), 0)
    kc_refs = (kc0_ref, kc1_ref, kc2_ref)
    vc_refs = (vc0_ref, vc1_ref, vc2_ref)
    b_ref[...] = jnp.zeros(b_ref.shape, F32)
    a_ref[...] = jnp.zeros(a_ref.shape, F32)
    for s in range(s_len):
        sc, sn = [], []
        for g in range(N_GROUPS):
            gs = slice(g * GROUP_W, (g + 1) * GROUP_W)
            rs = slice(0, GROUP_W) if g == 0 else slice(s * GROUP_W, (s + 1) * GROUP_W)
            qsg = q[s:s + 1, gs]
            s_c = jnp.dot((kc_refs[g][:, rs] * qsg).astype(BF16), seg, preferred_element_type=F32) * ATTN_SCALE
            s_n = jnp.dot((kn[:, gs] * qsg).astype(BF16), seg, preferred_element_type=F32) * ATTN_SCALE
            if g == 0:
                s_c = jnp.where(rows_c >= s, s_c, NEG)
                s_n = jnp.where(rows_n <= s, s_n, NEG)
            else:
                s_n = jnp.where(rows_n == s, s_n, NEG)
            sc.append(s_c)
            sn.append(s_n)
        m = None
        for a in sc + sn:
            am = jnp.max(a, axis=0, keepdims=True)
            m = am if m is None else jnp.maximum(m, am)
        num = jnp.zeros((1, GROUP_W), F32)
        den = jnp.zeros((1, GROUP_W), F32)
        for g in range(N_GROUPS):
            gs = slice(g * GROUP_W, (g + 1) * GROUP_W)
            rs = slice(0, GROUP_W) if g == 0 else slice(s * GROUP_W, (s + 1) * GROUP_W)
            pe_c = jnp.dot(jnp.exp(sc[g] - m).astype(BF16), ex, preferred_element_type=F32)
            pe_n = jnp.dot(jnp.exp(sn[g] - m).astype(BF16), ex, preferred_element_type=F32)
            num = num + jnp.sum(pe_c * vc_refs[g][:, rs], axis=0, keepdims=True) \
                + jnp.sum(pe_n * vn[:, gs], axis=0, keepdims=True)
            den = den + jnp.sum(pe_c, axis=0, keepdims=True) + jnp.sum(pe_n, axis=0, keepdims=True)
        b_ref[s:s + 1, :] = num / den
        y = jnp.sum(ext_ref[s:s + CONV_PAD, :] * cw_ref[...], axis=0, keepdims=True) + cb_ref[...]
        a_ref[s:s + 1, :] = _layernorm_silu(y, lng_ref[...], lnb_ref[...])


def _attn_sample(q8, kn8, vn8, caches_k, caches_v, ext, w, s_len):
    db = q8.shape[0]
    rows = q8.shape[1]
    cc = ext.shape[2]
    assert s_len <= DILATIONS[1] and s_len <= rows
    args = [q8, kn8, vn8]
    in_specs = [pl.BlockSpec((None, rows, ATTN_W), lambda b: (b, 0, 0))] * 3
    for g in range(N_GROUPS):
        d = DILATIONS[g]
        for c in (caches_k[g], caches_v[g]):
            assert c.shape == (db, WINDOWS[g], GROUP_W)
            args.append(c.reshape(db, BLK, d * GROUP_W))
            in_specs.append(pl.BlockSpec((None, BLK, min(d, DILATIONS[1]) * GROUP_W), lambda b: (b, 0, 0)))
    consts = [w["seg"], w["expand"]]
    conv_w = [w["conv_w"], w["conv_b"], w["conv_ln_g"], w["conv_ln_b"]]
    args += consts + [ext] + conv_w
    in_specs += [_resident(a.shape) for a in consts] + [pl.BlockSpec((None,) + ext.shape[1:], lambda b: (b, 0, 0))] \
        + [_resident(a.shape) for a in conv_w]
    return pl.pallas_call(
        functools.partial(_attn_sample_kernel, s_len=s_len), grid=(db,), in_specs=in_specs,
        out_specs=[pl.BlockSpec((None, rows, GROUP_W), lambda b: (b, 0, 0)),
                   pl.BlockSpec((None, rows, cc), lambda b: (b, 0, 0))],
        out_shape=[jax.ShapeDtypeStruct((db, rows, GROUP_W), F32), jax.ShapeDtypeStruct((db, rows, cc), F32)],
        compiler_params=pltpu.CompilerParams(dimension_semantics=("arbitrary",)),
        name="attn_sample",
    )(*args)


def _merge_ffn_body(x1_ref, a_ref, battn, gm_ref, wg_ref, bg_ref, wco_ref, wao_ref, wout_ref, g2_ref, wgu_ref,
                    wo_ref, y_ref):
    x1 = x1_ref[...]
    d = x1.shape[1]
    h = _rms(x1, gm_ref[...]).astype(BF16)
    gates = jax.nn.sigmoid(jnp.dot(h, wg_ref[...], preferred_element_type=F32) + bg_ref[...])
    a = jnp.dot(a_ref[...], wco_ref[...], preferred_element_type=F32)
    b = jnp.dot(battn.astype(BF16), wao_ref[...], preferred_element_type=F32)
    mix = (gates[:, :d] * a + gates[:, d:] * b).astype(BF16)
    y = x1 + jnp.dot(mix, wout_ref[...], preferred_element_type=F32)
    h2 = _rms(y, g2_ref[...]).astype(BF16)
    y_ref[...] = y + 0.5 * _swiglu(h2, wgu_ref, wo_ref)


def _merge_ffn_groups_kernel(x1_ref, a_ref, o0_ref, o1_ref, o2_ref, l0_ref, l1_ref, l2_ref, *rest):
    lses = [l0_ref[...], l1_ref[...], l2_ref[...]]
    outs = [o0_ref[...], o1_ref[...], o2_ref[...]]
    mx = jnp.maximum(jnp.maximum(lses[0], lses[1]), lses[2])
    num = None
    den = None
    for o, l in zip(outs, lses):
        wgt = jnp.exp(l - mx)
        num = wgt * o if num is None else num + wgt * o
        den = wgt if den is None else den + wgt
    _merge_ffn_body(x1_ref, a_ref, num / den, *rest)


def _merge_ffn_direct_kernel(x1_ref, a_ref, b_ref, *rest):
    _merge_ffn_body(x1_ref, a_ref, b_ref[...], *rest)


def _merge_ffn(x1, a_pre, attn, w):
    n, d = x1.shape
    tm = TOKEN_TILE
    assert n % tm == 0
    tok = lambda width: pl.BlockSpec((tm, width), lambda i: (i, 0))
    weights = [w["mix_norm"], w["w_in_g"], w["b_gate"], w["w_conv_out"], w["w_attn_out"], w["w_out"], w["ffn2_norm"],
               w["ffn2_wgu"], w["ffn2_wo"]]
    kern = _merge_ffn_groups_kernel if len(attn) > 1 else _merge_ffn_direct_kernel
    return pl.pallas_call(
        kern, grid=(n // tm,),
        in_specs=[tok(d), tok(a_pre.shape[1])] + [tok(GROUP_W)] * len(attn) + [_resident(a.shape) for a in weights],
        out_specs=tok(d), out_shape=jax.ShapeDtypeStruct((n, d), F32),
        compiler_params=pltpu.CompilerParams(dimension_semantics=("arbitrary",), vmem_limit_bytes=VMEM_LIMIT),
        name="merge_ffn_groups" if len(attn) > 1 else "merge_ffn",
    )(x1, a_pre, *attn, *weights)


def _rope_tables(pos):
    half = ROT_DIM // 2
    inv = jnp.float32(ROPE_THETA) ** (-jnp.arange(half, dtype=F32) * (2.0 / ROT_DIM))
    ang = pos.astype(F32)[:, None] * inv[None, :]
    cos, sin = jnp.cos(ang), jnp.sin(ang)
    n = pos.shape[0]
    rest = HEAD_DIM - ROT_DIM
    zh = jnp.zeros((n, half), F32)
    c = jnp.concatenate([cos, cos, jnp.ones((n, rest), F32)], axis=1)
    s1 = jnp.concatenate([zh, sin, jnp.zeros((n, rest), F32)], axis=1)
    s2 = jnp.concatenate([-sin, zh, jnp.zeros((n, rest), F32)], axis=1)
    rep = LANES // HEAD_DIM
    return tuple(jnp.tile(t, (1, rep)) for t in (c, s1, s2))


def _interleave_gate_up(w_in):
    d, f2 = w_in.shape
    f = f2 // 2
    assert f % FF_CHUNK == 0
    g = w_in[:, :f].reshape(d, f // FF_CHUNK, 1, FF_CHUNK)
    u = w_in[:, f:].reshape(d, f // FF_CHUNK, 1, FF_CHUNK)
    return jnp.concatenate([g, u], axis=2).reshape(d, f2).astype(BF16)


def _prepare_weights(l, ffn1_norm, ffn1_w_in, ffn1_w_out, mix_norm, w_in, b_gate, q_norm, k_norm, conv_w, conv_b,
                     conv_ln_g, conv_ln_b, w_conv_out, w_attn_out, w_out, ffn2_norm, ffn2_w_in, ffn2_w_out):
    d = w_in.shape[1]
    cc = conv_w.shape[2]
    n_a = 2 * cc + 3 * ATTN_W
    lane = jnp.arange(GROUP_W)
    same_head = (lane[:, None] // HEAD_DIM) == (lane[None, :] // HEAD_DIM)
    head_of = lane // HEAD_DIM
    row = lambda a: a.reshape(1, -1).astype(F32)
    return {
        "ffn1_norm": row(ffn1_norm[l]), "ffn1_wgu": _interleave_gate_up(ffn1_w_in[l]),
        "ffn1_wo": ffn1_w_out[l].astype(BF16),
        "mix_norm": row(mix_norm[l]), "w_in_a": w_in[l][:, :n_a].astype(BF16), "w_in_g": w_in[l][:, n_a:].astype(BF16),
        "b_gate": row(b_gate[l]),
        "head_mean": jnp.where(same_head, 1.0 / HEAD_DIM, 0.0).astype(BF16),
        "q_gain": row(q_norm[l]), "k_gain": row(k_norm[l]),
        "conv_w": jnp.concatenate([conv_w[l], jnp.zeros((CONV_PAD - CONV_WIDTH, cc), F32)], axis=0),
        "conv_b": row(conv_b[l]), "conv_ln_g": row(conv_ln_g[l]), "conv_ln_b": row(conv_ln_b[l]),
        "w_conv_out": w_conv_out[l].astype(BF16), "w_attn_out": w_attn_out[l].astype(BF16),
        "w_out": w_out[l].astype(BF16),
        "ffn2_norm": row(ffn2_norm[l]), "ffn2_wgu": _interleave_gate_up(ffn2_w_in[l]),
        "ffn2_wo": ffn2_w_out[l].astype(BF16),
        "seg": (head_of[:, None] == jnp.arange(LANES)[None, :]).astype(BF16),
        "expand": (jnp.arange(LANES)[:, None] == head_of[None, :]).astype(BF16),
    }


def kernel(x_prompt, x_sample, cache_k_w128, cache_v_w128, cache_k_w512, cache_v_w512, cache_k_w2048, cache_v_w2048, state_conv, ffn1_norm, ffn1_w_in, ffn1_w_out, mix_norm, w_in, b_gate, q_norm, k_norm, conv_w, conv_b, conv_ln_g, conv_ln_b, w_conv_out, w_attn_out, w_out, ffn2_norm, ffn2_w_in, ffn2_w_out):
    batch, t, d = x_prompt.shape
    db, s_len, _ = x_sample.shape
    depth = w_in.shape[0]
    cc = conv_w.shape[2]
    cache_k = (cache_k_w128, cache_k_w512, cache_k_w2048)
    cache_v = (cache_v_w128, cache_v_w512, cache_v_w2048)
    for g, wdw in enumerate(WINDOWS):
        assert cache_k[g].shape == (depth, db, wdw, HEADS_PER_GROUP, HEAD_DIM), "sample caches must hold a full window"
        assert t >= wdw
    rope_p = _rope_tables(jnp.arange(t))
    rope_s = _rope_tables(PAST_LEN + jnp.arange(db * s_len) % s_len)
    pad_rows = 8

    xp = x_prompt.reshape(batch * t, d)
    xs = x_sample.reshape(db * s_len, d)
    new_k_p = [[] for _ in range(N_GROUPS)]
    new_v_p = [[] for _ in range(N_GROUPS)]
    new_k_s = [[] for _ in range(N_GROUPS)]
    new_v_s = [[] for _ in range(N_GROUPS)]
    new_conv_p, new_conv_s = [], []
    for l in range(depth):
        w = _prepare_weights(l, ffn1_norm, ffn1_w_in, ffn1_w_out, mix_norm, w_in, b_gate, q_norm, k_norm, conv_w,
                             conv_b, conv_ln_g, conv_ln_b, w_conv_out, w_attn_out, w_out, ffn2_norm, ffn2_w_in,
                             ffn2_w_out)
        x1p, qp, kp, vp, k32p, v32p, a_p, utail = _ffn_mix(xp, rope_p, w, conv=True, seq_len=t)
        attn_p = _attn_prompt(qp, kp, vp, batch, t)
        xp = _merge_ffn(x1p, a_p, [attn_p[0], attn_p[2], attn_p[4], attn_p[1], attn_p[3], attn_p[5]], w)
        k32p = k32p.reshape(batch, t, ATTN_W)
        v32p = v32p.reshape(batch, t, ATTN_W)
        new_conv_p.append(utail[:, CONV_PAD - (CONV_WIDTH - 1):, :])
        x1s, qs, _, _, k32s, v32s, us = _ffn_mix(xs, rope_s, w, conv=False, seq_len=TOKEN_TILE)
        pad = lambda a: jnp.pad(a.astype(F32).reshape(db, s_len, -1), ((0, 0), (0, pad_rows - s_len), (0, 0)))
        ck = [cache_k[g][l].reshape(db, WINDOWS[g], GROUP_W) for g in range(N_GROUPS)]
        cv = [cache_v[g][l].reshape(db, WINDOWS[g], GROUP_W) for g in range(N_GROUPS)]
        conv_in = jnp.concatenate([state_conv[l], us.reshape(db, s_len, cc)], axis=1)
        ext = jnp.pad(conv_in, ((0, 0), (0, CONV_PAD + pad_rows - conv_in.shape[1]), (0, 0)))
        b_s, a_s = _attn_sample(pad(qs), pad(k32s), pad(v32s), ck, cv, ext, w, s_len)
        xs = _merge_ffn(x1s, a_s[:, :s_len].reshape(db * s_len, cc).astype(BF16),
                        [b_s[:, :s_len].reshape(db * s_len, GROUP_W)], w)
        new_conv_s.append(conv_in[:, s_len:, :])
        k32s = k32s.reshape(db, s_len, ATTN_W)
        v32s = v32s.reshape(db, s_len, ATTN_W)
        for g, wdw in enumerate(WINDOWS):
            gs = slice(g * GROUP_W, (g + 1) * GROUP_W)
            hd = (HEADS_PER_GROUP, HEAD_DIM)
            new_k_p[g].append(k32p[:, t - wdw:, gs].reshape(batch, wdw, *hd))
            new_v_p[g].append(v32p[:, t - wdw:, gs].reshape(batch, wdw, *hd))
            new_k_s[g].append(jnp.concatenate([ck[g][:, s_len:], k32s[:, :, gs]], axis=1).reshape(db, wdw, *hd))
            new_v_s[g].append(jnp.concatenate([cv[g][:, s_len:], v32s[:, :, gs]], axis=1).reshape(db, wdw, *hd))
    return (xp.reshape(batch, t, d), xs.reshape(db, s_len, d),
            jnp.stack(new_k_p[0]), jnp.stack(new_v_p[0]), jnp.stack(new_k_p[1]), jnp.stack(new_v_p[1]),
            jnp.stack(new_k_p[2]), jnp.stack(new_v_p[2]), jnp.stack(new_conv_p),
            jnp.stack(new_k_s[0]), jnp.stack(new_v_s[0]), jnp.stack(new_k_s[1]), jnp.stack(new_v_s[1]),
            jnp.stack(new_k_s[2]), jnp.stack(new_v_s[2]), jnp.stack(new_conv_s))
```

```python
import functools

import jax
import jax.numpy as jnp
from jax import lax
from jax.experimental import pallas as pl
from jax.experimental.pallas import tpu as pltpu

F32 = jnp.float32
BF16 = jnp.bfloat16

HEAD_DIM = 64
WINDOWS = (128, 512, 2048)
DILATIONS = (1, 4, 16)
N_GROUPS = 3
HEADS_PER_GROUP = 4
GROUP_W = HEADS_PER_GROUP * HEAD_DIM
ATTN_W = N_GROUPS * GROUP_W
BLK = WINDOWS[0] // DILATIONS[0]
assert all(w // d == BLK for w, d in zip(WINDOWS, DILATIONS))
ATTN_TILE = max(WINDOWS)
UNITS = ATTN_TILE // BLK
ATTN_SCALE = HEAD_DIM ** -0.5
ROT_DIM = HEAD_DIM // 4
ROPE_THETA = 500000.0
CONV_WIDTH = 31
CONV_PAD = 32
NORM_EPS = 1e-6
PAST_LEN = 2048
NEG = -1e30
LANES = 128
SUBLANES = 8
FF_CHUNK = 256
TOKEN_TILE = 256
SAMPLE_ROWS = 8
CONV_SEQS = 16
VMEM_LIMIT = 56 * 1024 * 1024


def _resident(shape):
    nd = len(shape)
    return pl.BlockSpec(shape, lambda *_: (0,) * nd, pipeline_mode=pl.Buffered(1))


def _rms(x, g):
    ms = jnp.mean(x * x, axis=-1, keepdims=True)
    return x * lax.rsqrt(ms + NORM_EPS) * g


def _swiglu(h, wgu_ref, wo_ref):
    f = wo_ref.shape[0]
    assert f % FF_CHUNK == 0
    acc = None
    for c in range(f // FF_CHUNK):
        g = jnp.dot(h, wgu_ref[:, c * FF_CHUNK:(c + 1) * FF_CHUNK], preferred_element_type=F32)
        u = jnp.dot(h, wgu_ref[:, f + c * FF_CHUNK:f + (c + 1) * FF_CHUNK], preferred_element_type=F32)
        a = (g * jax.nn.sigmoid(g) * u).astype(BF16)
        d = jnp.dot(a, wo_ref[c * FF_CHUNK:(c + 1) * FF_CHUNK, :], preferred_element_type=F32)
        acc = d if acc is None else acc + d
    return acc


def _head_norm(y, hm_ref, gain):
    parts = []
    for g in range(N_GROUPS):
        yg = y[:, g * GROUP_W:(g + 1) * GROUP_W]
        ms = jnp.dot((yg * yg).astype(BF16), hm_ref[...], preferred_element_type=F32)
        parts.append(yg * lax.rsqrt(ms + NORM_EPS))
    return jnp.concatenate(parts, axis=1) * gain


def _rope(y, c, s1, s2):
    half = ROT_DIM // 2
    parts = []
    for i in range(y.shape[1] // LANES):
        yc = y[:, i * LANES:(i + 1) * LANES]
        parts.append(yc * c + pltpu.roll(yc, half, 1) * s1 + pltpu.roll(yc, LANES - half, 1) * s2)
    return jnp.concatenate(parts, axis=1)


def _layernorm_silu(y, g, b):
    mu = jnp.mean(y, axis=-1, keepdims=True)
    yc = y - mu
    var = jnp.mean(yc * yc, axis=-1, keepdims=True)
    yn = yc * lax.rsqrt(var + NORM_EPS) * g + b
    return yn * jax.nn.sigmoid(yn)


def _ffn_mix_body(x_ref, rc_ref, rs1_ref, rs2_ref, g1_ref, wgu_ref, wo_ref, gm_ref, win_ref, hm_ref, qg_ref, kg_ref,
                  x1_ref, q_ref):
    x = x_ref[...]
    h = _rms(x, g1_ref[...]).astype(BF16)
    x1 = x + 0.5 * _swiglu(h, wgu_ref, wo_ref)
    x1_ref[...] = x1
    h2 = _rms(x1, gm_ref[...]).astype(BF16)
    cc = (win_ref.shape[1] - 3 * ATTN_W) // 2
    zu = jnp.dot(h2, win_ref[:, :2 * cc], preferred_element_type=F32)
    u = zu[:, :cc] * jax.nn.sigmoid(zu[:, cc:])
    z = jnp.dot(h2, win_ref[:, 2 * cc:], preferred_element_type=F32)
    c, s1, s2 = rc_ref[...], rs1_ref[...], rs2_ref[...]
    q = _rope(_head_norm(z[:, :ATTN_W], hm_ref, qg_ref[...]), c, s1, s2)
    k = _rope(_head_norm(z[:, ATTN_W:2 * ATTN_W], hm_ref, kg_ref[...]), c, s1, s2)
    q_ref[...] = q.astype(BF16)
    return u, k, z[:, 2 * ATTN_W:]


def _ffn_mix_prompt_kernel(x_ref, rc_ref, rs1_ref, rs2_ref, g1_ref, wgu_ref, wo_ref, gm_ref, win_ref, hm_ref, qg_ref,
                           kg_ref, cw_ref, cb_ref, lng_ref, lnb_ref,
                           x1_ref, q_ref, k_ref, v_ref, a_ref, utail_ref, kt0_ref, kt1_ref, kt2_ref, vt0_ref, vt1_ref,
                           vt2_ref, ext_ref, *, tiles_per_seq):
    tm = x_ref.shape[0]
    u, k, v = _ffn_mix_body(x_ref, rc_ref, rs1_ref, rs2_ref, g1_ref, wgu_ref, wo_ref, gm_ref, win_ref, hm_ref, qg_ref,
                            kg_ref, x1_ref, q_ref)
    k_ref[...] = k.astype(BF16)
    v_ref[...] = v.astype(BF16)
    for y, refs in ((k, (kt0_ref, kt1_ref, kt2_ref)), (v, (vt0_ref, vt1_ref, vt2_ref))):
        yt = jnp.transpose(y)
        for g, ref in enumerate(refs):
            ref[...] = yt[g * GROUP_W:(g + 1) * GROUP_W, tm - ref.shape[1]:]

    @pl.when(pl.program_id(0) % tiles_per_seq == 0)
    def _():
        ext_ref[0:CONV_PAD, :] = jnp.zeros((CONV_PAD, ext_ref.shape[1]), F32)

    ext_ref[CONV_PAD:CONV_PAD + tm, :] = u
    first = CONV_PAD - (CONV_WIDTH - 1)
    rows = 64
    for r0 in range(0, tm, rows):
        acc = None
        for r in range(SUBLANES):
            taps = [j for j in range(CONV_WIDTH) if (first + j) % SUBLANES == r]
            if not taps:
                continue
            lo = first + min(taps)
            span = max(taps) - min(taps) + rows
            win = ext_ref[r0 + lo:r0 + lo + span, :]
            for j in taps:
                o = j - min(taps)
                term = win[o:o + rows, :] * cw_ref[j:j + 1, :]
                acc = term if acc is None else acc + term
        y = acc + cb_ref[...]
        a_ref[r0:r0 + rows, :] = _layernorm_silu(y, lng_ref[...], lnb_ref[...]).astype(BF16)
    tail = ext_ref[tm:tm + CONV_PAD, :]
    utail_ref[...] = tail
    ext_ref[0:CONV_PAD, :] = tail


def _ffn_mix_sample_kernel(x_ref, rc_ref, rs1_ref, rs2_ref, g1_ref, wgu_ref, wo_ref, gm_ref, win_ref, hm_ref, qg_ref,
                           kg_ref, x1_ref, q_ref, k32_ref, v32_ref, u_ref):
    u, k, v = _ffn_mix_body(x_ref, rc_ref, rs1_ref, rs2_ref, g1_ref, wgu_ref, wo_ref, gm_ref, win_ref, hm_ref, qg_ref,
                            kg_ref, x1_ref, q_ref)
    u_ref[...] = u
    k32_ref[...] = k
    v32_ref[...] = v


def _ffn_mix(x, rope, w, *, prompt_seq_len=None):
    n, d = x.shape
    tm = TOKEN_TILE
    assert n % tm == 0
    rope_tiles = rope[0].shape[0] // tm
    cc = w["conv_w"].shape[1]
    tok = lambda width: pl.BlockSpec((tm, width), lambda i: (i, 0))
    rope_spec = pl.BlockSpec((tm, LANES), lambda i: (i % rope_tiles, 0))
    weights = [w["ffn1_norm"], w["ffn1_wgu"], w["ffn1_wo"], w["mix_norm"], w["w_in_a"], w["head_mean"],
               w["q_gain"], w["k_gain"]]
    in_specs = [tok(d), rope_spec, rope_spec, rope_spec] + [_resident(a.shape) for a in weights]
    out_shape = [jax.ShapeDtypeStruct((n, d), F32), jax.ShapeDtypeStruct((n, ATTN_W), BF16)]
    out_specs = [tok(d), tok(ATTN_W)]
    scratch = []
    if prompt_seq_len is not None:
        t = prompt_seq_len
        assert t % tm == 0 and all(wd % tm == 0 or tm % wd == 0 for wd in WINDOWS)
        tps = t // tm
        nseq = n // t
        conv_w = [w["conv_w"], w["conv_b"], w["conv_ln_g"], w["conv_ln_b"]]
        weights = weights + conv_w
        in_specs += [_resident(a.shape) for a in conv_w]
        out_shape += [jax.ShapeDtypeStruct((n, ATTN_W), BF16)] * 2 \
            + [jax.ShapeDtypeStruct((n, cc), BF16), jax.ShapeDtypeStruct((nseq, CONV_PAD, cc), F32)]
        out_specs += [tok(ATTN_W)] * 2 + [tok(cc), pl.BlockSpec((None, CONV_PAD, cc), lambda i: (i // tps, 0, 0))]
        for _ in range(2):
            for wd in WINDOWS:
                bw = min(wd, tm)
                first = (t - wd) // tm if wd >= tm else tps - 1
                out_shape.append(jax.ShapeDtypeStruct((nseq, GROUP_W, wd), F32))
                out_specs.append(pl.BlockSpec(
                    (None, GROUP_W, bw), lambda i, first=first: (i // tps, 0, jnp.maximum(i % tps - first, 0))))
        scratch = [pltpu.VMEM((tm + CONV_PAD, cc), F32)]
        kern = functools.partial(_ffn_mix_prompt_kernel, tiles_per_seq=tps)
    else:
        out_shape += [jax.ShapeDtypeStruct((n, ATTN_W), F32)] * 2 + [jax.ShapeDtypeStruct((n, cc), F32)]
        out_specs += [tok(ATTN_W)] * 2 + [tok(cc)]
        kern = _ffn_mix_sample_kernel
    return pl.pallas_call(
        kern, grid=(n // tm,), in_specs=in_specs, out_specs=out_specs, out_shape=out_shape, scratch_shapes=scratch,
        compiler_params=pltpu.CompilerParams(dimension_semantics=("arbitrary",), vmem_limit_bytes=VMEM_LIMIT),
        name="ffn_mix_prompt" if prompt_seq_len is not None else "ffn_mix_sample",
    )(x, *rope, *weights)


def _attn_unit(q, kp, kc, vp, vc, first):
    row = lax.broadcasted_iota(jnp.int32, (BLK, BLK), 0)
    col = lax.broadcasted_iota(jnp.int32, (BLK, BLK), 1)
    bias_p = jnp.where(col >= row, 0.0, NEG) + jnp.where(first, NEG, 0.0)
    bias_c = jnp.where(col <= row, 0.0, NEG)
    head = lax.broadcasted_iota(jnp.int32, (1, GROUP_W), 1) // HEAD_DIM
    contract_last = (((1,), (1,)), ((), ()))
    out = jnp.zeros((BLK, GROUP_W), F32)
    lse = jnp.zeros((BLK, GROUP_W), F32)
    zero = jnp.zeros((), BF16)
    for h in range(HEADS_PER_GROUP):
        hm = head == h
        qh = jnp.where(hm, q, zero)
        sp = lax.dot_general(qh, kp, contract_last, preferred_element_type=F32) * ATTN_SCALE + bias_p
        sc = lax.dot_general(qh, kc, contract_last, preferred_element_type=F32) * ATTN_SCALE + bias_c
        m = jnp.maximum(jnp.max(sp, axis=1, keepdims=True), jnp.max(sc, axis=1, keepdims=True))
        pp = jnp.exp(sp - m)
        pc = jnp.exp(sc - m)
        l = jnp.sum(pp, axis=1, keepdims=True) + jnp.sum(pc, axis=1, keepdims=True)
        o = jnp.dot(pp.astype(BF16), jnp.where(hm, vp, zero), preferred_element_type=F32) \
            + jnp.dot(pc.astype(BF16), jnp.where(hm, vc, zero), preferred_element_type=F32)
        out = out + o / l
        lse = lse + jnp.where(hm, m + jnp.log(l), 0.0)
    return out, lse


def _attn_prompt_kernel(*refs):
    ins, outs = refs[:5 * N_GROUPS], refs[5 * N_GROUPS:]
    t = pl.program_id(1)
    u = pl.program_id(2)
    for g in range(N_GROUPS):
        q_ref, kc_ref, kp_ref, vc_ref, vp_ref = ins[5 * g:5 * g + 5]
        d = DILATIONS[g]
        first = (t * (UNITS // d) + u // d) == 0
        o, lse = _attn_unit(q_ref[...], kp_ref[...], kc_ref[...], vp_ref[...], vc_ref[...], first)
        outs[2 * g][...] = o.astype(BF16)
        outs[2 * g + 1][...] = lse


def _attn_prompt(q, k, v, batch, seq_len):
    assert seq_len % ATTN_TILE == 0
    args, in_specs, out_shape, out_specs = [], [], [], []
    for g in range(N_GROUPS):
        d = DILATIONS[g]
        nb = UNITS // d
        cur = lambda b, t, u, d=d, nb=nb, g=g: (b, t * nb + u // d, (u % d) * N_GROUPS + g)
        prev = lambda b, t, u, d=d, nb=nb, g=g: (b, jnp.maximum(t * nb + u // d - 1, 0), (u % d) * N_GROUPS + g)
        outi = lambda b, t, u, d=d, nb=nb: (b, t * nb + u // d, u % d)
        view = lambda a, d=d: a.reshape(batch, seq_len // d, d * ATTN_W)
        blk = lambda im: pl.BlockSpec((None, BLK, GROUP_W), im)
        args += [view(q), view(k), view(k), view(v), view(v)]
        in_specs += [blk(cur), blk(cur), blk(prev), blk(cur), blk(prev)]
        out_shape += [jax.ShapeDtypeStruct((batch, seq_len // d, d * GROUP_W), BF16),
                      jax.ShapeDtypeStruct((batch, seq_len // d, d * GROUP_W), F32)]
        out_specs += [blk(outi), blk(outi)]
    res = pl.pallas_call(
        _attn_prompt_kernel, grid=(batch, seq_len // ATTN_TILE, UNITS), in_specs=in_specs, out_specs=out_specs,
        out_shape=out_shape,
        compiler_params=pltpu.CompilerParams(dimension_semantics=("arbitrary",) * 3),
        name="attn_prompt",
    )(*args)
    return [r.reshape(batch * seq_len, GROUP_W) for r in res]


def _conv_sample_kernel(state_ref, u_ref, cw_ref, cb_ref, lng_ref, lnb_ref, a_ref, new_state_ref):
    hist = state_ref.shape[0]
    s_len = u_ref.shape[0]
    row = lambda t: state_ref[t] if t < hist else u_ref[t - hist]
    for s in range(s_len):
        acc = None
        for j in range(CONV_WIDTH):
            term = row(s + j) * cw_ref[j:j + 1, :]
            acc = term if acc is None else acc + term
        a_ref[s] = _layernorm_silu(acc + cb_ref[...], lng_ref[...], lnb_ref[...])
    for t in range(hist):
        new_state_ref[t] = row(t + s_len)


def _conv_sample(state_t, u_t, w):
    hist, db, cc = state_t.shape
    s_len = u_t.shape[0]
    assert hist == CONV_WIDTH - 1 and db % CONV_SEQS == 0
    conv_w = [w["conv_w"], w["conv_b"], w["conv_ln_g"], w["conv_ln_b"]]
    slab = lambda rows: pl.BlockSpec((rows, CONV_SEQS, cc), lambda i: (0, i, 0))
    return pl.pallas_call(
        _conv_sample_kernel, grid=(db // CONV_SEQS,),
        in_specs=[slab(hist), slab(s_len)] + [_resident(a.shape) for a in conv_w],
        out_specs=[slab(s_len), slab(hist)],
        out_shape=[jax.ShapeDtypeStruct((s_len, db, cc), F32), jax.ShapeDtypeStruct((hist, db, cc), F32)],
        compiler_params=pltpu.CompilerParams(dimension_semantics=("arbitrary",)),
        name="conv_sample",
    )(state_t, u_t, *conv_w)


def _sample_bias(width, d, s_len):
    s = lax.broadcasted_iota(jnp.int32, (SAMPLE_ROWS, width), 0)
    t = lax.broadcasted_iota(jnp.int32, (SAMPLE_ROWS, width), 1)
    same_residue = jnp.bitwise_and(s - t, d - 1) == 0
    live = s < s_len
    old = jnp.where(live & same_residue & (t >= s), 0.0, NEG)
    sn = s[:, :LANES]
    tn = t[:, :LANES]
    new = jnp.where(live[:, :LANES] & same_residue[:, :LANES] & (tn <= sn), 0.0, NEG)
    return old, new


def _attn_sample_kernel(q_ref, kn_ref, vn_ref, kc0_ref, vc0_ref, kc1_ref, vc1_ref, kc2_ref, vc2_ref,
                        b_ref, nk0_ref, nv0_ref, nk1_ref, nv1_ref, nk2_ref, nv2_ref, *, s_len):
    kc_refs = (kc0_ref, kc1_ref, kc2_ref)
    vc_refs = (vc0_ref, vc1_ref, vc2_ref)
    nk_refs = (nk0_ref, nk1_ref, nk2_ref)
    nv_refs = (nv0_ref, nv1_ref, nv2_ref)
    q = q_ref[...]
    pad = jnp.zeros((LANES - SAMPLE_ROWS, ATTN_W), F32)
    knt = jnp.transpose(jnp.concatenate([kn_ref[...], pad], axis=0))
    vnt = jnp.transpose(jnp.concatenate([vn_ref[...], pad], axis=0))
    biases = [_sample_bias(WINDOWS[g], DILATIONS[g], s_len) for g in range(N_GROUPS)]
    lane = lax.broadcasted_iota(jnp.int32, (1, LANES), 1)
    keep = lane < LANES - s_len
    nt = (((1,), (1,)), ((), ()))

    def shifted(old, new_cols, out_ref, h):
        w = old.shape[1]
        cols = [old[:, j * LANES:(j + 1) * LANES] for j in range(w // LANES)] + [new_cols]
        rolled = [pltpu.roll(c, LANES - s_len, 1) for c in cols]
        for j in range(w // LANES):
            out_ref[h, :, j * LANES:(j + 1) * LANES] = jnp.where(keep, rolled[j], rolled[j + 1])

    for h in range(HEADS_PER_GROUP):
        s_old, s_new, olds, news = [], [], [], []
        for g in range(N_GROUPS):
            c = g * HEADS_PER_GROUP + h
            hs = slice(c * HEAD_DIM, (c + 1) * HEAD_DIM)
            qh = q[:, hs].astype(BF16)
            k_old = kc_refs[g][h]
            k_new = knt[hs, :]
            s_old.append(jnp.dot(qh, k_old.astype(BF16), preferred_element_type=F32) * ATTN_SCALE + biases[g][0])
            s_new.append(jnp.dot(qh, k_new.astype(BF16), preferred_element_type=F32) * ATTN_SCALE + biases[g][1])
            shifted(k_old, k_new, nk_refs[g], h)
        m = None
        for a in s_old + s_new:
            am = jnp.max(a, axis=1, keepdims=True)
            m = am if m is None else jnp.maximum(m, am)
        num = jnp.zeros((SAMPLE_ROWS, HEAD_DIM), F32)
        den = jnp.zeros((SAMPLE_ROWS, 1), F32)
        for g in range(N_GROUPS):
            c = g * HEADS_PER_GROUP + h
            hs = slice(c * HEAD_DIM, (c + 1) * HEAD_DIM)
            p_old = jnp.exp(s_old[g] - m)
            p_new = jnp.exp(s_new[g] - m)
            v_old = vc_refs[g][h]
            v_new = vnt[hs, :]
            den = den + jnp.sum(p_old, axis=1, keepdims=True) + jnp.sum(p_new, axis=1, keepdims=True)
            num = num + lax.dot_general(p_old.astype(BF16), v_old.astype(BF16), nt, preferred_element_type=F32) \
                + lax.dot_general(p_new.astype(BF16), v_new.astype(BF16), nt, preferred_element_type=F32)
            shifted(v_old, v_new, nv_refs[g], h)
        b_ref[:, h * HEAD_DIM:(h + 1) * HEAD_DIM] = num / den


def _attn_sample(q8, kn8, vn8, caches_k, caches_v, s_len):
    db = q8.shape[0]
    assert q8.shape[1] == SAMPLE_ROWS and s_len <= min(SAMPLE_ROWS, DILATIONS[1])
    tok = pl.BlockSpec((None, SAMPLE_ROWS, ATTN_W), lambda b: (b, 0, 0))
    args, in_specs, out_shape, out_specs = [q8, kn8, vn8], [tok] * 3, [], []
    out_shape.append(jax.ShapeDtypeStruct((db, SAMPLE_ROWS, GROUP_W), F32))
    out_specs.append(pl.BlockSpec((None, SAMPLE_ROWS, GROUP_W), lambda b: (b, 0, 0)))
    for g in range(N_GROUPS):
        shape = (db, HEADS_PER_GROUP, HEAD_DIM, WINDOWS[g])
        spec = pl.BlockSpec((None,) + shape[1:], lambda b: (b, 0, 0, 0))
        for c in (caches_k[g], caches_v[g]):
            assert c.shape == shape
            args.append(c)
            in_specs.append(spec)
            out_shape.append(jax.ShapeDtypeStruct(shape, F32))
            out_specs.append(spec)
    return pl.pallas_call(
        functools.partial(_attn_sample_kernel, s_len=s_len), grid=(db,), in_specs=in_specs, out_specs=out_specs,
        out_shape=out_shape,
        compiler_params=pltpu.CompilerParams(dimension_semantics=("arbitrary",), vmem_limit_bytes=VMEM_LIMIT),
        name="attn_sample",
    )(*args)


def _merge_ffn_body(x1_ref, a_ref, battn, gm_ref, wg_ref, bg_ref, wco_ref, wao_ref, wout_ref, g2_ref, wgu_ref,
                    wo_ref, y_ref):
    x1 = x1_ref[...]
    d = x1.shape[1]
    h = _rms(x1, gm_ref[...]).astype(BF16)
    gates = jax.nn.sigmoid(jnp.dot(h, wg_ref[...], preferred_element_type=F32) + bg_ref[...])
    a = jnp.dot(a_ref[...].astype(BF16), wco_ref[...], preferred_element_type=F32)
    b = jnp.dot(battn.astype(BF16), wao_ref[...], preferred_element_type=F32)
    mix = (gates[:, :d] * a + gates[:, d:] * b).astype(BF16)
    y = x1 + jnp.dot(mix, wout_ref[...], preferred_element_type=F32)
    h2 = _rms(y, g2_ref[...]).astype(BF16)
    y_ref[...] = y + 0.5 * _swiglu(h2, wgu_ref, wo_ref)


def _merge_ffn_groups_kernel(x1_ref, a_ref, o0_ref, o1_ref, o2_ref, l0_ref, l1_ref, l2_ref, *rest):
    lses = [l0_ref[...], l1_ref[...], l2_ref[...]]
    outs = [o0_ref[...], o1_ref[...], o2_ref[...]]
    mx = jnp.maximum(jnp.maximum(lses[0], lses[1]), lses[2])
    num = None
    den = None
    for o, l in zip(outs, lses):
        wgt = jnp.exp(l - mx)
        num = wgt * o if num is None else num + wgt * o
        den = wgt if den is None else den + wgt
    _merge_ffn_body(x1_ref, a_ref, num / den, *rest)


def _merge_ffn_direct_kernel(x1_ref, a_ref, b_ref, *rest):
    _merge_ffn_body(x1_ref, a_ref, b_ref[...], *rest)


def _merge_ffn(x1, a_pre, attn, w):
    n, d = x1.shape
    tm = TOKEN_TILE
    assert n % tm == 0
    tok = lambda width: pl.BlockSpec((tm, width), lambda i: (i, 0))
    weights = [w["mix_norm"], w["w_in_g"], w["b_gate"], w["w_conv_out"], w["w_attn_out"], w["w_out"], w["ffn2_norm"],
               w["ffn2_wgu"], w["ffn2_wo"]]
    kern = _merge_ffn_groups_kernel if len(attn) > 1 else _merge_ffn_direct_kernel
    return pl.pallas_call(
        kern, grid=(n // tm,),
        in_specs=[tok(d), tok(a_pre.shape[1])] + [tok(GROUP_W)] * len(attn) + [_resident(a.shape) for a in weights],
        out_specs=tok(d), out_shape=jax.ShapeDtypeStruct((n, d), F32),
        compiler_params=pltpu.CompilerParams(dimension_semantics=("arbitrary",), vmem_limit_bytes=VMEM_LIMIT),
        name="merge_ffn_groups" if len(attn) > 1 else "merge_ffn",
    )(x1, a_pre, *attn, *weights)


def _rope_tables(pos):
    half = ROT_DIM // 2
    inv = jnp.float32(ROPE_THETA) ** (-jnp.arange(half, dtype=F32) * (2.0 / ROT_DIM))
    ang = pos.astype(F32)[:, None] * inv[None, :]
    cos, sin = jnp.cos(ang), jnp.sin(ang)
    n = pos.shape[0]
    rest = HEAD_DIM - ROT_DIM
    zh = jnp.zeros((n, half), F32)
    c = jnp.concatenate([cos, cos, jnp.ones((n, rest), F32)], axis=1)
    s1 = jnp.concatenate([zh, sin, jnp.zeros((n, rest), F32)], axis=1)
    s2 = jnp.concatenate([-sin, zh, jnp.zeros((n, rest), F32)], axis=1)
    rep = LANES // HEAD_DIM
    return tuple(jnp.tile(t, (1, rep)) for t in (c, s1, s2))


def _prepare_weights(l, ffn1_norm, ffn1_w_in, ffn1_w_out, mix_norm, w_in, b_gate, q_norm, k_norm, conv_w, conv_b,
                     conv_ln_g, conv_ln_b, w_conv_out, w_attn_out, w_out, ffn2_norm, ffn2_w_in, ffn2_w_out):
    cc = conv_w.shape[2]
    n_a = 2 * cc + 3 * ATTN_W
    lane = jnp.arange(GROUP_W)
    same_head = (lane[:, None] // HEAD_DIM) == (lane[None, :] // HEAD_DIM)
    row = lambda a: a.reshape(1, -1).astype(F32)
    return {
        "ffn1_norm": row(ffn1_norm[l]), "ffn1_wgu": ffn1_w_in[l].astype(BF16), "ffn1_wo": ffn1_w_out[l].astype(BF16),
        "mix_norm": row(mix_norm[l]), "w_in_a": w_in[l][:, :n_a].astype(BF16), "w_in_g": w_in[l][:, n_a:].astype(BF16),
        "b_gate": row(b_gate[l]),
        "head_mean": jnp.where(same_head, 1.0 / HEAD_DIM, 0.0).astype(BF16),
        "q_gain": row(q_norm[l]), "k_gain": row(k_norm[l]),
        "conv_w": jnp.concatenate([conv_w[l], jnp.zeros((CONV_PAD - CONV_WIDTH, cc), F32)], axis=0),
        "conv_b": row(conv_b[l]), "conv_ln_g": row(conv_ln_g[l]), "conv_ln_b": row(conv_ln_b[l]),
        "w_conv_out": w_conv_out[l].astype(BF16), "w_attn_out": w_attn_out[l].astype(BF16),
        "w_out": w_out[l].astype(BF16),
        "ffn2_norm": row(ffn2_norm[l]), "ffn2_wgu": ffn2_w_in[l].astype(BF16), "ffn2_wo": ffn2_w_out[l].astype(BF16),
    }


def kernel(x_prompt, x_sample, cache_k_w128, cache_v_w128, cache_k_w512, cache_v_w512, cache_k_w2048, cache_v_w2048, state_conv, ffn1_norm, ffn1_w_in, ffn1_w_out, mix_norm, w_in, b_gate, q_norm, k_norm, conv_w, conv_b, conv_ln_g, conv_ln_b, w_conv_out, w_attn_out, w_out, ffn2_norm, ffn2_w_in, ffn2_w_out):
    batch, t, d = x_prompt.shape
    db, s_len, _ = x_sample.shape
    depth = w_in.shape[0]
    cc = conv_w.shape[2]
    hd = (HEADS_PER_GROUP, HEAD_DIM)
    cache_k = (cache_k_w128, cache_k_w512, cache_k_w2048)
    cache_v = (cache_v_w128, cache_v_w512, cache_v_w2048)
    for g, wdw in enumerate(WINDOWS):
        assert cache_k[g].shape == (depth, db, wdw, *hd), "sample caches must hold a full window"
        assert t >= wdw
    rope_p = _rope_tables(jnp.arange(t))
    rope_s = _rope_tables(PAST_LEN + jnp.arange(db * s_len) // db)
    to_time_minor = lambda c: jnp.transpose(c, (0, 2, 3, 1))
    from_time_minor = lambda c: jnp.transpose(c, (0, 3, 1, 2))

    xp = x_prompt.reshape(batch * t, d)
    xs = jnp.transpose(x_sample, (1, 0, 2)).reshape(s_len * db, d)
    outs_p = [[] for _ in range(2 * N_GROUPS)]
    outs_s = [[] for _ in range(2 * N_GROUPS)]
    new_conv_p, new_conv_s = [], []
    for l in range(depth):
        w = _prepare_weights(l, ffn1_norm, ffn1_w_in, ffn1_w_out, mix_norm, w_in, b_gate, q_norm, k_norm, conv_w,
                             conv_b, conv_ln_g, conv_ln_b, w_conv_out, w_attn_out, w_out, ffn2_norm, ffn2_w_in,
                             ffn2_w_out)
        x1p, qp, kp, vp, a_p, utail, *kv_t = _ffn_mix(xp, rope_p, w, prompt_seq_len=t)
        attn_p = _attn_prompt(qp, kp, vp, batch, t)
        xp = _merge_ffn(x1p, a_p, [attn_p[0], attn_p[2], attn_p[4], attn_p[1], attn_p[3], attn_p[5]], w)
        new_conv_p.append(utail[:, CONV_PAD - (CONV_WIDTH - 1):, :])
        for i, (g, c) in enumerate([(g, c) for c in range(2) for g in range(N_GROUPS)]):
            outs_p[2 * g + c].append(from_time_minor(kv_t[i].reshape(batch, *hd, WINDOWS[g])))
        x1s, qs, k32s, v32s, us = _ffn_mix(xs, rope_s, w)
        a_s, state_new = _conv_sample(jnp.transpose(state_conv[l], (1, 0, 2)), us.reshape(s_len, db, cc), w)
        rows8 = lambda a: jnp.pad(jnp.transpose(a.astype(F32).reshape(s_len, db, ATTN_W), (1, 0, 2)),
                                  ((0, 0), (0, SAMPLE_ROWS - s_len), (0, 0)))
        ck = [to_time_minor(cache_k[g][l]) for g in range(N_GROUPS)]
        cv = [to_time_minor(cache_v[g][l]) for g in range(N_GROUPS)]
        b_s, *new_caches = _attn_sample(rows8(qs), rows8(k32s), rows8(v32s), ck, cv, s_len)
        b_s = jnp.transpose(b_s[:, :s_len], (1, 0, 2)).reshape(s_len * db, GROUP_W)
        xs = _merge_ffn(x1s, a_s.reshape(s_len * db, cc), [b_s], w)
        new_conv_s.append(jnp.transpose(state_new, (1, 0, 2)))
        for i, nc in enumerate(new_caches):
            outs_s[i].append(from_time_minor(nc))
    y_s = jnp.transpose(xs.reshape(s_len, db, d), (1, 0, 2))
    return (xp.reshape(batch, t, d), y_s, *[jnp.stack(o) for o in outs_p], jnp.stack(new_conv_p),
            *[jnp.stack(o) for o in outs_s], jnp.stack(new_conv_s))
```

```python
import functools

import jax
import jax.numpy as jnp
from jax import lax
from jax.experimental import pallas as pl
from jax.experimental.pallas import tpu as pltpu

F32 = jnp.float32
BF16 = jnp.bfloat16

HEAD_DIM = 64
WINDOWS = (128, 512, 2048)
DILATIONS = (1, 4, 16)
N_GROUPS = 3
HEADS_PER_GROUP = 4
GROUP_W = HEADS_PER_GROUP * HEAD_DIM
ATTN_W = N_GROUPS * GROUP_W
BLK = WINDOWS[0] // DILATIONS[0]
assert all(w // d == BLK for w, d in zip(WINDOWS, DILATIONS))
ATTN_TILE = max(WINDOWS)
UNITS = ATTN_TILE // BLK
ATTN_SCALE = HEAD_DIM ** -0.5
ROT_DIM = HEAD_DIM // 4
ROPE_THETA = 500000.0
CONV_WIDTH = 31
CONV_PAD = 32
NORM_EPS = 1e-6
PAST_LEN = 2048
NEG = -1e30
LANES = 128
SUBLANES = 8
FF_CHUNK = 256
TOKEN_TILE = 256
SAMPLE_ROWS = 8
CONV_SEQS = 16
VMEM_LIMIT = 56 * 1024 * 1024


def _resident(shape):
    nd = len(shape)
    return pl.BlockSpec(shape, lambda *_: (0,) * nd, pipeline_mode=pl.Buffered(1))


def _rms(x, g):
    ms = jnp.mean(x * x, axis=-1, keepdims=True)
    return x * lax.rsqrt(ms + NORM_EPS) * g


def _swiglu(h, wgu_ref, wo_ref, act_ref):
    f = wo_ref.shape[0]
    assert f % FF_CHUNK == 0
    for c in range(f // FF_CHUNK):
        g = jnp.dot(h, wgu_ref[:, c * FF_CHUNK:(c + 1) * FF_CHUNK], preferred_element_type=F32)
        u = jnp.dot(h, wgu_ref[:, f + c * FF_CHUNK:f + (c + 1) * FF_CHUNK], preferred_element_type=F32)
        act_ref[:, c * FF_CHUNK:(c + 1) * FF_CHUNK] = (g * jax.nn.sigmoid(g) * u).astype(BF16)
    return jnp.dot(act_ref[...], wo_ref[...], preferred_element_type=F32)


def _head_norm(y, hm_ref, gain):
    parts = []
    for g in range(N_GROUPS):
        yg = y[:, g * GROUP_W:(g + 1) * GROUP_W]
        ms = jnp.dot((yg * yg).astype(BF16), hm_ref[...], preferred_element_type=F32)
        parts.append(yg * lax.rsqrt(ms + NORM_EPS))
    return jnp.concatenate(parts, axis=1) * gain


def _rope(y, c, s1, s2):
    half = ROT_DIM // 2
    parts = []
    for i in range(y.shape[1] // LANES):
        yc = y[:, i * LANES:(i + 1) * LANES]
        parts.append(yc * c + pltpu.roll(yc, half, 1) * s1 + pltpu.roll(yc, LANES - half, 1) * s2)
    return jnp.concatenate(parts, axis=1)


def _layernorm_silu(y, g, b):
    mu = jnp.mean(y, axis=-1, keepdims=True)
    yc = y - mu
    var = jnp.mean(yc * yc, axis=-1, keepdims=True)
    yn = yc * lax.rsqrt(var + NORM_EPS) * g + b
    return yn * jax.nn.sigmoid(yn)


def _proj_stage(x_ref, g1_ref, wgu_ref, wo_ref, gm_ref, win_ref, x1_ref, act_ref):
    x = x_ref[...]
    h = _rms(x, g1_ref[...]).astype(BF16)
    x1 = x + 0.5 * _swiglu(h, wgu_ref, wo_ref, act_ref)
    x1_ref[...] = x1
    h2 = _rms(x1, gm_ref[...]).astype(BF16)
    return jnp.dot(h2, win_ref[...], preferred_element_type=F32)


def _mix_stage(z, rc_ref, rs1_ref, rs2_ref, hm_ref, qg_ref, kg_ref):
    cc = (z.shape[1] - 3 * ATTN_W) // 2
    u = z[:, :cc] * jax.nn.sigmoid(z[:, cc:2 * cc])
    c, s1, s2 = rc_ref[...], rs1_ref[...], rs2_ref[...]
    q = _rope(_head_norm(z[:, 2 * cc:2 * cc + ATTN_W], hm_ref, qg_ref[...]), c, s1, s2)
    k = _rope(_head_norm(z[:, 2 * cc + ATTN_W:2 * cc + 2 * ATTN_W], hm_ref, kg_ref[...]), c, s1, s2)
    return u, q, k, z[:, 2 * cc + 2 * ATTN_W:]


def _store_dilated(y, perm_refs, out_refs):
    yb = y.astype(BF16)
    out_refs[0][...] = yb[:, :GROUP_W]
    for g in range(1, N_GROUPS):
        d = DILATIONS[g]
        rows = y.shape[0] // d
        yp = jnp.dot(perm_refs[g - 1][...], yb[:, g * GROUP_W:(g + 1) * GROUP_W],
                     preferred_element_type=F32).astype(BF16)
        for r in range(d):
            out_refs[g][:, r * GROUP_W:(r + 1) * GROUP_W] = yp[r * rows:(r + 1) * rows, :]


def _ffn_mix_prompt_kernel(x_ref, rc_ref, rs1_ref, rs2_ref, g1_ref, wgu_ref, wo_ref, gm_ref, win_ref, hm_ref, qg_ref,
                           kg_ref, cw_ref, cb_ref, lng_ref, lnb_ref, p1_ref, p2_ref,
                           x1_ref, q0_ref, q1_ref, q2_ref, k0_ref, k1_ref, k2_ref, v0_ref, v1_ref, v2_ref, a_ref,
                           utail_ref, kt0_ref, kt1_ref, kt2_ref, vt0_ref, vt1_ref, vt2_ref, act_ref, ext_ref, *,
                           tiles_per_seq):
    tm = x_ref.shape[0]
    z = _proj_stage(x_ref, g1_ref, wgu_ref, wo_ref, gm_ref, win_ref, x1_ref, act_ref)
    u, q, k, v = _mix_stage(z, rc_ref, rs1_ref, rs2_ref, hm_ref, qg_ref, kg_ref)
    perms = (p1_ref, p2_ref)
    _store_dilated(q, perms, (q0_ref, q1_ref, q2_ref))
    _store_dilated(k, perms, (k0_ref, k1_ref, k2_ref))
    _store_dilated(v, perms, (v0_ref, v1_ref, v2_ref))
    for y, refs in ((k, (kt0_ref, kt1_ref, kt2_ref)), (v, (vt0_ref, vt1_ref, vt2_ref))):
        yt = jnp.transpose(y)
        for g, ref in enumerate(refs):
            ref[...] = yt[g * GROUP_W:(g + 1) * GROUP_W, tm - ref.shape[1]:]

    @pl.when(pl.program_id(0) % tiles_per_seq == 0)
    def _():
        ext_ref[0:CONV_PAD, :] = jnp.zeros((CONV_PAD, ext_ref.shape[1]), F32)

    ext_ref[CONV_PAD:CONV_PAD + tm, :] = u
    first = CONV_PAD - (CONV_WIDTH - 1)
    rows = 64
    for r0 in range(0, tm, rows):
        acc = None
        for r in range(SUBLANES):
            taps = [j for j in range(CONV_WIDTH) if (first + j) % SUBLANES == r]
            lo = first + min(taps)
            span = max(taps) - min(taps) + rows
            win = ext_ref[r0 + lo:r0 + lo + span, :]
            for j in taps:
                o = j - min(taps)
                term = win[o:o + rows, :] * cw_ref[j:j + 1, :]
                acc = term if acc is None else acc + term
        y = acc + cb_ref[...]
        a_ref[r0:r0 + rows, :] = _layernorm_silu(y, lng_ref[...], lnb_ref[...]).astype(BF16)
    tail = ext_ref[tm:tm + CONV_PAD, :]
    utail_ref[...] = tail
    ext_ref[0:CONV_PAD, :] = tail


def _ffn_mix_sample_kernel(x_ref, rc_ref, rs1_ref, rs2_ref, g1_ref, wgu_ref, wo_ref, gm_ref, win_ref, hm_ref, qg_ref,
                           kg_ref, x1_ref, q_ref, k32_ref, v32_ref, u_ref, act_ref):
    z = _proj_stage(x_ref, g1_ref, wgu_ref, wo_ref, gm_ref, win_ref, x1_ref, act_ref)
    u, q, k, v = _mix_stage(z, rc_ref, rs1_ref, rs2_ref, hm_ref, qg_ref, kg_ref)
    q_ref[...] = q
    u_ref[...] = u
    k32_ref[...] = k
    v32_ref[...] = v


def _ffn_mix_weights(w):
    return [w["ffn1_norm"], w["ffn1_wgu"], w["ffn1_wo"], w["mix_norm"], w["w_in_a"], w["head_mean"],
            w["q_gain"], w["k_gain"]]


def _ffn_mix_sample(x, rope, w):
    n, d = x.shape
    tm = TOKEN_TILE
    assert n % tm == 0 and rope[0].shape[0] == n
    cc = w["conv_w"].shape[1]
    tok = lambda width: pl.BlockSpec((tm, width), lambda i: (i, 0))
    weights = _ffn_mix_weights(w)
    widths = [d, ATTN_W, ATTN_W, ATTN_W, cc]
    return pl.pallas_call(
        _ffn_mix_sample_kernel, grid=(n // tm,),
        in_specs=[tok(d)] + [tok(LANES)] * 3 + [_resident(a.shape) for a in weights],
        out_specs=[tok(wd) for wd in widths], out_shape=[jax.ShapeDtypeStruct((n, wd), F32) for wd in widths],
        scratch_shapes=[pltpu.VMEM((tm, w["ffn1_wo"].shape[0]), BF16)],
        compiler_params=pltpu.CompilerParams(dimension_semantics=("arbitrary",), vmem_limit_bytes=VMEM_LIMIT),
        name="ffn_mix_sample",
    )(x, *rope, *weights)


def _ffn_mix_prompt(x, rope, w, seq_len):
    n, d = x.shape
    tm = TOKEN_TILE
    t = seq_len
    assert n % t == 0 and t % tm == 0 and rope[0].shape[0] == t
    assert all(wd % tm == 0 or tm % wd == 0 for wd in WINDOWS)
    tps = t // tm
    nt = n // tm
    nseq = n // t
    cc = w["conv_w"].shape[1]
    weights = _ffn_mix_weights(w) + [w["conv_w"], w["conv_b"], w["conv_ln_g"], w["conv_ln_b"], w["perm1"],
                                     w["perm2"]]
    tok = lambda rows, width: pl.BlockSpec((rows, width), lambda i: (i, 0))
    rope_spec = pl.BlockSpec((tm, LANES), lambda i: (i % tps, 0))
    in_specs = [tok(tm, d)] + [rope_spec] * 3 + [_resident(a.shape) for a in weights]
    out_shape = [jax.ShapeDtypeStruct((n, d), F32)]
    out_specs = [tok(tm, d)]
    for _ in range(3):
        for dl in DILATIONS:
            out_shape.append(jax.ShapeDtypeStruct((n // dl, dl * GROUP_W), BF16))
            out_specs.append(tok(tm // dl, dl * GROUP_W))
    out_shape += [jax.ShapeDtypeStruct((n, cc), BF16), jax.ShapeDtypeStruct((nseq, CONV_PAD, cc), F32)]
    out_specs += [tok(tm, cc), pl.BlockSpec((None, CONV_PAD, cc), lambda i: (i // tps, 0, 0))]
    for _ in range(2):
        for wd in WINDOWS:
            first = (t - wd) // tm if wd >= tm else tps - 1
            out_shape.append(jax.ShapeDtypeStruct((nseq, GROUP_W, wd), F32))
            out_specs.append(pl.BlockSpec(
                (None, GROUP_W, min(wd, tm)),
                lambda i, first=first: (i // tps, 0, jnp.maximum(i % tps - first, 0))))
    return pl.pallas_call(
        functools.partial(_ffn_mix_prompt_kernel, tiles_per_seq=tps), grid=(nt,), in_specs=in_specs,
        out_specs=out_specs, out_shape=out_shape,
        scratch_shapes=[pltpu.VMEM((tm, w["ffn1_wo"].shape[0]), BF16), pltpu.VMEM((tm + CONV_PAD, cc), F32)],
        compiler_params=pltpu.CompilerParams(dimension_semantics=("arbitrary",), vmem_limit_bytes=VMEM_LIMIT),
        name="ffn_mix_prompt",
    )(x, *rope, *weights)


def _attn_unit(q, kp, kc, vp, vc, first):
    row = lax.broadcasted_iota(jnp.int32, (BLK, BLK), 0)
    col = lax.broadcasted_iota(jnp.int32, (BLK, BLK), 1)
    bias_p = jnp.where(col >= row, 0.0, NEG) + jnp.where(first, NEG, 0.0)
    bias_c = jnp.where(col <= row, 0.0, NEG)
    head = lax.broadcasted_iota(jnp.int32, (1, GROUP_W), 1) // HEAD_DIM
    contract_last = (((1,), (1,)), ((), ()))
    out = jnp.zeros((BLK, GROUP_W), F32)
    lse = jnp.zeros((BLK, GROUP_W), F32)
    zero = jnp.zeros((), BF16)
    for h in range(HEADS_PER_GROUP):
        hm = head == h
        qh = jnp.where(hm, q, zero)
        sp = lax.dot_general(qh, kp, contract_last, preferred_element_type=F32) * ATTN_SCALE + bias_p
        sc = lax.dot_general(qh, kc, contract_last, preferred_element_type=F32) * ATTN_SCALE + bias_c
        m = jnp.maximum(jnp.max(sp, axis=1, keepdims=True), jnp.max(sc, axis=1, keepdims=True))
        pp = jnp.exp(sp - m)
        pc = jnp.exp(sc - m)
        l = jnp.sum(pp, axis=1, keepdims=True) + jnp.sum(pc, axis=1, keepdims=True)
        o = jnp.dot(pp.astype(BF16), jnp.where(hm, vp, zero), preferred_element_type=F32) \
            + jnp.dot(pc.astype(BF16), jnp.where(hm, vc, zero), preferred_element_type=F32)
        out = out + o / l
        lse = lse + jnp.where(hm, m + jnp.log(l), 0.0)
    return out, lse


def _attn_prompt_kernel(*refs):
    ins, outs = refs[:5 * N_GROUPS], refs[5 * N_GROUPS:]
    t = pl.program_id(1)
    u = pl.program_id(2)
    for g in range(N_GROUPS):
        q_ref, kc_ref, kp_ref, vc_ref, vp_ref = ins[5 * g:5 * g + 5]
        d = DILATIONS[g]
        first = (t * (UNITS // d) + u // d) == 0
        o, lse = _attn_unit(q_ref[...], kp_ref[...], kc_ref[...], vp_ref[...], vc_ref[...], first)
        outs[2 * g][...] = o.astype(BF16)
        outs[2 * g + 1][...] = lse


def _attn_prompt(qd, kd, vd, batch, seq_len):
    assert seq_len % ATTN_TILE == 0
    args, in_specs, out_shape, out_specs = [], [], [], []
    for g in range(N_GROUPS):
        d = DILATIONS[g]
        nb = UNITS // d
        cur = lambda b, t, u, d=d, nb=nb: (b, t * nb + u // d, u % d)
        prev = lambda b, t, u, d=d, nb=nb: (b, jnp.maximum(t * nb + u // d - 1, 0), u % d)
        view = lambda a, d=d: a.reshape(batch, seq_len // d, d * GROUP_W)
        blk = lambda im: pl.BlockSpec((None, BLK, GROUP_W), im)
        args += [view(qd[g]), view(kd[g]), view(kd[g]), view(vd[g]), view(vd[g])]
        in_specs += [blk(cur), blk(cur), blk(prev), blk(cur), blk(prev)]
        out_shape += [jax.ShapeDtypeStruct((batch, seq_len // d, d * GROUP_W), BF16),
                      jax.ShapeDtypeStruct((batch, seq_len // d, d * GROUP_W), F32)]
        out_specs += [blk(cur), blk(cur)]
    res = pl.pallas_call(
        _attn_prompt_kernel, grid=(batch, seq_len // ATTN_TILE, UNITS), in_specs=in_specs, out_specs=out_specs,
        out_shape=out_shape,
        compiler_params=pltpu.CompilerParams(dimension_semantics=("arbitrary",) * 3),
        name="attn_prompt",
    )(*args)
    return [r.reshape(-1, r.shape[2]) for r in res]


def _conv_sample_kernel(state_ref, u_ref, cw_ref, cb_ref, lng_ref, lnb_ref, a_ref, new_state_ref):
    hist = state_ref.shape[0]
    s_len = u_ref.shape[0]
    row = lambda t: state_ref[t] if t < hist else u_ref[t - hist]
    for s in range(s_len):
        acc = None
        for j in range(CONV_WIDTH):
            term = row(s + j) * cw_ref[j:j + 1, :]
            acc = term if acc is None else acc + term
        a_ref[s] = _layernorm_silu(acc + cb_ref[...], lng_ref[...], lnb_ref[...])
    for t in range(hist):
        new_state_ref[t] = row(t + s_len)


def _conv_sample(state_t, u_t, w):
    hist, db, cc = state_t.shape
    s_len = u_t.shape[0]
    assert hist == CONV_WIDTH - 1 and db % CONV_SEQS == 0
    conv_w = [w["conv_w"], w["conv_b"], w["conv_ln_g"], w["conv_ln_b"]]
    slab = lambda rows: pl.BlockSpec((rows, CONV_SEQS, cc), lambda i: (0, i, 0))
    return pl.pallas_call(
        _conv_sample_kernel, grid=(db // CONV_SEQS,),
        in_specs=[slab(hist), slab(s_len)] + [_resident(a.shape) for a in conv_w],
        out_specs=[slab(s_len), slab(hist)],
        out_shape=[jax.ShapeDtypeStruct((s_len, db, cc), F32), jax.ShapeDtypeStruct((hist, db, cc), F32)],
        compiler_params=pltpu.CompilerParams(dimension_semantics=("arbitrary",)),
        name="conv_sample",
    )(state_t, u_t, *conv_w)


def _sample_bias(width, d, s_len):
    s = lax.broadcasted_iota(jnp.int32, (SAMPLE_ROWS, width), 0)
    t = lax.broadcasted_iota(jnp.int32, (SAMPLE_ROWS, width), 1)
    same_residue = jnp.bitwise_and(s - t, d - 1) == 0
    live = s < s_len
    old = jnp.where(live & same_residue & (t >= s), 0.0, NEG)
    sn = s[:, :LANES]
    tn = t[:, :LANES]
    new = jnp.where(live[:, :LANES] & same_residue[:, :LANES] & (tn <= sn), 0.0, NEG)
    return old, new


def _attn_sample_kernel(q_ref, kn_ref, vn_ref, kc0_ref, vc0_ref, kc1_ref, vc1_ref, kc2_ref, vc2_ref,
                        b_ref, nk0_ref, nv0_ref, nk1_ref, nv1_ref, nk2_ref, nv2_ref, *, s_len):
    kc_refs = (kc0_ref, kc1_ref, kc2_ref)
    vc_refs = (vc0_ref, vc1_ref, vc2_ref)
    nk_refs = (nk0_ref, nk1_ref, nk2_ref)
    nv_refs = (nv0_ref, nv1_ref, nv2_ref)
    q = q_ref[...]
    pad = jnp.zeros((LANES - SAMPLE_ROWS, ATTN_W), F32)
    knt = jnp.transpose(jnp.concatenate([kn_ref[...], pad], axis=0))
    vnt = jnp.transpose(jnp.concatenate([vn_ref[...], pad], axis=0))
    biases = [_sample_bias(WINDOWS[g], DILATIONS[g], s_len) for g in range(N_GROUPS)]
    lane = lax.broadcasted_iota(jnp.int32, (1, LANES), 1)
    keep = lane < LANES - s_len
    nt = (((1,), (1,)), ((), ()))

    def shifted(old, new_cols, out_ref, h):
        w = old.shape[1]
        cols = [old[:, j * LANES:(j + 1) * LANES] for j in range(w // LANES)] + [new_cols]
        rolled = [pltpu.roll(c, LANES - s_len, 1) for c in cols]
        for j in range(w // LANES):
            out_ref[h, :, j * LANES:(j + 1) * LANES] = jnp.where(keep, rolled[j], rolled[j + 1])

    for h in range(HEADS_PER_GROUP):
        s_old, s_new = [], []
        for g in range(N_GROUPS):
            c = g * HEADS_PER_GROUP + h
            hs = slice(c * HEAD_DIM, (c + 1) * HEAD_DIM)
            qh = q[:, hs].astype(BF16)
            k_old = kc_refs[g][h]
            k_new = knt[hs, :]
            s_old.append(jnp.dot(qh, k_old.astype(BF16), preferred_element_type=F32) * ATTN_SCALE + biases[g][0])
            s_new.append(jnp.dot(qh, k_new.astype(BF16), preferred_element_type=F32) * ATTN_SCALE + biases[g][1])
            shifted(k_old, k_new, nk_refs[g], h)
        m = None
        for a in s_old + s_new:
            am = jnp.max(a, axis=1, keepdims=True)
            m = am if m is None else jnp.maximum(m, am)
        num = jnp.zeros((SAMPLE_ROWS, HEAD_DIM), F32)
        den = jnp.zeros((SAMPLE_ROWS, 1), F32)
        for g in range(N_GROUPS):
            c = g * HEADS_PER_GROUP + h
            hs = slice(c * HEAD_DIM, (c + 1) * HEAD_DIM)
            p_old = jnp.exp(s_old[g] - m)
            p_new = jnp.exp(s_new[g] - m)
            v_old = vc_refs[g][h]
            v_new = vnt[hs, :]
            den = den + jnp.sum(p_old, axis=1, keepdims=True) + jnp.sum(p_new, axis=1, keepdims=True)
            num = num + lax.dot_general(p_old.astype(BF16), v_old.astype(BF16), nt, preferred_element_type=F32) \
                + lax.dot_general(p_new.astype(BF16), v_new.astype(BF16), nt, preferred_element_type=F32)
            shifted(v_old, v_new, nv_refs[g], h)
        b_ref[:, h * HEAD_DIM:(h + 1) * HEAD_DIM] = num / den


def _attn_sample(q8, kn8, vn8, caches_k, caches_v, s_len):
    db = q8.shape[0]
    assert q8.shape[1] == SAMPLE_ROWS and s_len <= min(SAMPLE_ROWS, DILATIONS[1])
    tok = pl.BlockSpec((None, SAMPLE_ROWS, ATTN_W), lambda b: (b, 0, 0))
    args, in_specs, out_shape, out_specs = [q8, kn8, vn8], [tok] * 3, [], []
    out_shape.append(jax.ShapeDtypeStruct((db, SAMPLE_ROWS, GROUP_W), F32))
    out_specs.append(pl.BlockSpec((None, SAMPLE_ROWS, GROUP_W), lambda b: (b, 0, 0)))
    for g in range(N_GROUPS):
        shape = (db, HEADS_PER_GROUP, HEAD_DIM, WINDOWS[g])
        spec = pl.BlockSpec((None,) + shape[1:], lambda b: (b, 0, 0, 0))
        for c in (caches_k[g], caches_v[g]):
            assert c.shape == shape
            args.append(c)
            in_specs.append(spec)
            out_shape.append(jax.ShapeDtypeStruct(shape, F32))
            out_specs.append(spec)
    return pl.pallas_call(
        functools.partial(_attn_sample_kernel, s_len=s_len), grid=(db,), in_specs=in_specs, out_specs=out_specs,
        out_shape=out_shape,
        compiler_params=pltpu.CompilerParams(dimension_semantics=("arbitrary",), vmem_limit_bytes=VMEM_LIMIT),
        name="attn_sample",
    )(*args)


def _merge_stage(x1_ref, a_ref, battn, gm_ref, wg_ref, bg_ref, wco_ref, wao_ref, wout_ref):
    x1 = x1_ref[...]
    d = x1.shape[1]
    h = _rms(x1, gm_ref[...]).astype(BF16)
    gates = jax.nn.sigmoid(jnp.dot(h, wg_ref[...], preferred_element_type=F32) + bg_ref[...])
    a = jnp.dot(a_ref[...].astype(BF16), wco_ref[...], preferred_element_type=F32)
    b = jnp.dot(battn.astype(BF16), wao_ref[...], preferred_element_type=F32)
    mix = (gates[:, :d] * a + gates[:, d:] * b).astype(BF16)
    return x1 + jnp.dot(mix, wout_ref[...], preferred_element_type=F32)


def _ffn_stage(y, g2_ref, wgu_ref, wo_ref, act_ref):
    return y + 0.5 * _swiglu(_rms(y, g2_ref[...]).astype(BF16), wgu_ref, wo_ref, act_ref)


def _undilate(blk, pt_ref, d):
    if d == 1:
        return blk.astype(F32)
    rows = jnp.concatenate([blk[:, r * GROUP_W:(r + 1) * GROUP_W] for r in range(d)], axis=0)
    pt = pt_ref[...]
    if rows.dtype == BF16:
        return jnp.dot(pt, rows, preferred_element_type=F32)
    hi = rows.astype(BF16)
    rest = rows - hi.astype(F32)
    mid = rest.astype(BF16)
    lo = (rest - mid.astype(F32)).astype(BF16)
    return jnp.dot(pt, hi, preferred_element_type=F32) + jnp.dot(pt, mid, preferred_element_type=F32) \
        + jnp.dot(pt, lo, preferred_element_type=F32)


def _merge_ffn_prompt_kernel(x1_ref, a_ref, o0_ref, o1_ref, o2_ref, l0_ref, l1_ref, l2_ref, pt1_ref, pt2_ref,
                             gm_ref, wg_ref, bg_ref, wco_ref, wao_ref, wout_ref, g2_ref, wgu_ref, wo_ref,
                             out_ref, act_ref):
    pts = (None, pt1_ref, pt2_ref)
    lses = [_undilate(r[...], pts[g], DILATIONS[g]) for g, r in enumerate((l0_ref, l1_ref, l2_ref))]
    outs = [_undilate(r[...], pts[g], DILATIONS[g]) for g, r in enumerate((o0_ref, o1_ref, o2_ref))]
    mx = jnp.maximum(jnp.maximum(lses[0], lses[1]), lses[2])
    num = None
    den = None
    for o, l in zip(outs, lses):
        wgt = jnp.exp(l - mx)
        num = wgt * o if num is None else num + wgt * o
        den = wgt if den is None else den + wgt
    y = _merge_stage(x1_ref, a_ref, num / den, gm_ref, wg_ref, bg_ref, wco_ref, wao_ref, wout_ref)
    out_ref[...] = _ffn_stage(y, g2_ref, wgu_ref, wo_ref, act_ref)


def _merge_ffn_sample_kernel(x1_ref, a_ref, b_ref, gm_ref, wg_ref, bg_ref, wco_ref, wao_ref, wout_ref, g2_ref,
                             wgu_ref, wo_ref, out_ref, act_ref):
    y = _merge_stage(x1_ref, a_ref, b_ref[...], gm_ref, wg_ref, bg_ref, wco_ref, wao_ref, wout_ref)
    out_ref[...] = _ffn_stage(y, g2_ref, wgu_ref, wo_ref, act_ref)


def _merge_weights(w):
    return [w["mix_norm"], w["w_in_g"], w["b_gate"], w["w_conv_out"], w["w_attn_out"], w["w_out"], w["ffn2_norm"],
            w["ffn2_wgu"], w["ffn2_wo"]]


def _merge_ffn_sample(x1, a_pre, b, w):
    n, d = x1.shape
    tm = TOKEN_TILE
    assert n % tm == 0
    tok = lambda width: pl.BlockSpec((tm, width), lambda i: (i, 0))
    weights = _merge_weights(w)
    return pl.pallas_call(
        _merge_ffn_sample_kernel, grid=(n // tm,),
        in_specs=[tok(d), tok(a_pre.shape[1]), tok(GROUP_W)] + [_resident(a.shape) for a in weights],
        out_specs=tok(d), out_shape=jax.ShapeDtypeStruct((n, d), F32),
        scratch_shapes=[pltpu.VMEM((tm, w["ffn2_wo"].shape[0]), BF16)],
        compiler_params=pltpu.CompilerParams(dimension_semantics=("arbitrary",), vmem_limit_bytes=VMEM_LIMIT),
        name="merge_ffn_sample",
    )(x1, a_pre, b, *weights)


def _merge_ffn_prompt(x1, a_pre, outs, lses, w):
    n, d = x1.shape
    tm = TOKEN_TILE
    assert n % tm == 0
    tok = lambda rows, width: pl.BlockSpec((rows, width), lambda i: (i, 0))
    grp = [tok(tm // dl, dl * GROUP_W) for dl in DILATIONS]
    weights = [w["permt1"], w["permt2"]] + _merge_weights(w)
    return pl.pallas_call(
        _merge_ffn_prompt_kernel, grid=(n // tm,),
        in_specs=[tok(tm, d), tok(tm, a_pre.shape[1])] + grp + grp + [_resident(a.shape) for a in weights],
        out_specs=tok(tm, d), out_shape=jax.ShapeDtypeStruct((n, d), F32),
        scratch_shapes=[pltpu.VMEM((tm, w["ffn2_wo"].shape[0]), BF16)],
        compiler_params=pltpu.CompilerParams(dimension_semantics=("arbitrary",), vmem_limit_bytes=VMEM_LIMIT),
        name="merge_ffn_prompt",
    )(x1, a_pre, *outs, *lses, *weights)


def _rope_tables(pos):
    half = ROT_DIM // 2
    inv = jnp.float32(ROPE_THETA) ** (-jnp.arange(half, dtype=F32) * (2.0 / ROT_DIM))
    ang = pos.astype(F32)[:, None] * inv[None, :]
    cos, sin = jnp.cos(ang), jnp.sin(ang)
    n = pos.shape[0]
    rest = HEAD_DIM - ROT_DIM
    zh = jnp.zeros((n, half), F32)
    c = jnp.concatenate([cos, cos, jnp.ones((n, rest), F32)], axis=1)
    s1 = jnp.concatenate([zh, sin, jnp.zeros((n, rest), F32)], axis=1)
    s2 = jnp.concatenate([-sin, zh, jnp.zeros((n, rest), F32)], axis=1)
    rep = LANES // HEAD_DIM
    return tuple(jnp.tile(t, (1, rep)) for t in (c, s1, s2))


def _prepare_weights(l, ffn1_norm, ffn1_w_in, ffn1_w_out, mix_norm, w_in, b_gate, q_norm, k_norm, conv_w, conv_b,
                     conv_ln_g, conv_ln_b, w_conv_out, w_attn_out, w_out, ffn2_norm, ffn2_w_in, ffn2_w_out):
    cc = conv_w.shape[2]
    n_a = 2 * cc + 3 * ATTN_W
    lane = jnp.arange(GROUP_W)
    same_head = (lane[:, None] // HEAD_DIM) == (lane[None, :] // HEAD_DIM)
    row = lambda a: a.reshape(1, -1).astype(F32)
    tok = jnp.arange(TOKEN_TILE)

    def perm(d):
        src = (tok % (TOKEN_TILE // d)) * d + tok // (TOKEN_TILE // d)
        return (src[:, None] == tok[None, :]).astype(BF16)

    return {
        "perm1": perm(DILATIONS[1]), "perm2": perm(DILATIONS[2]),
        "permt1": perm(DILATIONS[1]).T, "permt2": perm(DILATIONS[2]).T,
        "ffn1_norm": row(ffn1_norm[l]), "ffn1_wgu": ffn1_w_in[l].astype(BF16), "ffn1_wo": ffn1_w_out[l].astype(BF16),
        "mix_norm": row(mix_norm[l]), "w_in_a": w_in[l][:, :n_a].astype(BF16), "w_in_g": w_in[l][:, n_a:].astype(BF16),
        "b_gate": row(b_gate[l]),
        "head_mean": jnp.where(same_head, 1.0 / HEAD_DIM, 0.0).astype(BF16),
        "q_gain": row(q_norm[l]), "k_gain": row(k_norm[l]),
        "conv_w": jnp.concatenate([conv_w[l], jnp.zeros((CONV_PAD - CONV_WIDTH, cc), F32)], axis=0),
        "conv_b": row(conv_b[l]), "conv_ln_g": row(conv_ln_g[l]), "conv_ln_b": row(conv_ln_b[l]),
        "w_conv_out": w_conv_out[l].astype(BF16), "w_attn_out": w_attn_out[l].astype(BF16),
        "w_out": w_out[l].astype(BF16),
        "ffn2_norm": row(ffn2_norm[l]), "ffn2_wgu": ffn2_w_in[l].astype(BF16), "ffn2_wo": ffn2_w_out[l].astype(BF16),
    }


def kernel(x_prompt, x_sample, cache_k_w128, cache_v_w128, cache_k_w512, cache_v_w512, cache_k_w2048, cache_v_w2048, state_conv, ffn1_norm, ffn1_w_in, ffn1_w_out, mix_norm, w_in, b_gate, q_norm, k_norm, conv_w, conv_b, conv_ln_g, conv_ln_b, w_conv_out, w_attn_out, w_out, ffn2_norm, ffn2_w_in, ffn2_w_out):
    batch, t, d = x_prompt.shape
    db, s_len, _ = x_sample.shape
    depth = w_in.shape[0]
    cc = conv_w.shape[2]
    hd = (HEADS_PER_GROUP, HEAD_DIM)
    cache_k = (cache_k_w128, cache_k_w512, cache_k_w2048)
    cache_v = (cache_v_w128, cache_v_w512, cache_v_w2048)
    for g, wdw in enumerate(WINDOWS):
        assert cache_k[g].shape == (depth, db, wdw, *hd), "sample caches must hold a full window"
        assert t >= wdw
    rope_p = _rope_tables(jnp.arange(t))
    rope_s = _rope_tables(PAST_LEN + jnp.arange(db * s_len) // db)
    to_time_minor = lambda c: jnp.transpose(c, (0, 2, 3, 1))
    from_time_minor = lambda c: jnp.transpose(c, (0, 3, 1, 2))

    xp = x_prompt.reshape(batch * t, d)
    xs = jnp.transpose(x_sample, (1, 0, 2)).reshape(s_len * db, d)
    outs_p = [[] for _ in range(2 * N_GROUPS)]
    outs_s = [[] for _ in range(2 * N_GROUPS)]
    new_conv_p, new_conv_s = [], []
    for l in range(depth):
        w = _prepare_weights(l, ffn1_norm, ffn1_w_in, ffn1_w_out, mix_norm, w_in, b_gate, q_norm, k_norm, conv_w,
                             conv_b, conv_ln_g, conv_ln_b, w_conv_out, w_attn_out, w_out, ffn2_norm, ffn2_w_in,
                             ffn2_w_out)
        x1p, *rest = _ffn_mix_prompt(xp, rope_p, w, t)
        qkv, (a_p, utail), kv_t = rest[:9], rest[9:11], rest[11:]
        attn_p = _attn_prompt(qkv[0:3], qkv[3:6], qkv[6:9], batch, t)
        xp = _merge_ffn_prompt(x1p, a_p, attn_p[0::2], attn_p[1::2], w)
        new_conv_p.append(utail[:, CONV_PAD - (CONV_WIDTH - 1):, :])
        for i, (g, c) in enumerate([(g, c) for c in range(2) for g in range(N_GROUPS)]):
            outs_p[2 * g + c].append(from_time_minor(kv_t[i].reshape(batch, *hd, WINDOWS[g])))
        x1s, qs, k32s, v32s, us = _ffn_mix_sample(xs, rope_s, w)
        a_s, state_new = _conv_sample(jnp.transpose(state_conv[l], (1, 0, 2)), us.reshape(s_len, db, cc), w)
        rows8 = lambda a: jnp.pad(jnp.transpose(a.reshape(s_len, db, ATTN_W), (1, 0, 2)),
                                  ((0, 0), (0, SAMPLE_ROWS - s_len), (0, 0)))
        ck = [to_time_minor(cache_k[g][l]) for g in range(N_GROUPS)]
        cv = [to_time_minor(cache_v[g][l]) for g in range(N_GROUPS)]
        b_s, *new_caches = _attn_sample(rows8(qs), rows8(k32s), rows8(v32s), ck, cv, s_len)
        b_s = jnp.transpose(b_s[:, :s_len], (1, 0, 2)).reshape(s_len * db, GROUP_W)
        xs = _merge_ffn_sample(x1s, a_s.reshape(s_len * db, cc), b_s, w)
        new_conv_s.append(jnp.transpose(state_new, (1, 0, 2)))
        for i, nc in enumerate(new_caches):
            outs_s[i].append(from_time_minor(nc))
    y_s = jnp.transpose(xs.reshape(s_len, db, d), (1, 0, 2))
    return (xp.reshape(batch, t, d), y_s, *[jnp.stack(o) for o in outs_p], jnp.stack(new_conv_p),
            *[jnp.stack(o) for o in outs_s], jnp.stack(new_conv_s))
```

```python
import functools

import jax
import jax.numpy as jnp
from jax import lax
from jax.experimental import pallas as pl
from jax.experimental.pallas import tpu as pltpu

F32 = jnp.float32
BF16 = jnp.bfloat16

HEAD_DIM = 64
WINDOWS = (128, 512, 2048)
DILATIONS = (1, 4, 16)
N_GROUPS = 3
HEADS_PER_GROUP = 4
GROUP_W = HEADS_PER_GROUP * HEAD_DIM
ATTN_W = N_GROUPS * GROUP_W
BLK = WINDOWS[0] // DILATIONS[0]
assert all(w // d == BLK for w, d in zip(WINDOWS, DILATIONS))
ATTN_TILE = max(WINDOWS)
UNITS = ATTN_TILE // BLK
ATTN_SCALE = HEAD_DIM ** -0.5
ROT_DIM = HEAD_DIM // 4
ROPE_THETA = 500000.0
CONV_WIDTH = 31
CONV_PAD = 32
NORM_EPS = 1e-6
PAST_LEN = 2048
NEG = -1e30
LANES = 128
SUBLANES = 8
FF_CHUNK = 256
TOKEN_TILE = 256
SAMPLE_ROWS = 8
CONV_SEQS = 16
VMEM_LIMIT = 56 * 1024 * 1024


def _resident(shape):
    nd = len(shape)
    return pl.BlockSpec(shape, lambda *_: (0,) * nd, pipeline_mode=pl.Buffered(1))


def _rms(x, g):
    ms = jnp.mean(x * x, axis=-1, keepdims=True)
    return x * lax.rsqrt(ms + NORM_EPS) * g


def _swiglu(h, wgu_ref, wo_ref, act_ref):
    f = wo_ref.shape[0]
    assert f % FF_CHUNK == 0
    for c in range(f // FF_CHUNK):
        g = jnp.dot(h, wgu_ref[:, c * FF_CHUNK:(c + 1) * FF_CHUNK], preferred_element_type=F32)
        u = jnp.dot(h, wgu_ref[:, f + c * FF_CHUNK:f + (c + 1) * FF_CHUNK], preferred_element_type=F32)
        act_ref[:, c * FF_CHUNK:(c + 1) * FF_CHUNK] = (g * jax.nn.sigmoid(g) * u).astype(BF16)
    return jnp.dot(act_ref[...], wo_ref[...], preferred_element_type=F32)


def _head_norm(y, hm_ref, gain):
    parts = []
    for g in range(N_GROUPS):
        yg = y[:, g * GROUP_W:(g + 1) * GROUP_W]
        ms = jnp.dot((yg * yg).astype(BF16), hm_ref[...], preferred_element_type=F32)
        parts.append(yg * lax.rsqrt(ms + NORM_EPS))
    return jnp.concatenate(parts, axis=1) * gain


def _rope(y, c, s1, s2):
    half = ROT_DIM // 2
    parts = []
    for i in range(y.shape[1] // LANES):
        yc = y[:, i * LANES:(i + 1) * LANES]
        parts.append(yc * c + pltpu.roll(yc, half, 1) * s1 + pltpu.roll(yc, LANES - half, 1) * s2)
    return jnp.concatenate(parts, axis=1)


def _layernorm_silu(y, g, b):
    mu = jnp.mean(y, axis=-1, keepdims=True)
    yc = y - mu
    var = jnp.mean(yc * yc, axis=-1, keepdims=True)
    yn = yc * lax.rsqrt(var + NORM_EPS) * g + b
    return yn * jax.nn.sigmoid(yn)


def _proj_stage(x_ref, g1_ref, wgu_ref, wo_ref, gm_ref, win_ref, x1_ref, act_ref):
    x = x_ref[...]
    h = _rms(x, g1_ref[...]).astype(BF16)
    x1 = x + 0.5 * _swiglu(h, wgu_ref, wo_ref, act_ref)
    x1_ref[...] = x1
    h2 = _rms(x1, gm_ref[...]).astype(BF16)
    return jnp.dot(h2, win_ref[...], preferred_element_type=F32)


def _mix_stage(z, rc_ref, rs1_ref, rs2_ref, hm_ref, qg_ref, kg_ref):
    cc = (z.shape[1] - 3 * ATTN_W) // 2
    u = z[:, :cc] * jax.nn.sigmoid(z[:, cc:2 * cc])
    c, s1, s2 = rc_ref[...], rs1_ref[...], rs2_ref[...]
    q = _rope(_head_norm(z[:, 2 * cc:2 * cc + ATTN_W], hm_ref, qg_ref[...]), c, s1, s2)
    k = _rope(_head_norm(z[:, 2 * cc + ATTN_W:2 * cc + 2 * ATTN_W], hm_ref, kg_ref[...]), c, s1, s2)
    return u, q, k, z[:, 2 * cc + 2 * ATTN_W:]


def _store_dilated(y, perm_refs, out_refs):
    yb = y.astype(BF16)
    out_refs[0][...] = yb[:, :GROUP_W]
    for g in range(1, N_GROUPS):
        d = DILATIONS[g]
        rows = y.shape[0] // d
        yp = jnp.dot(perm_refs[g - 1][...], yb[:, g * GROUP_W:(g + 1) * GROUP_W],
                     preferred_element_type=F32).astype(BF16)
        for r in range(d):
            out_refs[g][:, r * GROUP_W:(r + 1) * GROUP_W] = yp[r * rows:(r + 1) * rows, :]


def _ffn_mix_prompt_kernel(x_ref, rc_ref, rs1_ref, rs2_ref, g1_ref, wgu_ref, wo_ref, gm_ref, win_ref, hm_ref, qg_ref,
                           kg_ref, cw_ref, cb_ref, lng_ref, lnb_ref, p1_ref, p2_ref,
                           x1_ref, q0_ref, q1_ref, q2_ref, k0_ref, k1_ref, k2_ref, v0_ref, v1_ref, v2_ref, a_ref,
                           utail_ref, kt0_ref, kt1_ref, kt2_ref, vt0_ref, vt1_ref, vt2_ref, act_ref, ext_ref, *,
                           tiles_per_seq):
    tm = x_ref.shape[0]
    z = _proj_stage(x_ref, g1_ref, wgu_ref, wo_ref, gm_ref, win_ref, x1_ref, act_ref)
    u, q, k, v = _mix_stage(z, rc_ref, rs1_ref, rs2_ref, hm_ref, qg_ref, kg_ref)
    perms = (p1_ref, p2_ref)
    _store_dilated(q, perms, (q0_ref, q1_ref, q2_ref))
    _store_dilated(k, perms, (k0_ref, k1_ref, k2_ref))
    _store_dilated(v, perms, (v0_ref, v1_ref, v2_ref))
    for y, refs in ((k, (kt0_ref, kt1_ref, kt2_ref)), (v, (vt0_ref, vt1_ref, vt2_ref))):
        yt = jnp.transpose(y)
        for g, ref in enumerate(refs):
            ref[...] = yt[g * GROUP_W:(g + 1) * GROUP_W, tm - ref.shape[1]:]

    @pl.when(pl.program_id(0) % tiles_per_seq == 0)
    def _():
        ext_ref[0:CONV_PAD, :] = jnp.zeros((CONV_PAD, ext_ref.shape[1]), F32)

    ext_ref[CONV_PAD:CONV_PAD + tm, :] = u
    first = CONV_PAD - (CONV_WIDTH - 1)
    rows = 64
    for r0 in range(0, tm, rows):
        acc = None
        base = ext_ref[r0:r0 + rows + CONV_PAD, :]
        for r in range(SUBLANES):
            win = base if r == 0 else pltpu.roll(base, base.shape[0] - r, 0)
            for j in range(CONV_WIDTH):
                if (first + j) % SUBLANES != r:
                    continue
                o = first + j - r
                term = win[o:o + rows, :] * cw_ref[j:j + 1, :]
                acc = term if acc is None else acc + term
        y = acc + cb_ref[...]
        a_ref[r0:r0 + rows, :] = _layernorm_silu(y, lng_ref[...], lnb_ref[...]).astype(BF16)
    tail = ext_ref[tm:tm + CONV_PAD, :]
    utail_ref[...] = tail
    ext_ref[0:CONV_PAD, :] = tail


def _ffn_mix_sample_kernel(x_ref, rc_ref, rs1_ref, rs2_ref, g1_ref, wgu_ref, wo_ref, gm_ref, win_ref, hm_ref, qg_ref,
                           kg_ref, x1_ref, q_ref, k32_ref, v32_ref, u_ref, act_ref):
    z = _proj_stage(x_ref, g1_ref, wgu_ref, wo_ref, gm_ref, win_ref, x1_ref, act_ref)
    u, q, k, v = _mix_stage(z, rc_ref, rs1_ref, rs2_ref, hm_ref, qg_ref, kg_ref)
    q_ref[...] = q
    u_ref[...] = u
    k32_ref[...] = k
    v32_ref[...] = v


def _ffn_mix_weights(w):
    return [w["ffn1_norm"], w["ffn1_wgu"], w["ffn1_wo"], w["mix_norm"], w["w_in_a"], w["head_mean"],
            w["q_gain"], w["k_gain"]]


def _ffn_mix_sample(x, rope, w):
    n, d = x.shape
    tm = TOKEN_TILE
    assert n % tm == 0 and rope[0].shape[0] == n
    cc = w["conv_w"].shape[1]
    tok = lambda width: pl.BlockSpec((tm, width), lambda i: (i, 0))
    weights = _ffn_mix_weights(w)
    widths = [d, ATTN_W, ATTN_W, ATTN_W, cc]
    return pl.pallas_call(
        _ffn_mix_sample_kernel, grid=(n // tm,),
        in_specs=[tok(d)] + [tok(LANES)] * 3 + [_resident(a.shape) for a in weights],
        out_specs=[tok(wd) for wd in widths], out_shape=[jax.ShapeDtypeStruct((n, wd), F32) for wd in widths],
        scratch_shapes=[pltpu.VMEM((tm, w["ffn1_wo"].shape[0]), BF16)],
        compiler_params=pltpu.CompilerParams(dimension_semantics=("arbitrary",), vmem_limit_bytes=VMEM_LIMIT),
        name="ffn_mix_sample",
    )(x, *rope, *weights)


def _ffn_mix_prompt(x, rope, w, seq_len):
    n, d = x.shape
    tm = TOKEN_TILE
    t = seq_len
    assert n % t == 0 and t % tm == 0 and rope[0].shape[0] == t
    assert all(wd % tm == 0 or tm % wd == 0 for wd in WINDOWS)
    tps = t // tm
    nt = n // tm
    nseq = n // t
    cc = w["conv_w"].shape[1]
    weights = _ffn_mix_weights(w) + [w["conv_w"], w["conv_b"], w["conv_ln_g"], w["conv_ln_b"], w["perm1"],
                                     w["perm2"]]
    tok = lambda rows, width: pl.BlockSpec((rows, width), lambda i: (i, 0))
    rope_spec = pl.BlockSpec((tm, LANES), lambda i: (i % tps, 0))
    in_specs = [tok(tm, d)] + [rope_spec] * 3 + [_resident(a.shape) for a in weights]
    out_shape = [jax.ShapeDtypeStruct((n, d), F32)]
    out_specs = [tok(tm, d)]
    for _ in range(3):
        for dl in DILATIONS:
            out_shape.append(jax.ShapeDtypeStruct((n // dl, dl * GROUP_W), BF16))
            out_specs.append(tok(tm // dl, dl * GROUP_W))
    out_shape += [jax.ShapeDtypeStruct((n, cc), BF16), jax.ShapeDtypeStruct((nseq, CONV_PAD, cc), F32)]
    out_specs += [tok(tm, cc), pl.BlockSpec((None, CONV_PAD, cc), lambda i: (i // tps, 0, 0))]
    for _ in range(2):
        for wd in WINDOWS:
            first = (t - wd) // tm if wd >= tm else tps - 1
            out_shape.append(jax.ShapeDtypeStruct((nseq, GROUP_W, wd), F32))
            out_specs.append(pl.BlockSpec(
                (None, GROUP_W, min(wd, tm)),
                lambda i, first=first: (i // tps, 0, jnp.maximum(i % tps - first, 0))))
    return pl.pallas_call(
        functools.partial(_ffn_mix_prompt_kernel, tiles_per_seq=tps), grid=(nt,), in_specs=in_specs,
        out_specs=out_specs, out_shape=out_shape,
        scratch_shapes=[pltpu.VMEM((tm, w["ffn1_wo"].shape[0]), BF16), pltpu.VMEM((tm + CONV_PAD, cc), F32)],
        compiler_params=pltpu.CompilerParams(dimension_semantics=("arbitrary",), vmem_limit_bytes=VMEM_LIMIT),
        name="ffn_mix_prompt",
    )(x, *rope, *weights)


def _attn_unit(q, kp, kc, vp, vc, first):
    row = lax.broadcasted_iota(jnp.int32, (BLK, BLK), 0)
    col = lax.broadcasted_iota(jnp.int32, (BLK, BLK), 1)
    bias_p = jnp.where(col >= row, 0.0, NEG) + jnp.where(first, NEG, 0.0)
    bias_c = jnp.where(col <= row, 0.0, NEG)
    head = lax.broadcasted_iota(jnp.int32, (1, GROUP_W), 1) // HEAD_DIM
    contract_last = (((1,), (1,)), ((), ()))
    out = jnp.zeros((BLK, GROUP_W), F32)
    lse = jnp.zeros((BLK, GROUP_W), F32)
    zero = jnp.zeros((), BF16)
    for h in range(HEADS_PER_GROUP):
        hm = head == h
        qh = jnp.where(hm, q, zero)
        sp = lax.dot_general(qh, kp, contract_last, preferred_element_type=F32) * ATTN_SCALE + bias_p
        sc = lax.dot_general(qh, kc, contract_last, preferred_element_type=F32) * ATTN_SCALE + bias_c
        m = jnp.maximum(jnp.max(sp, axis=1, keepdims=True), jnp.max(sc, axis=1, keepdims=True))
        pp = jnp.exp(sp - m)
        pc = jnp.exp(sc - m)
        l = jnp.sum(pp, axis=1, keepdims=True) + jnp.sum(pc, axis=1, keepdims=True)
        o = jnp.dot(pp.astype(BF16), jnp.where(hm, vp, zero), preferred_element_type=F32) \
            + jnp.dot(pc.astype(BF16), jnp.where(hm, vc, zero), preferred_element_type=F32)
        out = out + o / l
        lse = lse + jnp.where(hm, m + jnp.log(l), 0.0)
    return out, lse


def _attn_prompt_body(ins, outs, t, u):
    for g in range(N_GROUPS):
        q_ref, kc_ref, kp_ref, vc_ref, vp_ref = ins[5 * g:5 * g + 5]
        d = DILATIONS[g]
        first = (t * (UNITS // d) + u // d) == 0
        o, lse = _attn_unit(q_ref[...], kp_ref[...], kc_ref[...], vp_ref[...], vc_ref[...], first)
        outs[2 * g][...] = o.astype(BF16)
        outs[2 * g + 1][...] = lse


def _attn_prompt_kernel(*refs):
    _attn_prompt_body(refs[:5 * N_GROUPS], refs[5 * N_GROUPS:], pl.program_id(1), pl.program_id(2))


def _attn_prompt_specs(qd, kd, vd, batch, seq_len, index):
    assert seq_len % ATTN_TILE == 0
    args, in_specs, out_shape, out_specs = [], [], [], []
    for g in range(N_GROUPS):
        d = DILATIONS[g]
        nb = UNITS // d

        def cur(*i, d=d, nb=nb):
            b, t, u = index(*i)
            return (b, t * nb + u // d, u % d)

        def prev(*i, d=d, nb=nb):
            b, t, u = index(*i)
            return (b, jnp.maximum(t * nb + u // d - 1, 0), u % d)

        view = lambda a, d=d: a.reshape(batch, seq_len // d, d * GROUP_W)
        blk = lambda im: pl.BlockSpec((None, BLK, GROUP_W), im)
        args += [view(qd[g]), view(kd[g]), view(kd[g]), view(vd[g]), view(vd[g])]
        in_specs += [blk(cur), blk(cur), blk(prev), blk(cur), blk(prev)]
        out_shape += [jax.ShapeDtypeStruct((batch, seq_len // d, d * GROUP_W), BF16),
                      jax.ShapeDtypeStruct((batch, seq_len // d, d * GROUP_W), F32)]
        out_specs += [blk(cur), blk(cur)]
    return args, in_specs, out_shape, out_specs


def _attn_prompt(qd, kd, vd, batch, seq_len):
    args, in_specs, out_shape, out_specs = _attn_prompt_specs(qd, kd, vd, batch, seq_len, lambda b, t, u: (b, t, u))
    res = pl.pallas_call(
        _attn_prompt_kernel, grid=(batch, seq_len // ATTN_TILE, UNITS), in_specs=in_specs, out_specs=out_specs,
        out_shape=out_shape,
        compiler_params=pltpu.CompilerParams(dimension_semantics=("arbitrary",) * 3),
        name="attn_prompt",
    )(*args)
    return [r.reshape(-1, r.shape[2]) for r in res]


def _conv_sample_kernel(state_ref, u_ref, cw_ref, cb_ref, lng_ref, lnb_ref, a_ref, new_state_ref):
    hist = state_ref.shape[0]
    s_len = u_ref.shape[0]
    row = lambda t: state_ref[t] if t < hist else u_ref[t - hist]
    for s in range(s_len):
        acc = None
        for j in range(CONV_WIDTH):
            term = row(s + j) * cw_ref[j:j + 1, :]
            acc = term if acc is None else acc + term
        a_ref[s] = _layernorm_silu(acc + cb_ref[...], lng_ref[...], lnb_ref[...])
    for t in range(hist):
        new_state_ref[t] = row(t + s_len)


def _conv_sample(state_t, u_t, w):
    hist, db, cc = state_t.shape
    s_len = u_t.shape[0]
    assert hist == CONV_WIDTH - 1 and db % CONV_SEQS == 0
    conv_w = [w["conv_w"], w["conv_b"], w["conv_ln_g"], w["conv_ln_b"]]
    slab = lambda rows: pl.BlockSpec((rows, CONV_SEQS, cc), lambda i: (0, i, 0))
    return pl.pallas_call(
        _conv_sample_kernel, grid=(db // CONV_SEQS,),
        in_specs=[slab(hist), slab(s_len)] + [_resident(a.shape) for a in conv_w],
        out_specs=[slab(s_len), slab(hist)],
        out_shape=[jax.ShapeDtypeStruct((s_len, db, cc), F32), jax.ShapeDtypeStruct((hist, db, cc), F32)],
        compiler_params=pltpu.CompilerParams(dimension_semantics=("arbitrary",)),
        name="conv_sample",
    )(state_t, u_t, *conv_w)


def _sample_bias(width, d, s_len):
    s = lax.broadcasted_iota(jnp.int32, (SAMPLE_ROWS, width), 0)
    t = lax.broadcasted_iota(jnp.int32, (SAMPLE_ROWS, width), 1)
    same_residue = jnp.bitwise_and(s - t, d - 1) == 0
    live = s < s_len
    old = jnp.where(live & same_residue & (t >= s), 0.0, NEG)
    sn = s[:, :LANES]
    tn = t[:, :LANES]
    new = jnp.where(live[:, :LANES] & same_residue[:, :LANES] & (tn <= sn), 0.0, NEG)
    return old, new


def _attn_sample_kernel(*refs, s_len):
    _attn_sample_body(*refs, s_len=s_len)


def _attn_sample_body(q_ref, kn_ref, vn_ref, kc0_ref, vc0_ref, kc1_ref, vc1_ref, kc2_ref, vc2_ref,
                      b_ref, nk0_ref, nv0_ref, nk1_ref, nv1_ref, nk2_ref, nv2_ref, *, s_len):
    kc_refs = (kc0_ref, kc1_ref, kc2_ref)
    vc_refs = (vc0_ref, vc1_ref, vc2_ref)
    nk_refs = (nk0_ref, nk1_ref, nk2_ref)
    nv_refs = (nv0_ref, nv1_ref, nv2_ref)
    q = q_ref[...]
    pad = jnp.zeros((LANES - SAMPLE_ROWS, ATTN_W), F32)
    knt = jnp.transpose(jnp.concatenate([kn_ref[...], pad], axis=0))
    vnt = jnp.transpose(jnp.concatenate([vn_ref[...], pad], axis=0))
    biases = [_sample_bias(WINDOWS[g], DILATIONS[g], s_len) for g in range(N_GROUPS)]
    lane = lax.broadcasted_iota(jnp.int32, (1, LANES), 1)
    keep = lane < LANES - s_len
    nt = (((1,), (1,)), ((), ()))

    def shifted(old, new_cols, out_ref, h):
        w = old.shape[1]
        cols = [old[:, j * LANES:(j + 1) * LANES] for j in range(w // LANES)] + [new_cols]
        rolled = [pltpu.roll(c, LANES - s_len, 1) for c in cols]
        for j in range(w // LANES):
            out_ref[h, :, j * LANES:(j + 1) * LANES] = jnp.where(keep, rolled[j], rolled[j + 1])

    for h in range(HEADS_PER_GROUP):
        s_old, s_new = [], []
        for g in range(N_GROUPS):
            c = g * HEADS_PER_GROUP + h
            hs = slice(c * HEAD_DIM, (c + 1) * HEAD_DIM)
            qh = q[:, hs].astype(BF16)
            k_old = kc_refs[g][h]
            k_new = knt[hs, :]
            s_old.append(jnp.dot(qh, k_old.astype(BF16), preferred_element_type=F32) * ATTN_SCALE + biases[g][0])
            s_new.append(jnp.dot(qh, k_new.astype(BF16), preferred_element_type=F32) * ATTN_SCALE + biases[g][1])
            shifted(k_old, k_new, nk_refs[g], h)
        m = None
        for a in s_old + s_new:
            am = jnp.max(a, axis=1, keepdims=True)
            m = am if m is None else jnp.maximum(m, am)
        num = jnp.zeros((SAMPLE_ROWS, HEAD_DIM), F32)
        den = jnp.zeros((SAMPLE_ROWS, 1), F32)
        for g in range(N_GROUPS):
            c = g * HEADS_PER_GROUP + h
            hs = slice(c * HEAD_DIM, (c + 1) * HEAD_DIM)
            p_old = jnp.exp(s_old[g] - m)
            p_new = jnp.exp(s_new[g] - m)
            v_old = vc_refs[g][h]
            v_new = vnt[hs, :]
            den = den + jnp.sum(p_old, axis=1, keepdims=True) + jnp.sum(p_new, axis=1, keepdims=True)
            num = num + lax.dot_general(p_old.astype(BF16), v_old.astype(BF16), nt, preferred_element_type=F32) \
                + lax.dot_general(p_new.astype(BF16), v_new.astype(BF16), nt, preferred_element_type=F32)
            shifted(v_old, v_new, nv_refs[g], h)
        b_ref[:, h * HEAD_DIM:(h + 1) * HEAD_DIM] = num / den


def _attn_sample_specs(q8, kn8, vn8, caches_k, caches_v, s_len):
    db = q8.shape[0]
    assert q8.shape[1] == SAMPLE_ROWS and s_len <= min(SAMPLE_ROWS, DILATIONS[1])
    tok = pl.BlockSpec((None, SAMPLE_ROWS, ATTN_W), lambda b: (b, 0, 0))
    args, in_specs, out_shape, out_specs = [q8, kn8, vn8], [tok] * 3, [], []
    out_shape.append(jax.ShapeDtypeStruct((db, SAMPLE_ROWS, GROUP_W), F32))
    out_specs.append(pl.BlockSpec((None, SAMPLE_ROWS, GROUP_W), lambda b: (b, 0, 0)))
    for g in range(N_GROUPS):
        shape = (db, HEADS_PER_GROUP, HEAD_DIM, WINDOWS[g])
        spec = pl.BlockSpec((None,) + shape[1:], lambda b: (b, 0, 0, 0))
        for c in (caches_k[g], caches_v[g]):
            assert c.shape == shape
            args.append(c)
            in_specs.append(spec)
            out_shape.append(jax.ShapeDtypeStruct(shape, F32))
            out_specs.append(spec)
    return args, in_specs, out_shape, out_specs


def _attn_sample(q8, kn8, vn8, caches_k, caches_v, s_len):
    db = q8.shape[0]
    args, in_specs, out_shape, out_specs = _attn_sample_specs(q8, kn8, vn8, caches_k, caches_v, s_len)
    return pl.pallas_call(
        functools.partial(_attn_sample_kernel, s_len=s_len), grid=(db,), in_specs=in_specs, out_specs=out_specs,
        out_shape=out_shape,
        compiler_params=pltpu.CompilerParams(dimension_semantics=("arbitrary",), vmem_limit_bytes=VMEM_LIMIT),
        name="attn_sample",
    )(*args)


def _attn_both_kernel(*refs, s_len, tiles):
    n_pi, n_si = 5 * N_GROUPS, 3 + 2 * N_GROUPS
    n_po = 2 * N_GROUPS
    p_in, s_in = refs[:n_pi], refs[n_pi:n_pi + n_si]
    p_out, s_out = refs[n_pi + n_si:n_pi + n_si + n_po], refs[n_pi + n_si + n_po:]
    i = pl.program_id(0)
    _attn_prompt_body(p_in, p_out, (i // UNITS) % tiles, i % UNITS)
    _attn_sample_body(*s_in, *s_out, s_len=s_len)


def _attn_both(qd, kd, vd, batch, seq_len, q8, kn8, vn8, caches_k, caches_v, s_len):
    tiles = seq_len // ATTN_TILE
    n = batch * tiles * UNITS
    assert n == q8.shape[0]
    index = lambda i: (i // (tiles * UNITS), (i // UNITS) % tiles, i % UNITS)
    p_args, p_in, p_shape, p_out = _attn_prompt_specs(qd, kd, vd, batch, seq_len, index)
    s_args, s_in, s_shape, s_out = _attn_sample_specs(q8, kn8, vn8, caches_k, caches_v, s_len)
    res = pl.pallas_call(
        functools.partial(_attn_both_kernel, s_len=s_len, tiles=tiles), grid=(n,), in_specs=p_in + s_in,
        out_specs=p_out + s_out, out_shape=p_shape + s_shape,
        compiler_params=pltpu.CompilerParams(dimension_semantics=("arbitrary",), vmem_limit_bytes=VMEM_LIMIT),
        name="attn_both",
    )(*p_args, *s_args)
    return [r.reshape(-1, r.shape[2]) for r in res[:len(p_shape)]], res[len(p_shape):]


def _merge_stage(x1_ref, a_ref, battn, gm_ref, wg_ref, bg_ref, wco_ref, wao_ref, wout_ref):
    x1 = x1_ref[...]
    d = x1.shape[1]
    h = _rms(x1, gm_ref[...]).astype(BF16)
    gates = jax.nn.sigmoid(jnp.dot(h, wg_ref[...], preferred_element_type=F32) + bg_ref[...])
    a = jnp.dot(a_ref[...].astype(BF16), wco_ref[...], preferred_element_type=F32)
    b = jnp.dot(battn.astype(BF16), wao_ref[...], preferred_element_type=F32)
    mix = (gates[:, :d] * a + gates[:, d:] * b).astype(BF16)
    return x1 + jnp.dot(mix, wout_ref[...], preferred_element_type=F32)


def _ffn_stage(y, g2_ref, wgu_ref, wo_ref, act_ref):
    return y + 0.5 * _swiglu(_rms(y, g2_ref[...]).astype(BF16), wgu_ref, wo_ref, act_ref)


def _undilate(blk, pt_ref, d):
    if d == 1:
        return blk.astype(F32)
    rows = jnp.concatenate([blk[:, r * GROUP_W:(r + 1) * GROUP_W] for r in range(d)], axis=0)
    pt = pt_ref[...]
    if rows.dtype == BF16:
        return jnp.dot(pt, rows, preferred_element_type=F32)
    hi = rows.astype(BF16)
    rest = rows - hi.astype(F32)
    mid = rest.astype(BF16)
    lo = (rest - mid.astype(F32)).astype(BF16)
    return jnp.dot(pt, hi, preferred_element_type=F32) + jnp.dot(pt, mid, preferred_element_type=F32) \
        + jnp.dot(pt, lo, preferred_element_type=F32)


def _merge_ffn_prompt_kernel(x1_ref, a_ref, o0_ref, o1_ref, o2_ref, l0_ref, l1_ref, l2_ref, pt1_ref, pt2_ref,
                             gm_ref, wg_ref, bg_ref, wco_ref, wao_ref, wout_ref, g2_ref, wgu_ref, wo_ref,
                             out_ref, act_ref):
    pts = (None, pt1_ref, pt2_ref)
    lses = [_undilate(r[...], pts[g], DILATIONS[g]) for g, r in enumerate((l0_ref, l1_ref, l2_ref))]
    outs = [_undilate(r[...], pts[g], DILATIONS[g]) for g, r in enumerate((o0_ref, o1_ref, o2_ref))]
    mx = jnp.maximum(jnp.maximum(lses[0], lses[1]), lses[2])
    num = None
    den = None
    for o, l in zip(outs, lses):
        wgt = jnp.exp(l - mx)
        num = wgt * o if num is None else num + wgt * o
        den = wgt if den is None else den + wgt
    y = _merge_stage(x1_ref, a_ref, num / den, gm_ref, wg_ref, bg_ref, wco_ref, wao_ref, wout_ref)
    out_ref[...] = _ffn_stage(y, g2_ref, wgu_ref, wo_ref, act_ref)


def _merge_ffn_sample_kernel(x1_ref, a_ref, b_ref, gm_ref, wg_ref, bg_ref, wco_ref, wao_ref, wout_ref, g2_ref,
                             wgu_ref, wo_ref, out_ref, act_ref):
    y = _merge_stage(x1_ref, a_ref, b_ref[...], gm_ref, wg_ref, bg_ref, wco_ref, wao_ref, wout_ref)
    out_ref[...] = _ffn_stage(y, g2_ref, wgu_ref, wo_ref, act_ref)


def _merge_weights(w):
    return [w["mix_norm"], w["w_in_g"], w["b_gate"], w["w_conv_out"], w["w_attn_out"], w["w_out"], w["ffn2_norm"],
            w["ffn2_wgu"], w["ffn2_wo"]]


def _merge_ffn_sample(x1, a_pre, b, w):
    n, d = x1.shape
    tm = TOKEN_TILE
    assert n % tm == 0
    tok = lambda width: pl.BlockSpec((tm, width), lambda i: (i, 0))
    weights = _merge_weights(w)
    return pl.pallas_call(
        _merge_ffn_sample_kernel, grid=(n // tm,),
        in_specs=[tok(d), tok(a_pre.shape[1]), tok(GROUP_W)] + [_resident(a.shape) for a in weights],
        out_specs=tok(d), out_shape=jax.ShapeDtypeStruct((n, d), F32),
        scratch_shapes=[pltpu.VMEM((tm, w["ffn2_wo"].shape[0]), BF16)],
        compiler_params=pltpu.CompilerParams(dimension_semantics=("arbitrary",), vmem_limit_bytes=VMEM_LIMIT),
        name="merge_ffn_sample",
    )(x1, a_pre, b, *weights)


def _merge_ffn_prompt(x1, a_pre, outs, lses, w):
    n, d = x1.shape
    tm = TOKEN_TILE
    assert n % tm == 0
    tok = lambda rows, width: pl.BlockSpec((rows, width), lambda i: (i, 0))
    grp = [tok(tm // dl, dl * GROUP_W) for dl in DILATIONS]
    weights = [w["permt1"], w["permt2"]] + _merge_weights(w)
    return pl.pallas_call(
        _merge_ffn_prompt_kernel, grid=(n // tm,),
        in_specs=[tok(tm, d), tok(tm, a_pre.shape[1])] + grp + grp + [_resident(a.shape) for a in weights],
        out_specs=tok(tm, d), out_shape=jax.ShapeDtypeStruct((n, d), F32),
        scratch_shapes=[pltpu.VMEM((tm, w["ffn2_wo"].shape[0]), BF16)],
        compiler_params=pltpu.CompilerParams(dimension_semantics=("arbitrary",), vmem_limit_bytes=VMEM_LIMIT),
        name="merge_ffn_prompt",
    )(x1, a_pre, *outs, *lses, *weights)


def _rope_tables(pos):
    half = ROT_DIM // 2
    inv = jnp.float32(ROPE_THETA) ** (-jnp.arange(half, dtype=F32) * (2.0 / ROT_DIM))
    ang = pos.astype(F32)[:, None] * inv[None, :]
    cos, sin = jnp.cos(ang), jnp.sin(ang)
    n = pos.shape[0]
    rest = HEAD_DIM - ROT_DIM
    zh = jnp.zeros((n, half), F32)
    c = jnp.concatenate([cos, cos, jnp.ones((n, rest), F32)], axis=1)
    s1 = jnp.concatenate([zh, sin, jnp.zeros((n, rest), F32)], axis=1)
    s2 = jnp.concatenate([-sin, zh, jnp.zeros((n, rest), F32)], axis=1)
    rep = LANES // HEAD_DIM
    return tuple(jnp.tile(t, (1, rep)) for t in (c, s1, s2))


def _prepare_weights(l, ffn1_norm, ffn1_w_in, ffn1_w_out, mix_norm, w_in, b_gate, q_norm, k_norm, conv_w, conv_b,
                     conv_ln_g, conv_ln_b, w_conv_out, w_attn_out, w_out, ffn2_norm, ffn2_w_in, ffn2_w_out):
    cc = conv_w.shape[2]
    n_a = 2 * cc + 3 * ATTN_W
    lane = jnp.arange(GROUP_W)
    same_head = (lane[:, None] // HEAD_DIM) == (lane[None, :] // HEAD_DIM)
    row = lambda a: a.reshape(1, -1).astype(F32)
    tok = jnp.arange(TOKEN_TILE)

    def perm(d):
        src = (tok % (TOKEN_TILE // d)) * d + tok // (TOKEN_TILE // d)
        return (src[:, None] == tok[None, :]).astype(BF16)

    return {
        "perm1": perm(DILATIONS[1]), "perm2": perm(DILATIONS[2]),
        "permt1": perm(DILATIONS[1]).T, "permt2": perm(DILATIONS[2]).T,
        "ffn1_norm": row(ffn1_norm[l]), "ffn1_wgu": ffn1_w_in[l].astype(BF16), "ffn1_wo": ffn1_w_out[l].astype(BF16),
        "mix_norm": row(mix_norm[l]), "w_in_a": w_in[l][:, :n_a].astype(BF16), "w_in_g": w_in[l][:, n_a:].astype(BF16),
        "b_gate": row(b_gate[l]),
        "head_mean": jnp.where(same_head, 1.0 / HEAD_DIM, 0.0).astype(BF16),
        "q_gain": row(q_norm[l]), "k_gain": row(k_norm[l]),
        "conv_w": jnp.concatenate([conv_w[l], jnp.zeros((CONV_PAD - CONV_WIDTH, cc), F32)], axis=0),
        "conv_b": row(conv_b[l]), "conv_ln_g": row(conv_ln_g[l]), "conv_ln_b": row(conv_ln_b[l]),
        "w_conv_out": w_conv_out[l].astype(BF16), "w_attn_out": w_attn_out[l].astype(BF16),
        "w_out": w_out[l].astype(BF16),
        "ffn2_norm": row(ffn2_norm[l]), "ffn2_wgu": ffn2_w_in[l].astype(BF16), "ffn2_wo": ffn2_w_out[l].astype(BF16),
    }


def kernel(x_prompt, x_sample, cache_k_w128, cache_v_w128, cache_k_w512, cache_v_w512, cache_k_w2048, cache_v_w2048, state_conv, ffn1_norm, ffn1_w_in, ffn1_w_out, mix_norm, w_in, b_gate, q_norm, k_norm, conv_w, conv_b, conv_ln_g, conv_ln_b, w_conv_out, w_attn_out, w_out, ffn2_norm, ffn2_w_in, ffn2_w_out):
    batch, t, d = x_prompt.shape
    db, s_len, _ = x_sample.shape
    depth = w_in.shape[0]
    cc = conv_w.shape[2]
    hd = (HEADS_PER_GROUP, HEAD_DIM)
    cache_k = (cache_k_w128, cache_k_w512, cache_k_w2048)
    cache_v = (cache_v_w128, cache_v_w512, cache_v_w2048)
    for g, wdw in enumerate(WINDOWS):
        assert cache_k[g].shape == (depth, db, wdw, *hd), "sample caches must hold a full window"
        assert t >= wdw
    rope_p = _rope_tables(jnp.arange(t))
    rope_s = _rope_tables(PAST_LEN + jnp.arange(db * s_len) // db)
    to_time_minor = lambda c: jnp.transpose(c, (0, 2, 3, 1))
    from_time_minor = lambda c: jnp.transpose(c, (0, 3, 1, 2))

    xp = x_prompt.reshape(batch * t, d)
    xs = jnp.transpose(x_sample, (1, 0, 2)).reshape(s_len * db, d)
    outs_p = [[] for _ in range(2 * N_GROUPS)]
    outs_s = [[] for _ in range(2 * N_GROUPS)]
    new_conv_p, new_conv_s = [], []
    for l in range(depth):
        w = _prepare_weights(l, ffn1_norm, ffn1_w_in, ffn1_w_out, mix_norm, w_in, b_gate, q_norm, k_norm, conv_w,
                             conv_b, conv_ln_g, conv_ln_b, w_conv_out, w_attn_out, w_out, ffn2_norm, ffn2_w_in,
                             ffn2_w_out)
        x1p, *rest = _ffn_mix_prompt(xp, rope_p, w, t)
        qkv, (a_p, utail), kv_t = rest[:9], rest[9:11], rest[11:]
        x1s, qs, k32s, v32s, us = _ffn_mix_sample(xs, rope_s, w)
        a_s, state_new = _conv_sample(jnp.transpose(state_conv[l], (1, 0, 2)), us.reshape(s_len, db, cc), w)
        rows8 = lambda a: jnp.pad(jnp.transpose(a.reshape(s_len, db, ATTN_W), (1, 0, 2)),
                                  ((0, 0), (0, SAMPLE_ROWS - s_len), (0, 0)))
        ck = [to_time_minor(cache_k[g][l]) for g in range(N_GROUPS)]
        cv = [to_time_minor(cache_v[g][l]) for g in range(N_GROUPS)]
        if batch * (t // ATTN_TILE) * UNITS == db:
            attn_p, (b_s, *new_caches) = _attn_both(qkv[0:3], qkv[3:6], qkv[6:9], batch, t, rows8(qs), rows8(k32s),
                                                    rows8(v32s), ck, cv, s_len)
        else:
            attn_p = _attn_prompt(qkv[0:3], qkv[3:6], qkv[6:9], batch, t)
            b_s, *new_caches = _attn_sample(rows8(qs), rows8(k32s), rows8(v32s), ck, cv, s_len)
        xp = _merge_ffn_prompt(x1p, a_p, attn_p[0::2], attn_p[1::2], w)
        b_s = jnp.transpose(b_s[:, :s_len], (1, 0, 2)).reshape(s_len * db, GROUP_W)
        xs = _merge_ffn_sample(x1s, a_s.reshape(s_len * db, cc), b_s, w)
        new_conv_p.append(utail[:, CONV_PAD - (CONV_WIDTH - 1):, :])
        new_conv_s.append(jnp.transpose(state_new, (1, 0, 2)))
        for i, (g, c) in enumerate([(g, c) for c in range(2) for g in range(N_GROUPS)]):
            outs_p[2 * g + c].append(from_time_minor(kv_t[i].reshape(batch, *hd, WINDOWS[g])))
        for i, nc in enumerate(new_caches):
            outs_s[i].append(from_time_minor(nc))
    y_s = jnp.transpose(xs.reshape(s_len, db, d), (1, 0, 2))
    return (xp.reshape(batch, t, d), y_s, *[jnp.stack(o) for o in outs_p], jnp.stack(new_conv_p),
            *[jnp.stack(o) for o in outs_s], jnp.stack(new_conv_s))
```

```python
import functools

import jax
import jax.numpy as jnp
from jax import lax
from jax.experimental import pallas as pl
from jax.experimental.pallas import tpu as pltpu

F32 = jnp.float32
BF16 = jnp.bfloat16

HEAD_DIM = 64
WINDOWS = (128, 512, 2048)
DILATIONS = (1, 4, 16)
N_GROUPS = 3
HEADS_PER_GROUP = 4
GROUP_W = HEADS_PER_GROUP * HEAD_DIM
ATTN_W = N_GROUPS * GROUP_W
BLK = WINDOWS[0] // DILATIONS[0]
assert all(w // d == BLK for w, d in zip(WINDOWS, DILATIONS))
ATTN_TILE = max(WINDOWS)
UNITS = ATTN_TILE // BLK
ATTN_SCALE = HEAD_DIM ** -0.5
ROT_DIM = HEAD_DIM // 4
ROPE_THETA = 500000.0
CONV_WIDTH = 31
CONV_PAD = 32
NORM_EPS = 1e-6
PAST_LEN = 2048
NEG = -1e30
LANES = 128
SUBLANES = 8
FF_CHUNK = 256
TOKEN_TILE = 256
SAMPLE_ROWS = 8
CONV_SEQS = 16
VMEM_LIMIT = 56 * 1024 * 1024


def _resident(shape):
    nd = len(shape)
    return pl.BlockSpec(shape, lambda *_: (0,) * nd, pipeline_mode=pl.Buffered(1))


def _rms(x, g):
    ms = jnp.mean(x * x, axis=-1, keepdims=True)
    return x * lax.rsqrt(ms + NORM_EPS) * g


def _swiglu(h, wgu_ref, wo_ref, act_ref):
    f = wo_ref.shape[0]
    assert f % FF_CHUNK == 0
    for c in range(f // FF_CHUNK):
        g = jnp.dot(h, wgu_ref[:, c * FF_CHUNK:(c + 1) * FF_CHUNK], preferred_element_type=F32)
        u = jnp.dot(h, wgu_ref[:, f + c * FF_CHUNK:f + (c + 1) * FF_CHUNK], preferred_element_type=F32)
        act_ref[:, c * FF_CHUNK:(c + 1) * FF_CHUNK] = (g * jax.nn.sigmoid(g) * u).astype(BF16)
    return jnp.dot(act_ref[...], wo_ref[...], preferred_element_type=F32)


def _head_norm(y, hm_ref, gain):
    parts = []
    for g in range(N_GROUPS):
        yg = y[:, g * GROUP_W:(g + 1) * GROUP_W]
        ms = jnp.dot((yg * yg).astype(BF16), hm_ref[...], preferred_element_type=F32)
        parts.append(yg * lax.rsqrt(ms + NORM_EPS))
    return jnp.concatenate(parts, axis=1) * gain


def _rope(y, c, s1, s2):
    half = ROT_DIM // 2
    parts = []
    for i in range(y.shape[1] // LANES):
        yc = y[:, i * LANES:(i + 1) * LANES]
        parts.append(yc * c + pltpu.roll(yc, half, 1) * s1 + pltpu.roll(yc, LANES - half, 1) * s2)
    return jnp.concatenate(parts, axis=1)


def _layernorm_silu(y, g, b):
    mu = jnp.mean(y, axis=-1, keepdims=True)
    yc = y - mu
    var = jnp.mean(yc * yc, axis=-1, keepdims=True)
    yn = yc * lax.rsqrt(var + NORM_EPS) * g + b
    return yn * jax.nn.sigmoid(yn)


def _causal_conv(ext, cw_ref, cb_ref, lng_ref, lnb_ref):
    tm = ext.shape[0] - CONV_PAD
    first = CONV_PAD - (CONV_WIDTH - 1)
    rows = 64
    outs = []
    for r0 in range(0, tm, rows):
        acc = None
        base = ext[r0:r0 + rows + CONV_PAD, :]
        for r in range(SUBLANES):
            win = base if r == 0 else pltpu.roll(base, base.shape[0] - r, 0)
            for j in range(CONV_WIDTH):
                if (first + j) % SUBLANES != r:
                    continue
                o = first + j - r
                term = win[o:o + rows, :] * cw_ref[j:j + 1, :]
                acc = term if acc is None else acc + term
        outs.append(_layernorm_silu(acc + cb_ref[...], lng_ref[...], lnb_ref[...]))
    return jnp.concatenate(outs, axis=0)


def _proj_stage(x_ref, g1_ref, wgu_ref, wo_ref, gm_ref, x1_ref, act_ref):
    x = x_ref[...]
    h = _rms(x, g1_ref[...]).astype(BF16)
    x1 = x + 0.5 * _swiglu(h, wgu_ref, wo_ref, act_ref)
    x1_ref[...] = x1
    return _rms(x1, gm_ref[...]).astype(BF16)


def _mix_parts(h2, win_ref, rc_ref, rs1_ref, rs2_ref, hm_ref, qg_ref, kg_ref):
    cc = (win_ref.shape[1] - 3 * ATTN_W) // 2
    proj = lambda lo, hi: jnp.dot(h2, win_ref[:, lo:hi], preferred_element_type=F32)
    zu = proj(0, 2 * cc)
    yield "u", zu[:, :cc] * jax.nn.sigmoid(zu[:, cc:])
    c, s1, s2 = rc_ref[...], rs1_ref[...], rs2_ref[...]
    yield "q", _rope(_head_norm(proj(2 * cc, 2 * cc + ATTN_W), hm_ref, qg_ref[...]), c, s1, s2)
    yield "k", _rope(_head_norm(proj(2 * cc + ATTN_W, 2 * cc + 2 * ATTN_W), hm_ref, kg_ref[...]), c, s1, s2)
    yield "v", proj(2 * cc + 2 * ATTN_W, 2 * cc + 3 * ATTN_W)


def _store_dilated(y, perm_refs, out_refs):
    yb = y.astype(BF16)
    out_refs[0][...] = yb[:, :GROUP_W]
    for g in range(1, N_GROUPS):
        d = DILATIONS[g]
        rows = y.shape[0] // d
        yp = jnp.dot(perm_refs[g - 1][...], yb[:, g * GROUP_W:(g + 1) * GROUP_W],
                     preferred_element_type=F32).astype(BF16)
        for r in range(d):
            out_refs[g][:, r * GROUP_W:(r + 1) * GROUP_W] = yp[r * rows:(r + 1) * rows, :]


def _ffn_mix_prompt_kernel(x_ref, rc_ref, rs1_ref, rs2_ref, g1_ref, wgu_ref, wo_ref, gm_ref, win_ref, hm_ref, qg_ref,
                           kg_ref, p1_ref, p2_ref,
                           x1_ref, q0_ref, q1_ref, q2_ref, k0_ref, k1_ref, k2_ref, v0_ref, v1_ref, v2_ref, u_ref,
                           kt0_ref, kt1_ref, kt2_ref, vt0_ref, vt1_ref, vt2_ref, act_ref):
    tm = x_ref.shape[0]
    h2 = _proj_stage(x_ref, g1_ref, wgu_ref, wo_ref, gm_ref, x1_ref, act_ref)
    perms = (p1_ref, p2_ref)
    dilated = {"q": (q0_ref, q1_ref, q2_ref), "k": (k0_ref, k1_ref, k2_ref), "v": (v0_ref, v1_ref, v2_ref)}
    windows = {"k": (kt0_ref, kt1_ref, kt2_ref), "v": (vt0_ref, vt1_ref, vt2_ref)}
    for name, y in _mix_parts(h2, win_ref, rc_ref, rs1_ref, rs2_ref, hm_ref, qg_ref, kg_ref):
        if name == "u":
            u_ref[...] = y
            continue
        _store_dilated(y, perms, dilated[name])
        if name in windows:
            yt = jnp.transpose(y)
            for g, ref in enumerate(windows[name]):
                ref[...] = yt[g * GROUP_W:(g + 1) * GROUP_W, tm - ref.shape[1]:]


def _ffn_mix_sample_kernel(x_ref, rc_ref, rs1_ref, rs2_ref, g1_ref, wgu_ref, wo_ref, gm_ref, win_ref, hm_ref, qg_ref,
                           kg_ref, x1_ref, q_ref, k32_ref, v32_ref, u_ref, act_ref):
    h2 = _proj_stage(x_ref, g1_ref, wgu_ref, wo_ref, gm_ref, x1_ref, act_ref)
    outs = {"u": u_ref, "q": q_ref, "k": k32_ref, "v": v32_ref}
    for name, y in _mix_parts(h2, win_ref, rc_ref, rs1_ref, rs2_ref, hm_ref, qg_ref, kg_ref):
        outs[name][...] = y


def _ffn_mix_weights(w):
    return [w["ffn1_norm"], w["ffn1_wgu"], w["ffn1_wo"], w["mix_norm"], w["w_in_a"], w["head_mean"],
            w["q_gain"], w["k_gain"]]


def _ffn_mix_sample(x, rope, w):
    n, d = x.shape
    tm = TOKEN_TILE
    assert n % tm == 0 and rope[0].shape[0] == n
    cc = w["conv_w"].shape[1]
    tok = lambda width: pl.BlockSpec((tm, width), lambda i: (i, 0))
    weights = _ffn_mix_weights(w)
    widths = [d, ATTN_W, ATTN_W, ATTN_W, cc]
    return pl.pallas_call(
        _ffn_mix_sample_kernel, grid=(n // tm,),
        in_specs=[tok(d)] + [tok(LANES)] * 3 + [_resident(a.shape) for a in weights],
        out_specs=[tok(wd) for wd in widths], out_shape=[jax.ShapeDtypeStruct((n, wd), F32) for wd in widths],
        scratch_shapes=[pltpu.VMEM((tm, w["ffn1_wo"].shape[0]), BF16)],
        compiler_params=pltpu.CompilerParams(dimension_semantics=("arbitrary",), vmem_limit_bytes=VMEM_LIMIT),
        name="ffn_mix_sample",
    )(x, *rope, *weights)


def _ffn_mix_prompt(x, rope, w, seq_len):
    n, d = x.shape
    tm = TOKEN_TILE
    t = seq_len
    assert n % t == 0 and t % tm == 0 and rope[0].shape[0] == t
    assert all(wd % tm == 0 or tm % wd == 0 for wd in WINDOWS)
    tps = t // tm
    nt = n // tm
    nseq = n // t
    cc = w["conv_w"].shape[1]
    weights = _ffn_mix_weights(w) + [w["perm1"], w["perm2"]]
    tok = lambda rows, width: pl.BlockSpec((rows, width), lambda i: (i, 0))
    rope_spec = pl.BlockSpec((tm, LANES), lambda i: (i % tps, 0))
    in_specs = [tok(tm, d)] + [rope_spec] * 3 + [_resident(a.shape) for a in weights]
    out_shape = [jax.ShapeDtypeStruct((n, d), F32)]
    out_specs = [tok(tm, d)]
    for _ in range(3):
        for dl in DILATIONS:
            out_shape.append(jax.ShapeDtypeStruct((n // dl, dl * GROUP_W), BF16))
            out_specs.append(tok(tm // dl, dl * GROUP_W))
    out_shape += [jax.ShapeDtypeStruct((n, cc), F32)]
    out_specs += [tok(tm, cc)]
    for _ in range(2):
        for wd in WINDOWS:
            first = (t - wd) // tm if wd >= tm else tps - 1
            out_shape.append(jax.ShapeDtypeStruct((nseq, GROUP_W, wd), F32))
            out_specs.append(pl.BlockSpec(
                (None, GROUP_W, min(wd, tm)),
                lambda i, first=first: (i // tps, 0, jnp.maximum(i % tps - first, 0))))
    return pl.pallas_call(
        _ffn_mix_prompt_kernel, grid=(nt,), in_specs=in_specs, out_specs=out_specs, out_shape=out_shape,
        scratch_shapes=[pltpu.VMEM((tm, w["ffn1_wo"].shape[0]), BF16)],
        compiler_params=pltpu.CompilerParams(dimension_semantics=("arbitrary",), vmem_limit_bytes=VMEM_LIMIT),
        name="ffn_mix_prompt",
    )(x, *rope, *weights)


def _attn_unit(q, kp, kc, vp, vc, first):
    row = lax.broadcasted_iota(jnp.int32, (BLK, BLK), 0)
    col = lax.broadcasted_iota(jnp.int32, (BLK, BLK), 1)
    bias_p = jnp.where(col >= row, 0.0, NEG) + jnp.where(first, NEG, 0.0)
    bias_c = jnp.where(col <= row, 0.0, NEG)
    head = lax.broadcasted_iota(jnp.int32, (1, GROUP_W), 1) // HEAD_DIM
    contract_last = (((1,), (1,)), ((), ()))
    out = jnp.zeros((BLK, GROUP_W), F32)
    lse = jnp.zeros((BLK, GROUP_W), F32)
    zero = jnp.zeros((), BF16)
    for h in range(HEADS_PER_GROUP):
        hm = head == h
        qh = jnp.where(hm, q, zero)
        sp = lax.dot_general(qh, kp, contract_last, preferred_element_type=F32) * ATTN_SCALE + bias_p
        sc = lax.dot_general(qh, kc, contract_last, preferred_element_type=F32) * ATTN_SCALE + bias_c
        m = jnp.maximum(jnp.max(sp, axis=1, keepdims=True), jnp.max(sc, axis=1, keepdims=True))
        pp = jnp.exp(sp - m)
        pc = jnp.exp(sc - m)
        l = jnp.sum(pp, axis=1, keepdims=True) + jnp.sum(pc, axis=1, keepdims=True)
        o = jnp.dot(pp.astype(BF16), jnp.where(hm, vp, zero), preferred_element_type=F32) \
            + jnp.dot(pc.astype(BF16), jnp.where(hm, vc, zero), preferred_element_type=F32)
        out = out + o / l
        lse = lse + jnp.where(hm, m + jnp.log(l), 0.0)
    return out, lse


def _attn_prompt_body(ins, outs, t, u):
    for g in range(N_GROUPS):
        q_ref, kc_ref, kp_ref, vc_ref, vp_ref = ins[5 * g:5 * g + 5]
        d = DILATIONS[g]
        first = (t * (UNITS // d) + u // d) == 0
        o, lse = _attn_unit(q_ref[...], kp_ref[...], kc_ref[...], vp_ref[...], vc_ref[...], first)
        outs[2 * g][...] = o.astype(BF16)
        outs[2 * g + 1][...] = lse


def _attn_prompt_kernel(*refs):
    _attn_prompt_body(refs[:5 * N_GROUPS], refs[5 * N_GROUPS:], pl.program_id(1), pl.program_id(2))


def _attn_prompt_specs(qd, kd, vd, batch, seq_len, index):
    assert seq_len % ATTN_TILE == 0
    args, in_specs, out_shape, out_specs = [], [], [], []
    for g in range(N_GROUPS):
        d = DILATIONS[g]
        nb = UNITS // d

        def cur(*i, d=d, nb=nb):
            b, t, u = index(*i)
            return (b, t * nb + u // d, u % d)

        def prev(*i, d=d, nb=nb):
            b, t, u = index(*i)
            return (b, jnp.maximum(t * nb + u // d - 1, 0), u % d)

        view = lambda a, d=d: a.reshape(batch, seq_len // d, d * GROUP_W)
        blk = lambda im: pl.BlockSpec((None, BLK, GROUP_W), im)
        args += [view(qd[g]), view(kd[g]), view(kd[g]), view(vd[g]), view(vd[g])]
        in_specs += [blk(cur), blk(cur), blk(prev), blk(cur), blk(prev)]
        out_shape += [jax.ShapeDtypeStruct((batch, seq_len // d, d * GROUP_W), BF16),
                      jax.ShapeDtypeStruct((batch, seq_len // d, d * GROUP_W), F32)]
        out_specs += [blk(cur), blk(cur)]
    return args, in_specs, out_shape, out_specs


def _attn_prompt(qd, kd, vd, batch, seq_len):
    args, in_specs, out_shape, out_specs = _attn_prompt_specs(qd, kd, vd, batch, seq_len, lambda b, t, u: (b, t, u))
    res = pl.pallas_call(
        _attn_prompt_kernel, grid=(batch, seq_len // ATTN_TILE, UNITS), in_specs=in_specs, out_specs=out_specs,
        out_shape=out_shape,
        compiler_params=pltpu.CompilerParams(dimension_semantics=("arbitrary",) * 3),
        name="attn_prompt",
    )(*args)
    return [r.reshape(-1, r.shape[2]) for r in res]


def _conv_sample_kernel(state_ref, u_ref, cw_ref, cb_ref, lng_ref, lnb_ref, a_ref, new_state_ref):
    hist = state_ref.shape[0]
    s_len = u_ref.shape[0]
    row = lambda t: state_ref[t] if t < hist else u_ref[t - hist]
    for s in range(s_len):
        acc = None
        for j in range(CONV_WIDTH):
            term = row(s + j) * cw_ref[j:j + 1, :]
            acc = term if acc is None else acc + term
        a_ref[s] = _layernorm_silu(acc + cb_ref[...], lng_ref[...], lnb_ref[...])
    for t in range(hist):
        new_state_ref[t] = row(t + s_len)


def _conv_sample(state_t, u_t, w):
    hist, db, cc = state_t.shape
    s_len = u_t.shape[0]
    assert hist == CONV_WIDTH - 1 and db % CONV_SEQS == 0
    conv_w = [w["conv_w"], w["conv_b"], w["conv_ln_g"], w["conv_ln_b"]]
    slab = lambda rows: pl.BlockSpec((rows, CONV_SEQS, cc), lambda i: (0, i, 0))
    return pl.pallas_call(
        _conv_sample_kernel, grid=(db // CONV_SEQS,),
        in_specs=[slab(hist), slab(s_len)] + [_resident(a.shape) for a in conv_w],
        out_specs=[slab(s_len), slab(hist)],
        out_shape=[jax.ShapeDtypeStruct((s_len, db, cc), F32), jax.ShapeDtypeStruct((hist, db, cc), F32)],
        compiler_params=pltpu.CompilerParams(dimension_semantics=("arbitrary",)),
        name="conv_sample",
    )(state_t, u_t, *conv_w)


def _sample_bias(width, d, s_len):
    s = lax.broadcasted_iota(jnp.int32, (SAMPLE_ROWS, width), 0)
    t = lax.broadcasted_iota(jnp.int32, (SAMPLE_ROWS, width), 1)
    same_residue = jnp.bitwise_and(s - t, d - 1) == 0
    live = s < s_len
    old = jnp.where(live & same_residue & (t >= s), 0.0, NEG)
    sn = s[:, :LANES]
    tn = t[:, :LANES]
    new = jnp.where(live[:, :LANES] & same_residue[:, :LANES] & (tn <= sn), 0.0, NEG)
    return old, new


def _attn_sample_kernel(*refs, s_len):
    _attn_sample_body(*refs, s_len=s_len)


def _attn_sample_body(q_ref, kn_ref, vn_ref, kc0_ref, vc0_ref, kc1_ref, vc1_ref, kc2_ref, vc2_ref,
                      b_ref, nk0_ref, nv0_ref, nk1_ref, nv1_ref, nk2_ref, nv2_ref, *, s_len):
    kc_refs = (kc0_ref, kc1_ref, kc2_ref)
    vc_refs = (vc0_ref, vc1_ref, vc2_ref)
    nk_refs = (nk0_ref, nk1_ref, nk2_ref)
    nv_refs = (nv0_ref, nv1_ref, nv2_ref)
    q = q_ref[...]
    pad = jnp.zeros((LANES - SAMPLE_ROWS, ATTN_W), F32)
    knt = jnp.transpose(jnp.concatenate([kn_ref[...], pad], axis=0))
    vnt = jnp.transpose(jnp.concatenate([vn_ref[...], pad], axis=0))
    biases = [_sample_bias(WINDOWS[g], DILATIONS[g], s_len) for g in range(N_GROUPS)]
    lane = lax.broadcasted_iota(jnp.int32, (1, LANES), 1)
    keep = lane < LANES - s_len
    nt = (((1,), (1,)), ((), ()))

    def shifted(old, new_cols, out_ref, h):
        w = old.shape[1]
        cols = [old[:, j * LANES:(j + 1) * LANES] for j in range(w // LANES)] + [new_cols]
        rolled = [pltpu.roll(c, LANES - s_len, 1) for c in cols]
        for j in range(w // LANES):
            out_ref[h, :, j * LANES:(j + 1) * LANES] = jnp.where(keep, rolled[j], rolled[j + 1])

    heads = range(HEADS_PER_GROUP)
    lanes_of = lambda g, h: slice((g * HEADS_PER_GROUP + h) * HEAD_DIM, (g * HEADS_PER_GROUP + h + 1) * HEAD_DIM)
    s_old = [[None] * N_GROUPS for _ in heads]
    s_new = [[None] * N_GROUPS for _ in heads]
    for h in heads:
        for g in range(N_GROUPS):
            qh = q[:, lanes_of(g, h)].astype(BF16)
            k_old = kc_refs[g][h]
            k_new = knt[lanes_of(g, h), :]
            s_old[h][g] = jnp.dot(qh, k_old.astype(BF16), preferred_element_type=F32) * ATTN_SCALE + biases[g][0]
            s_new[h][g] = jnp.dot(qh, k_new.astype(BF16), preferred_element_type=F32) * ATTN_SCALE + biases[g][1]
            shifted(k_old, k_new, nk_refs[g], h)
    p_old = [[None] * N_GROUPS for _ in heads]
    p_new = [[None] * N_GROUPS for _ in heads]
    dens = []
    for h in heads:
        m = None
        for a in s_old[h] + s_new[h]:
            am = jnp.max(a, axis=1, keepdims=True)
            m = am if m is None else jnp.maximum(m, am)
        den = jnp.zeros((SAMPLE_ROWS, 1), F32)
        for g in range(N_GROUPS):
            p_old[h][g] = jnp.exp(s_old[h][g] - m)
            p_new[h][g] = jnp.exp(s_new[h][g] - m)
            den = den + jnp.sum(p_old[h][g], axis=1, keepdims=True) + jnp.sum(p_new[h][g], axis=1, keepdims=True)
        dens.append(den)
    for h in heads:
        num = jnp.zeros((SAMPLE_ROWS, HEAD_DIM), F32)
        for g in range(N_GROUPS):
            v_old = vc_refs[g][h]
            v_new = vnt[lanes_of(g, h), :]
            num = num + lax.dot_general(p_old[h][g].astype(BF16), v_old.astype(BF16), nt,
                                        preferred_element_type=F32) \
                + lax.dot_general(p_new[h][g].astype(BF16), v_new.astype(BF16), nt, preferred_element_type=F32)
            shifted(v_old, v_new, nv_refs[g], h)
        b_ref[:, h * HEAD_DIM:(h + 1) * HEAD_DIM] = num / dens[h]


def _attn_sample_specs(q8, kn8, vn8, caches_k, caches_v, s_len):
    db = q8.shape[0]
    assert q8.shape[1] == SAMPLE_ROWS and s_len <= min(SAMPLE_ROWS, DILATIONS[1])
    tok = pl.BlockSpec((None, SAMPLE_ROWS, ATTN_W), lambda b: (b, 0, 0))
    args, in_specs, out_shape, out_specs = [q8, kn8, vn8], [tok] * 3, [], []
    out_shape.append(jax.ShapeDtypeStruct((db, SAMPLE_ROWS, GROUP_W), F32))
    out_specs.append(pl.BlockSpec((None, SAMPLE_ROWS, GROUP_W), lambda b: (b, 0, 0)))
    for g in range(N_GROUPS):
        shape = (db, HEADS_PER_GROUP, HEAD_DIM, WINDOWS[g])
        spec = pl.BlockSpec((None,) + shape[1:], lambda b: (b, 0, 0, 0))
        for c in (caches_k[g], caches_v[g]):
            assert c.shape == shape
            args.append(c)
            in_specs.append(spec)
            out_shape.append(jax.ShapeDtypeStruct(shape, F32))
            out_specs.append(spec)
    return args, in_specs, out_shape, out_specs


def _attn_sample(q8, kn8, vn8, caches_k, caches_v, s_len):
    db = q8.shape[0]
    args, in_specs, out_shape, out_specs = _attn_sample_specs(q8, kn8, vn8, caches_k, caches_v, s_len)
    return pl.pallas_call(
        functools.partial(_attn_sample_kernel, s_len=s_len), grid=(db,), in_specs=in_specs, out_specs=out_specs,
        out_shape=out_shape,
        compiler_params=pltpu.CompilerParams(dimension_semantics=("arbitrary",), vmem_limit_bytes=VMEM_LIMIT),
        name="attn_sample",
    )(*args)


def _attn_both_kernel(*refs, s_len, tiles):
    n_pi, n_si = 5 * N_GROUPS, 3 + 2 * N_GROUPS
    n_po = 2 * N_GROUPS
    p_in, s_in = refs[:n_pi], refs[n_pi:n_pi + n_si]
    p_out, s_out = refs[n_pi + n_si:n_pi + n_si + n_po], refs[n_pi + n_si + n_po:]
    i = pl.program_id(0)
    _attn_prompt_body(p_in, p_out, (i // UNITS) % tiles, i % UNITS)
    _attn_sample_body(*s_in, *s_out, s_len=s_len)


def _attn_both(qd, kd, vd, batch, seq_len, q8, kn8, vn8, caches_k, caches_v, s_len):
    tiles = seq_len // ATTN_TILE
    n = batch * tiles * UNITS
    assert n == q8.shape[0]
    index = lambda i: (i // (tiles * UNITS), (i // UNITS) % tiles, i % UNITS)
    p_args, p_in, p_shape, p_out = _attn_prompt_specs(qd, kd, vd, batch, seq_len, index)
    s_args, s_in, s_shape, s_out = _attn_sample_specs(q8, kn8, vn8, caches_k, caches_v, s_len)
    res = pl.pallas_call(
        functools.partial(_attn_both_kernel, s_len=s_len, tiles=tiles), grid=(n,), in_specs=p_in + s_in,
        out_specs=p_out + s_out, out_shape=p_shape + s_shape,
        compiler_params=pltpu.CompilerParams(dimension_semantics=("arbitrary",), vmem_limit_bytes=VMEM_LIMIT),
        name="attn_both",
    )(*p_args, *s_args)
    return [r.reshape(-1, r.shape[2]) for r in res[:len(p_shape)]], res[len(p_shape):]


def _merge_stage(x1_ref, a_pre, battn, gm_ref, wg_ref, bg_ref, wco_ref, wao_ref, wout_ref):
    x1 = x1_ref[...]
    d = x1.shape[1]
    h = _rms(x1, gm_ref[...]).astype(BF16)
    gates = jax.nn.sigmoid(jnp.dot(h, wg_ref[...], preferred_element_type=F32) + bg_ref[...])
    a = jnp.dot(a_pre.astype(BF16), wco_ref[...], preferred_element_type=F32)
    b = jnp.dot(battn.astype(BF16), wao_ref[...], preferred_element_type=F32)
    mix = (gates[:, :d] * a + gates[:, d:] * b).astype(BF16)
    return x1 + jnp.dot(mix, wout_ref[...], preferred_element_type=F32)


def _ffn_stage(y, g2_ref, wgu_ref, wo_ref, act_ref):
    return y + 0.5 * _swiglu(_rms(y, g2_ref[...]).astype(BF16), wgu_ref, wo_ref, act_ref)


def _undilate(blk, pt_ref, d):
    if d == 1:
        return blk.astype(F32)
    rows = jnp.concatenate([blk[:, r * GROUP_W:(r + 1) * GROUP_W] for r in range(d)], axis=0)
    pt = pt_ref[...]
    if rows.dtype == BF16:
        return jnp.dot(pt, rows, preferred_element_type=F32)
    hi = rows.astype(BF16)
    rest = rows - hi.astype(F32)
    mid = rest.astype(BF16)
    lo = (rest - mid.astype(F32)).astype(BF16)
    return jnp.dot(pt, hi, preferred_element_type=F32) + jnp.dot(pt, mid, preferred_element_type=F32) \
        + jnp.dot(pt, lo, preferred_element_type=F32)


def _merge_ffn_prompt_kernel(x1_ref, u_ref, uprev_ref, o0_ref, o1_ref, o2_ref, l0_ref, l1_ref, l2_ref, cw_ref, cb_ref,
                             lng_ref, lnb_ref, pt1_ref, pt2_ref, gm_ref, wg_ref, bg_ref, wco_ref, wao_ref, wout_ref,
                             g2_ref, wgu_ref, wo_ref, out_ref, act_ref, *, tiles_per_seq):
    tile = jnp.zeros(uprev_ref.shape, jnp.int32) + pl.program_id(0) % tiles_per_seq
    ext = jnp.concatenate([jnp.where(tile == 0, 0.0, uprev_ref[...]), u_ref[...]], axis=0)
    a_pre = _causal_conv(ext, cw_ref, cb_ref, lng_ref, lnb_ref)
    pts = (None, pt1_ref, pt2_ref)
    lses = [_undilate(r[...], pts[g], DILATIONS[g]) for g, r in enumerate((l0_ref, l1_ref, l2_ref))]
    outs = [_undilate(r[...], pts[g], DILATIONS[g]) for g, r in enumerate((o0_ref, o1_ref, o2_ref))]
    mx = jnp.maximum(jnp.maximum(lses[0], lses[1]), lses[2])
    num = None
    den = None
    for o, l in zip(outs, lses):
        wgt = jnp.exp(l - mx)
        num = wgt * o if num is None else num + wgt * o
        den = wgt if den is None else den + wgt
    y = _merge_stage(x1_ref, a_pre, num / den, gm_ref, wg_ref, bg_ref, wco_ref, wao_ref, wout_ref)
    out_ref[...] = _ffn_stage(y, g2_ref, wgu_ref, wo_ref, act_ref)


def _merge_ffn_sample_kernel(x1_ref, a_ref, b_ref, gm_ref, wg_ref, bg_ref, wco_ref, wao_ref, wout_ref, g2_ref,
                             wgu_ref, wo_ref, out_ref, act_ref):
    y = _merge_stage(x1_ref, a_ref[...], b_ref[...], gm_ref, wg_ref, bg_ref, wco_ref, wao_ref, wout_ref)
    out_ref[...] = _ffn_stage(y, g2_ref, wgu_ref, wo_ref, act_ref)


def _merge_weights(w):
    return [w["mix_norm"], w["w_in_g"], w["b_gate"], w["w_conv_out"], w["w_attn_out"], w["w_out"], w["ffn2_norm"],
            w["ffn2_wgu"], w["ffn2_wo"]]


def _merge_ffn_sample(x1, a_pre, b, w):
    n, d = x1.shape
    tm = TOKEN_TILE
    assert n % tm == 0
    tok = lambda width: pl.BlockSpec((tm, width), lambda i: (i, 0))
    weights = _merge_weights(w)
    return pl.pallas_call(
        _merge_ffn_sample_kernel, grid=(n // tm,),
        in_specs=[tok(d), tok(a_pre.shape[1]), tok(GROUP_W)] + [_resident(a.shape) for a in weights],
        out_specs=tok(d), out_shape=jax.ShapeDtypeStruct((n, d), F32),
        scratch_shapes=[pltpu.VMEM((tm, w["ffn2_wo"].shape[0]), BF16)],
        compiler_params=pltpu.CompilerParams(dimension_semantics=("arbitrary",), vmem_limit_bytes=VMEM_LIMIT),
        name="merge_ffn_sample",
    )(x1, a_pre, b, *weights)


def _merge_ffn_prompt(x1, u, outs, lses, w, seq_len):
    n, d = x1.shape
    tm = TOKEN_TILE
    cc = u.shape[1]
    assert n % tm == 0 and seq_len % tm == 0 and tm % CONV_PAD == 0
    tok = lambda rows, width: pl.BlockSpec((rows, width), lambda i: (i, 0))
    before = pl.BlockSpec((CONV_PAD, cc), lambda i: (jnp.maximum(i * (tm // CONV_PAD) - 1, 0), 0))
    grp = [tok(tm // dl, dl * GROUP_W) for dl in DILATIONS]
    weights = [w["conv_w"], w["conv_b"], w["conv_ln_g"], w["conv_ln_b"], w["permt1"], w["permt2"]] \
        + _merge_weights(w)
    return pl.pallas_call(
        functools.partial(_merge_ffn_prompt_kernel, tiles_per_seq=seq_len // tm), grid=(n // tm,),
        in_specs=[tok(tm, d), tok(tm, cc), before] + grp + grp + [_resident(a.shape) for a in weights],
        out_specs=tok(tm, d), out_shape=jax.ShapeDtypeStruct((n, d), F32),
        scratch_shapes=[pltpu.VMEM((tm, w["ffn2_wo"].shape[0]), BF16)],
        compiler_params=pltpu.CompilerParams(dimension_semantics=("arbitrary",), vmem_limit_bytes=VMEM_LIMIT),
        name="merge_ffn_prompt",
    )(x1, u, u, *outs, *lses, *weights)


def _rope_tables(pos):
    half = ROT_DIM // 2
    inv = jnp.float32(ROPE_THETA) ** (-jnp.arange(half, dtype=F32) * (2.0 / ROT_DIM))
    ang = pos.astype(F32)[:, None] * inv[None, :]
    cos, sin = jnp.cos(ang), jnp.sin(ang)
    n = pos.shape[0]
    rest = HEAD_DIM - ROT_DIM
    zh = jnp.zeros((n, half), F32)
    c = jnp.concatenate([cos, cos, jnp.ones((n, rest), F32)], axis=1)
    s1 = jnp.concatenate([zh, sin, jnp.zeros((n, rest), F32)], axis=1)
    s2 = jnp.concatenate([-sin, zh, jnp.zeros((n, rest), F32)], axis=1)
    rep = LANES // HEAD_DIM
    return tuple(jnp.tile(t, (1, rep)) for t in (c, s1, s2))


def _prepare_weights(l, ffn1_norm, ffn1_w_in, ffn1_w_out, mix_norm, w_in, b_gate, q_norm, k_norm, conv_w, conv_b,
                     conv_ln_g, conv_ln_b, w_conv_out, w_attn_out, w_out, ffn2_norm, ffn2_w_in, ffn2_w_out):
    cc = conv_w.shape[2]
    n_a = 2 * cc + 3 * ATTN_W
    lane = jnp.arange(GROUP_W)
    same_head = (lane[:, None] // HEAD_DIM) == (lane[None, :] // HEAD_DIM)
    row = lambda a: a.reshape(1, -1).astype(F32)
    tok = jnp.arange(TOKEN_TILE)

    def perm(d):
        src = (tok % (TOKEN_TILE // d)) * d + tok // (TOKEN_TILE // d)
        return (src[:, None] == tok[None, :]).astype(BF16)

    return {
        "perm1": perm(DILATIONS[1]), "perm2": perm(DILATIONS[2]),
        "permt1": perm(DILATIONS[1]).T, "permt2": perm(DILATIONS[2]).T,
        "ffn1_norm": row(ffn1_norm[l]), "ffn1_wgu": ffn1_w_in[l].astype(BF16), "ffn1_wo": ffn1_w_out[l].astype(BF16),
        "mix_norm": row(mix_norm[l]), "w_in_a": w_in[l][:, :n_a].astype(BF16), "w_in_g": w_in[l][:, n_a:].astype(BF16),
        "b_gate": row(b_gate[l]),
        "head_mean": jnp.where(same_head, 1.0 / HEAD_DIM, 0.0).astype(BF16),
        "q_gain": row(q_norm[l]), "k_gain": row(k_norm[l]),
        "conv_w": jnp.concatenate([conv_w[l], jnp.zeros((CONV_PAD - CONV_WIDTH, cc), F32)], axis=0),
        "conv_b": row(conv_b[l]), "conv_ln_g": row(conv_ln_g[l]), "conv_ln_b": row(conv_ln_b[l]),
        "w_conv_out": w_conv_out[l].astype(BF16), "w_attn_out": w_attn_out[l].astype(BF16),
        "w_out": w_out[l].astype(BF16),
        "ffn2_norm": row(ffn2_norm[l]), "ffn2_wgu": ffn2_w_in[l].astype(BF16), "ffn2_wo": ffn2_w_out[l].astype(BF16),
    }


def kernel(x_prompt, x_sample, cache_k_w128, cache_v_w128, cache_k_w512, cache_v_w512, cache_k_w2048, cache_v_w2048, state_conv, ffn1_norm, ffn1_w_in, ffn1_w_out, mix_norm, w_in, b_gate, q_norm, k_norm, conv_w, conv_b, conv_ln_g, conv_ln_b, w_conv_out, w_attn_out, w_out, ffn2_norm, ffn2_w_in, ffn2_w_out):
    batch, t, d = x_prompt.shape
    db, s_len, _ = x_sample.shape
    depth = w_in.shape[0]
    cc = conv_w.shape[2]
    hd = (HEADS_PER_GROUP, HEAD_DIM)
    cache_k = (cache_k_w128, cache_k_w512, cache_k_w2048)
    cache_v = (cache_v_w128, cache_v_w512, cache_v_w2048)
    for g, wdw in enumerate(WINDOWS):
        assert cache_k[g].shape == (depth, db, wdw, *hd), "sample caches must hold a full window"
        assert t >= wdw
    rope_p = _rope_tables(jnp.arange(t))
    rope_s = _rope_tables(PAST_LEN + jnp.arange(db * s_len) // db)
    to_time_minor = lambda c: jnp.transpose(c, (0, 2, 3, 1))
    from_time_minor = lambda c: jnp.transpose(c, (0, 3, 1, 2))

    xp = x_prompt.reshape(batch * t, d)
    xs = jnp.transpose(x_sample, (1, 0, 2)).reshape(s_len * db, d)
    outs_p = [[] for _ in range(2 * N_GROUPS)]
    outs_s = [[] for _ in range(2 * N_GROUPS)]
    new_conv_p, new_conv_s = [], []
    for l in range(depth):
        w = _prepare_weights(l, ffn1_norm, ffn1_w_in, ffn1_w_out, mix_norm, w_in, b_gate, q_norm, k_norm, conv_w,
                             conv_b, conv_ln_g, conv_ln_b, w_conv_out, w_attn_out, w_out, ffn2_norm, ffn2_w_in,
                             ffn2_w_out)
        x1p, *rest = _ffn_mix_prompt(xp, rope_p, w, t)
        qkv, u_p, kv_t = rest[:9], rest[9], rest[10:]
        x1s, qs, k32s, v32s, us = _ffn_mix_sample(xs, rope_s, w)
        a_s, state_new = _conv_sample(jnp.transpose(state_conv[l], (1, 0, 2)), us.reshape(s_len, db, cc), w)
        rows8 = lambda a: jnp.pad(jnp.transpose(a.reshape(s_len, db, ATTN_W), (1, 0, 2)),
                                  ((0, 0), (0, SAMPLE_ROWS - s_len), (0, 0)))
        ck = [to_time_minor(cache_k[g][l]) for g in range(N_GROUPS)]
        cv = [to_time_minor(cache_v[g][l]) for g in range(N_GROUPS)]
        if batch * (t // ATTN_TILE) * UNITS == db:
            attn_p, (b_s, *new_caches) = _attn_both(qkv[0:3], qkv[3:6], qkv[6:9], batch, t, rows8(qs), rows8(k32s),
                                                    rows8(v32s), ck, cv, s_len)
        else:
            attn_p = _attn_prompt(qkv[0:3], qkv[3:6], qkv[6:9], batch, t)
            b_s, *new_caches = _attn_sample(rows8(qs), rows8(k32s), rows8(v32s), ck, cv, s_len)
        xp = _merge_ffn_prompt(x1p, u_p, attn_p[0::2], attn_p[1::2], w, t)
        b_s = jnp.transpose(b_s[:, :s_len], (1, 0, 2)).reshape(s_len * db, GROUP_W)
        xs = _merge_ffn_sample(x1s, a_s.reshape(s_len * db, cc), b_s, w)
        new_conv_p.append(u_p.reshape(batch, t, cc)[:, t - (CONV_WIDTH - 1):, :])
        new_conv_s.append(jnp.transpose(state_new, (1, 0, 2)))
        for i, (g, c) in enumerate([(g, c) for c in range(2) for g in range(N_GROUPS)]):
            outs_p[2 * g + c].append(from_time_minor(kv_t[i].reshape(batch, *hd, WINDOWS[g])))
        for i, nc in enumerate(new_caches):
            outs_s[i].append(from_time_minor(nc))
    y_s = jnp.transpose(xs.reshape(s_len, db, d), (1, 0, 2))
    return (xp.reshape(batch, t, d), y_s, *[jnp.stack(o) for o in outs_p], jnp.stack(new_conv_p),
            *[jnp.stack(o) for o in outs_s], jnp.stack(new_conv_s))
```

```python
import functools

import jax
import jax.numpy as jnp
from jax import lax
from jax.experimental import pallas as pl
from jax.experimental.pallas import tpu as pltpu

F32 = jnp.float32
BF16 = jnp.bfloat16

HEAD_DIM = 64
WINDOWS = (128, 512, 2048)
DILATIONS = (1, 4, 16)
N_GROUPS = 3
HEADS_PER_GROUP = 4
GROUP_W = HEADS_PER_GROUP * HEAD_DIM
ATTN_W = N_GROUPS * GROUP_W
BLK = WINDOWS[0] // DILATIONS[0]
assert all(w // d == BLK for w, d in zip(WINDOWS, DILATIONS))
ATTN_TILE = max(WINDOWS)
UNITS = ATTN_TILE // BLK
ATTN_SCALE = HEAD_DIM ** -0.5
ROT_DIM = HEAD_DIM // 4
ROPE_THETA = 500000.0
CONV_WIDTH = 31
CONV_PAD = 32
NORM_EPS = 1e-6
PAST_LEN = 2048
NEG = -1e30
LANES = 128
SUBLANES = 8
FF_CHUNK = 256
TOKEN_TILE = 256
SAMPLE_ROWS = 8
CONV_SEQS = 16
VMEM_LIMIT = 56 * 1024 * 1024


def _resident(shape):
    nd = len(shape)
    return pl.BlockSpec(shape, lambda *_: (0,) * nd, pipeline_mode=pl.Buffered(1))


def _rms(x, g):
    ms = jnp.mean(x * x, axis=-1, keepdims=True)
    return x * lax.rsqrt(ms + NORM_EPS) * g


def _swiglu(h, wgu_ref, wo_ref, act_ref):
    f = wo_ref.shape[0]
    assert f % FF_CHUNK == 0
    for c in range(f // FF_CHUNK):
        g = jnp.dot(h, wgu_ref[:, c * FF_CHUNK:(c + 1) * FF_CHUNK], preferred_element_type=F32)
        u = jnp.dot(h, wgu_ref[:, f + c * FF_CHUNK:f + (c + 1) * FF_CHUNK], preferred_element_type=F32)
        act_ref[:, c * FF_CHUNK:(c + 1) * FF_CHUNK] = (g * jax.nn.sigmoid(g) * u).astype(BF16)
    return jnp.dot(act_ref[...], wo_ref[...], preferred_element_type=F32)


def _head_norm(y, hm_ref, gain):
    parts = []
    for g in range(N_GROUPS):
        yg = y[:, g * GROUP_W:(g + 1) * GROUP_W]
        ms = jnp.dot((yg * yg).astype(BF16), hm_ref[...], preferred_element_type=F32)
        parts.append(yg * lax.rsqrt(ms + NORM_EPS))
    return jnp.concatenate(parts, axis=1) * gain


def _rope(y, c, s1, s2):
    half = ROT_DIM // 2
    parts = []
    for i in range(y.shape[1] // LANES):
        yc = y[:, i * LANES:(i + 1) * LANES]
        parts.append(yc * c + pltpu.roll(yc, half, 1) * s1 + pltpu.roll(yc, LANES - half, 1) * s2)
    return jnp.concatenate(parts, axis=1)


def _layernorm_silu(y, g, b):
    mu = jnp.mean(y, axis=-1, keepdims=True)
    yc = y - mu
    var = jnp.mean(yc * yc, axis=-1, keepdims=True)
    yn = yc * lax.rsqrt(var + NORM_EPS) * g + b
    return yn * jax.nn.sigmoid(yn)


def _causal_conv(ext, cw_ref, cb_ref, lng_ref, lnb_ref):
    tm = ext.shape[0] - CONV_PAD
    first = CONV_PAD - (CONV_WIDTH - 1)
    rows = 64
    outs = []
    for r0 in range(0, tm, rows):
        acc = None
        base = ext[r0:r0 + rows + CONV_PAD, :]
        for r in range(SUBLANES):
            win = base if r == 0 else pltpu.roll(base, base.shape[0] - r, 0)
            for j in range(CONV_WIDTH):
                if (first + j) % SUBLANES != r:
                    continue
                o = first + j - r
                term = win[o:o + rows, :] * cw_ref[j:j + 1, :]
                acc = term if acc is None else acc + term
        outs.append(_layernorm_silu(acc + cb_ref[...], lng_ref[...], lnb_ref[...]))
    return jnp.concatenate(outs, axis=0)


def _proj_stage(x_ref, g1_ref, wgu_ref, wo_ref, gm_ref, x1_ref, act_ref):
    x = x_ref[...]
    h = _rms(x, g1_ref[...]).astype(BF16)
    x1 = x + 0.5 * _swiglu(h, wgu_ref, wo_ref, act_ref)
    x1_ref[...] = x1
    return _rms(x1, gm_ref[...]).astype(BF16)


def _mix_parts(h2, win_ref, rc_ref, rs1_ref, rs2_ref, hm_ref, qg_ref, kg_ref):
    cc = (win_ref.shape[1] - 3 * ATTN_W) // 2
    proj = lambda lo, hi: jnp.dot(h2, win_ref[:, lo:hi], preferred_element_type=F32)
    zu = proj(0, 2 * cc)
    yield "u", zu[:, :cc] * jax.nn.sigmoid(zu[:, cc:])
    c, s1, s2 = rc_ref[...], rs1_ref[...], rs2_ref[...]
    yield "q", _rope(_head_norm(proj(2 * cc, 2 * cc + ATTN_W), hm_ref, qg_ref[...]), c, s1, s2)
    yield "k", _rope(_head_norm(proj(2 * cc + ATTN_W, 2 * cc + 2 * ATTN_W), hm_ref, kg_ref[...]), c, s1, s2)
    yield "v", proj(2 * cc + 2 * ATTN_W, 2 * cc + 3 * ATTN_W)


def _store_dilated(y, perm_refs, out_refs):
    yb = y.astype(BF16)
    out_refs[0][...] = yb[:, :GROUP_W]
    for g in range(1, N_GROUPS):
        d = DILATIONS[g]
        rows = y.shape[0] // d
        yp = jnp.dot(perm_refs[g - 1][...], yb[:, g * GROUP_W:(g + 1) * GROUP_W],
                     preferred_element_type=F32).astype(BF16)
        for r in range(d):
            out_refs[g][:, r * GROUP_W:(r + 1) * GROUP_W] = yp[r * rows:(r + 1) * rows, :]


def _ffn_mix_prompt_kernel(x_ref, rc_ref, rs1_ref, rs2_ref, g1_ref, wgu_ref, wo_ref, gm_ref, win_ref, hm_ref, qg_ref,
                           kg_ref, p1_ref, p2_ref, cw_ref, cb_ref, lng_ref, lnb_ref, wco_ref,
                           x1_ref, q0_ref, q1_ref, q2_ref, k0_ref, k1_ref, k2_ref, v0_ref, v1_ref, v2_ref, a_ref,
                           utail_ref, kt0_ref, kt1_ref, kt2_ref, vt0_ref, vt1_ref, vt2_ref, act_ref, carry_ref, *,
                           tiles_per_seq):
    tm = x_ref.shape[0]
    i = pl.program_id(0)

    @pl.when(i == 0)
    def _():
        carry_ref[...] = jnp.zeros(carry_ref.shape, F32)

    h2 = _proj_stage(x_ref, g1_ref, wgu_ref, wo_ref, gm_ref, x1_ref, act_ref)
    perms = (p1_ref, p2_ref)
    dilated = {"q": (q0_ref, q1_ref, q2_ref), "k": (k0_ref, k1_ref, k2_ref), "v": (v0_ref, v1_ref, v2_ref)}
    windows = {"k": (kt0_ref, kt1_ref, kt2_ref), "v": (vt0_ref, vt1_ref, vt2_ref)}
    parts = dict(_mix_parts(h2, win_ref, rc_ref, rs1_ref, rs2_ref, hm_ref, qg_ref, kg_ref))
    u = parts.pop("u")
    for name, y in parts.items():
        _store_dilated(y, perms, dilated[name])
        if name in windows:
            yt = jnp.transpose(y)
            for g, ref in enumerate(windows[name]):
                ref[...] = yt[g * GROUP_W:(g + 1) * GROUP_W, tm - ref.shape[1]:]
    tile = jnp.zeros(carry_ref.shape, jnp.int32) + i % tiles_per_seq
    ext = jnp.concatenate([jnp.where(tile == 0, 0.0, carry_ref[...]), u], axis=0)
    a_pre = _causal_conv(ext, cw_ref, cb_ref, lng_ref, lnb_ref)
    a_ref[...] = jnp.dot(a_pre.astype(BF16), wco_ref[...], preferred_element_type=F32).astype(BF16)
    tail = u[tm - CONV_PAD:, :]
    utail_ref[...] = tail
    carry_ref[...] = tail


def _ffn_mix_sample_kernel(x_ref, rc_ref, rs1_ref, rs2_ref, g1_ref, wgu_ref, wo_ref, gm_ref, win_ref, hm_ref, qg_ref,
                           kg_ref, x1_ref, q_ref, k32_ref, v32_ref, u_ref, act_ref):
    h2 = _proj_stage(x_ref, g1_ref, wgu_ref, wo_ref, gm_ref, x1_ref, act_ref)
    outs = {"u": u_ref, "q": q_ref, "k": k32_ref, "v": v32_ref}
    for name, y in _mix_parts(h2, win_ref, rc_ref, rs1_ref, rs2_ref, hm_ref, qg_ref, kg_ref):
        outs[name][...] = y


def _ffn_mix_weights(w):
    return [w["ffn1_norm"], w["ffn1_wgu"], w["ffn1_wo"], w["mix_norm"], w["w_in_a"], w["head_mean"],
            w["q_gain"], w["k_gain"]]


def _ffn_mix_sample(x, rope, w):
    n, d = x.shape
    tm = TOKEN_TILE
    assert n % tm == 0 and rope[0].shape[0] == n
    cc = w["conv_w"].shape[1]
    tok = lambda width: pl.BlockSpec((tm, width), lambda i: (i, 0))
    weights = _ffn_mix_weights(w)
    widths = [d, ATTN_W, ATTN_W, ATTN_W, cc]
    return pl.pallas_call(
        _ffn_mix_sample_kernel, grid=(n // tm,),
        in_specs=[tok(d)] + [tok(LANES)] * 3 + [_resident(a.shape) for a in weights],
        out_specs=[tok(wd) for wd in widths], out_shape=[jax.ShapeDtypeStruct((n, wd), F32) for wd in widths],
        scratch_shapes=[pltpu.VMEM((tm, w["ffn1_wo"].shape[0]), BF16)],
        compiler_params=pltpu.CompilerParams(dimension_semantics=("arbitrary",), vmem_limit_bytes=VMEM_LIMIT),
        name="ffn_mix_sample",
    )(x, *rope, *weights)


def _ffn_mix_prompt(x, rope, w, seq_len):
    n, d = x.shape
    tm = TOKEN_TILE
    t = seq_len
    assert n % t == 0 and t % tm == 0 and rope[0].shape[0] == t
    assert all(wd % tm == 0 or tm % wd == 0 for wd in WINDOWS)
    tps = t // tm
    nt = n // tm
    nseq = n // t
    cc = w["conv_w"].shape[1]
    weights = _ffn_mix_weights(w) + [w["perm1"], w["perm2"], w["conv_w"], w["conv_b"], w["conv_ln_g"],
                                     w["conv_ln_b"], w["w_conv_out"]]
    tok = lambda rows, width: pl.BlockSpec((rows, width), lambda i: (i, 0))
    rope_spec = pl.BlockSpec((tm, LANES), lambda i: (i % tps, 0))
    in_specs = [tok(tm, d)] + [rope_spec] * 3 + [_resident(a.shape) for a in weights]
    out_shape = [jax.ShapeDtypeStruct((n, d), F32)]
    out_specs = [tok(tm, d)]
    for _ in range(3):
        for dl in DILATIONS:
            out_shape.append(jax.ShapeDtypeStruct((n // dl, dl * GROUP_W), BF16))
            out_specs.append(tok(tm // dl, dl * GROUP_W))
    out_shape += [jax.ShapeDtypeStruct((n, d), BF16), jax.ShapeDtypeStruct((nseq, CONV_PAD, cc), F32)]
    out_specs += [tok(tm, d), pl.BlockSpec((None, CONV_PAD, cc), lambda i: (i // tps, 0, 0))]
    for _ in range(2):
        for wd in WINDOWS:
            first = (t - wd) // tm if wd >= tm else tps - 1
            out_shape.append(jax.ShapeDtypeStruct((nseq, GROUP_W, wd), F32))
            out_specs.append(pl.BlockSpec(
                (None, GROUP_W, min(wd, tm)),
                lambda i, first=first: (i // tps, 0, jnp.maximum(i % tps - first, 0))))
    return pl.pallas_call(
        functools.partial(_ffn_mix_prompt_kernel, tiles_per_seq=tps), grid=(nt,), in_specs=in_specs,
        out_specs=out_specs, out_shape=out_shape,
        scratch_shapes=[pltpu.VMEM((tm, w["ffn1_wo"].shape[0]), BF16), pltpu.VMEM((CONV_PAD, cc), F32)],
        compiler_params=pltpu.CompilerParams(dimension_semantics=("arbitrary",), vmem_limit_bytes=VMEM_LIMIT),
        name="ffn_mix_prompt",
    )(x, *rope, *weights)


def _attn_unit(q, kp, kc, vp, vc, first):
    row = lax.broadcasted_iota(jnp.int32, (BLK, BLK), 0)
    col = lax.broadcasted_iota(jnp.int32, (BLK, BLK), 1)
    bias_p = jnp.where(col >= row, 0.0, NEG) + jnp.where(first, NEG, 0.0)
    bias_c = jnp.where(col <= row, 0.0, NEG)
    head = lax.broadcasted_iota(jnp.int32, (1, GROUP_W), 1) // HEAD_DIM
    contract_last = (((1,), (1,)), ((), ()))
    out = jnp.zeros((BLK, GROUP_W), F32)
    lse = jnp.zeros((BLK, GROUP_W), F32)
    zero = jnp.zeros((), BF16)
    for h in range(HEADS_PER_GROUP):
        hm = head == h
        qh = jnp.where(hm, q, zero)
        sp = lax.dot_general(qh, kp, contract_last, preferred_element_type=F32) * ATTN_SCALE + bias_p
        sc = lax.dot_general(qh, kc, contract_last, preferred_element_type=F32) * ATTN_SCALE + bias_c
        m = jnp.maximum(jnp.max(sp, axis=1, keepdims=True), jnp.max(sc, axis=1, keepdims=True))
        pp = jnp.exp(sp - m)
        pc = jnp.exp(sc - m)
        l = jnp.sum(pp, axis=1, keepdims=True) + jnp.sum(pc, axis=1, keepdims=True)
        o = jnp.dot(pp.astype(BF16), jnp.where(hm, vp, zero), preferred_element_type=F32) \
            + jnp.dot(pc.astype(BF16), jnp.where(hm, vc, zero), preferred_element_type=F32)
        out = out + o / l
        lse = lse + jnp.where(hm, m + jnp.log(l), 0.0)
    return out, lse


def _attn_prompt_body(ins, outs, t, u):
    for g in range(N_GROUPS):
        q_ref, kc_ref, kp_ref, vc_ref, vp_ref = ins[5 * g:5 * g + 5]
        d = DILATIONS[g]
        first = (t * (UNITS // d) + u // d) == 0
        o, lse = _attn_unit(q_ref[...], kp_ref[...], kc_ref[...], vp_ref[...], vc_ref[...], first)
        outs[2 * g][...] = o.astype(BF16)
        outs[2 * g + 1][...] = lse


def _attn_prompt_kernel(*refs):
    _attn_prompt_body(refs[:5 * N_GROUPS], refs[5 * N_GROUPS:], pl.program_id(1), pl.program_id(2))


def _attn_prompt_specs(qd, kd, vd, batch, seq_len, index):
    assert seq_len % ATTN_TILE == 0
    args, in_specs, out_shape, out_specs = [], [], [], []
    for g in range(N_GROUPS):
        d = DILATIONS[g]
        nb = UNITS // d

        def cur(*i, d=d, nb=nb):
            b, t, u = index(*i)
            return (b, t * nb + u // d, u % d)

        def prev(*i, d=d, nb=nb):
            b, t, u = index(*i)
            return (b, jnp.maximum(t * nb + u // d - 1, 0), u % d)

        view = lambda a, d=d: a.reshape(batch, seq_len // d, d * GROUP_W)
        blk = lambda im: pl.BlockSpec((None, BLK, GROUP_W), im)
        args += [view(qd[g]), view(kd[g]), view(kd[g]), view(vd[g]), view(vd[g])]
        in_specs += [blk(cur), blk(cur), blk(prev), blk(cur), blk(prev)]
        out_shape += [jax.ShapeDtypeStruct((batch, seq_len // d, d * GROUP_W), BF16),
                      jax.ShapeDtypeStruct((batch, seq_len // d, d * GROUP_W), F32)]
        out_specs += [blk(cur), blk(cur)]
    return args, in_specs, out_shape, out_specs


def _attn_prompt(qd, kd, vd, batch, seq_len):
    args, in_specs, out_shape, out_specs = _attn_prompt_specs(qd, kd, vd, batch, seq_len, lambda b, t, u: (b, t, u))
    res = pl.pallas_call(
        _attn_prompt_kernel, grid=(batch, seq_len // ATTN_TILE, UNITS), in_specs=in_specs, out_specs=out_specs,
        out_shape=out_shape,
        compiler_params=pltpu.CompilerParams(dimension_semantics=("arbitrary",) * 3),
        name="attn_prompt",
    )(*args)
    return [r.reshape(-1, r.shape[2]) for r in res]


def _conv_sample_kernel(state_ref, u_ref, cw_ref, cb_ref, lng_ref, lnb_ref, a_ref, new_state_ref):
    hist = state_ref.shape[0]
    s_len = u_ref.shape[0]
    row = lambda t: state_ref[t] if t < hist else u_ref[t - hist]
    for s in range(s_len):
        acc = None
        for j in range(CONV_WIDTH):
            term = row(s + j) * cw_ref[j:j + 1, :]
            acc = term if acc is None else acc + term
        a_ref[s] = _layernorm_silu(acc + cb_ref[...], lng_ref[...], lnb_ref[...])
    for t in range(hist):
        new_state_ref[t] = row(t + s_len)


def _conv_sample(state_t, u_t, w):
    hist, db, cc = state_t.shape
    s_len = u_t.shape[0]
    assert hist == CONV_WIDTH - 1 and db % CONV_SEQS == 0
    conv_w = [w["conv_w"], w["conv_b"], w["conv_ln_g"], w["conv_ln_b"]]
    slab = lambda rows: pl.BlockSpec((rows, CONV_SEQS, cc), lambda i: (0, i, 0))
    return pl.pallas_call(
        _conv_sample_kernel, grid=(db // CONV_SEQS,),
        in_specs=[slab(hist), slab(s_len)] + [_resident(a.shape) for a in conv_w],
        out_specs=[slab(s_len), slab(hist)],
        out_shape=[jax.ShapeDtypeStruct((s_len, db, cc), F32), jax.ShapeDtypeStruct((hist, db, cc), F32)],
        compiler_params=pltpu.CompilerParams(dimension_semantics=("arbitrary",)),
        name="conv_sample",
    )(state_t, u_t, *conv_w)


def _sample_bias(width, d, s_len):
    s = lax.broadcasted_iota(jnp.int32, (SAMPLE_ROWS, width), 0)
    t = lax.broadcasted_iota(jnp.int32, (SAMPLE_ROWS, width), 1)
    same_residue = jnp.bitwise_and(s - t, d - 1) == 0
    live = s < s_len
    old = jnp.where(live & same_residue & (t >= s), 0.0, NEG)
    sn = s[:, :LANES]
    tn = t[:, :LANES]
    new = jnp.where(live[:, :LANES] & same_residue[:, :LANES] & (tn <= sn), 0.0, NEG)
    return old, new


def _attn_sample_kernel(*refs, s_len):
    _attn_sample_body(*refs, s_len=s_len)


def _attn_sample_body(q_ref, kn_ref, vn_ref, kc0_ref, vc0_ref, kc1_ref, vc1_ref, kc2_ref, vc2_ref,
                      b_ref, nk0_ref, nv0_ref, nk1_ref, nv1_ref, nk2_ref, nv2_ref, *, s_len):
    kc_refs = (kc0_ref, kc1_ref, kc2_ref)
    vc_refs = (vc0_ref, vc1_ref, vc2_ref)
    nk_refs = (nk0_ref, nk1_ref, nk2_ref)
    nv_refs = (nv0_ref, nv1_ref, nv2_ref)
    q = q_ref[...]
    pad = jnp.zeros((LANES - SAMPLE_ROWS, ATTN_W), F32)
    knt = jnp.transpose(jnp.concatenate([kn_ref[...], pad], axis=0))
    vnt = jnp.transpose(jnp.concatenate([vn_ref[...], pad], axis=0))
    biases = [_sample_bias(WINDOWS[g], DILATIONS[g], s_len) for g in range(N_GROUPS)]
    lane = lax.broadcasted_iota(jnp.int32, (1, LANES), 1)
    keep = lane < LANES - s_len
    nt = (((1,), (1,)), ((), ()))

    def shifted(old, new_cols, out_ref, h):
        w = old.shape[1]
        cols = [old[:, j * LANES:(j + 1) * LANES] for j in range(w // LANES)] + [new_cols]
        rolled = [pltpu.roll(c, LANES - s_len, 1) for c in cols]
        for j in range(w // LANES):
            out_ref[h, :, j * LANES:(j + 1) * LANES] = jnp.where(keep, rolled[j], rolled[j + 1])

    heads = range(HEADS_PER_GROUP)
    lanes_of = lambda g, h: slice((g * HEADS_PER_GROUP + h) * HEAD_DIM, (g * HEADS_PER_GROUP + h + 1) * HEAD_DIM)
    s_old = [[None] * N_GROUPS for _ in heads]
    s_new = [[None] * N_GROUPS for _ in heads]
    for h in heads:
        for g in range(N_GROUPS):
            qh = q[:, lanes_of(g, h)].astype(BF16)
            k_old = kc_refs[g][h]
            k_new = knt[lanes_of(g, h), :]
            s_old[h][g] = jnp.dot(qh, k_old.astype(BF16), preferred_element_type=F32) * ATTN_SCALE + biases[g][0]
            s_new[h][g] = jnp.dot(qh, k_new.astype(BF16), preferred_element_type=F32) * ATTN_SCALE + biases[g][1]
            shifted(k_old, k_new, nk_refs[g], h)
    p_old = [[None] * N_GROUPS for _ in heads]
    p_new = [[None] * N_GROUPS for _ in heads]
    dens = []
    for h in heads:
        m = None
        for a in s_old[h] + s_new[h]:
            am = jnp.max(a, axis=1, keepdims=True)
            m = am if m is None else jnp.maximum(m, am)
        den = jnp.zeros((SAMPLE_ROWS, 1), F32)
        for g in range(N_GROUPS):
            p_old[h][g] = jnp.exp(s_old[h][g] - m)
            p_new[h][g] = jnp.exp(s_new[h][g] - m)
            den = den + jnp.sum(p_old[h][g], axis=1, keepdims=True) + jnp.sum(p_new[h][g], axis=1, keepdims=True)
        dens.append(den)
    for h in heads:
        num = jnp.zeros((SAMPLE_ROWS, HEAD_DIM), F32)
        for g in range(N_GROUPS):
            v_old = vc_refs[g][h]
            v_new = vnt[lanes_of(g, h), :]
            num = num + lax.dot_general(p_old[h][g].astype(BF16), v_old.astype(BF16), nt,
                                        preferred_element_type=F32) \
                + lax.dot_general(p_new[h][g].astype(BF16), v_new.astype(BF16), nt, preferred_element_type=F32)
            shifted(v_old, v_new, nv_refs[g], h)
        b_ref[:, h * HEAD_DIM:(h + 1) * HEAD_DIM] = num / dens[h]


def _attn_sample_specs(q8, kn8, vn8, caches_k, caches_v, s_len):
    db = q8.shape[0]
    assert q8.shape[1] == SAMPLE_ROWS and s_len <= min(SAMPLE_ROWS, DILATIONS[1])
    tok = pl.BlockSpec((None, SAMPLE_ROWS, ATTN_W), lambda b: (b, 0, 0))
    args, in_specs, out_shape, out_specs = [q8, kn8, vn8], [tok] * 3, [], []
    out_shape.append(jax.ShapeDtypeStruct((db, SAMPLE_ROWS, GROUP_W), F32))
    out_specs.append(pl.BlockSpec((None, SAMPLE_ROWS, GROUP_W), lambda b: (b, 0, 0)))
    for g in range(N_GROUPS):
        shape = (db, HEADS_PER_GROUP, HEAD_DIM, WINDOWS[g])
        spec = pl.BlockSpec((None,) + shape[1:], lambda b: (b, 0, 0, 0))
        for c in (caches_k[g], caches_v[g]):
            assert c.shape == shape
            args.append(c)
            in_specs.append(spec)
            out_shape.append(jax.ShapeDtypeStruct(shape, F32))
            out_specs.append(spec)
    return args, in_specs, out_shape, out_specs


def _attn_sample(q8, kn8, vn8, caches_k, caches_v, s_len):
    db = q8.shape[0]
    args, in_specs, out_shape, out_specs = _attn_sample_specs(q8, kn8, vn8, caches_k, caches_v, s_len)
    return pl.pallas_call(
        functools.partial(_attn_sample_kernel, s_len=s_len), grid=(db,), in_specs=in_specs, out_specs=out_specs,
        out_shape=out_shape,
        compiler_params=pltpu.CompilerParams(dimension_semantics=("arbitrary",), vmem_limit_bytes=VMEM_LIMIT),
        name="attn_sample",
    )(*args)


def _attn_both_kernel(*refs, s_len, tiles):
    n_pi, n_si = 5 * N_GROUPS, 3 + 2 * N_GROUPS
    n_po = 2 * N_GROUPS
    p_in, s_in = refs[:n_pi], refs[n_pi:n_pi + n_si]
    p_out, s_out = refs[n_pi + n_si:n_pi + n_si + n_po], refs[n_pi + n_si + n_po:]
    i = pl.program_id(0)
    _attn_prompt_body(p_in, p_out, (i // UNITS) % tiles, i % UNITS)
    _attn_sample_body(*s_in, *s_out, s_len=s_len)


def _attn_both(qd, kd, vd, batch, seq_len, q8, kn8, vn8, caches_k, caches_v, s_len):
    tiles = seq_len // ATTN_TILE
    n = batch * tiles * UNITS
    assert n == q8.shape[0]
    index = lambda i: (i // (tiles * UNITS), (i // UNITS) % tiles, i % UNITS)
    p_args, p_in, p_shape, p_out = _attn_prompt_specs(qd, kd, vd, batch, seq_len, index)
    s_args, s_in, s_shape, s_out = _attn_sample_specs(q8, kn8, vn8, caches_k, caches_v, s_len)
    res = pl.pallas_call(
        functools.partial(_attn_both_kernel, s_len=s_len, tiles=tiles), grid=(n,), in_specs=p_in + s_in,
        out_specs=p_out + s_out, out_shape=p_shape + s_shape,
        compiler_params=pltpu.CompilerParams(dimension_semantics=("arbitrary",), vmem_limit_bytes=VMEM_LIMIT),
        name="attn_both",
    )(*p_args, *s_args)
    return [r.reshape(-1, r.shape[2]) for r in res[:len(p_shape)]], res[len(p_shape):]


def _merge_stage(x1_ref, a, battn, gm_ref, wg_ref, bg_ref, wao_ref, wout_ref):
    x1 = x1_ref[...]
    d = x1.shape[1]
    h = _rms(x1, gm_ref[...]).astype(BF16)
    gates = jax.nn.sigmoid(jnp.dot(h, wg_ref[...], preferred_element_type=F32) + bg_ref[...])
    b = jnp.dot(battn.astype(BF16), wao_ref[...], preferred_element_type=F32)
    mix = (gates[:, :d] * a + gates[:, d:] * b).astype(BF16)
    return x1 + jnp.dot(mix, wout_ref[...], preferred_element_type=F32)


def _ffn_stage(y, g2_ref, wgu_ref, wo_ref, act_ref):
    return y + 0.5 * _swiglu(_rms(y, g2_ref[...]).astype(BF16), wgu_ref, wo_ref, act_ref)


def _undilate(blk, pt_ref, d):
    if d == 1:
        return blk.astype(F32)
    rows = jnp.concatenate([blk[:, r * GROUP_W:(r + 1) * GROUP_W] for r in range(d)], axis=0)
    pt = pt_ref[...]
    if rows.dtype == BF16:
        return jnp.dot(pt, rows, preferred_element_type=F32)
    hi = rows.astype(BF16)
    rest = rows - hi.astype(F32)
    mid = rest.astype(BF16)
    lo = (rest - mid.astype(F32)).astype(BF16)
    return jnp.dot(pt, hi, preferred_element_type=F32) + jnp.dot(pt, mid, preferred_element_type=F32) \
        + jnp.dot(pt, lo, preferred_element_type=F32)


def _merge_ffn_prompt_kernel(x1_ref, a_ref, o0_ref, o1_ref, o2_ref, l0_ref, l1_ref, l2_ref, pt1_ref, pt2_ref,
                             gm_ref, wg_ref, bg_ref, wao_ref, wout_ref, g2_ref, wgu_ref, wo_ref, out_ref, act_ref):
    pts = (None, pt1_ref, pt2_ref)
    lses = [_undilate(r[...], pts[g], DILATIONS[g]) for g, r in enumerate((l0_ref, l1_ref, l2_ref))]
    outs = [_undilate(r[...], pts[g], DILATIONS[g]) for g, r in enumerate((o0_ref, o1_ref, o2_ref))]
    mx = jnp.maximum(jnp.maximum(lses[0], lses[1]), lses[2])
    num = None
    den = None
    for o, l in zip(outs, lses):
        wgt = jnp.exp(l - mx)
        num = wgt * o if num is None else num + wgt * o
        den = wgt if den is None else den + wgt
    y = _merge_stage(x1_ref, a_ref[...].astype(F32), num / den, gm_ref, wg_ref, bg_ref, wao_ref, wout_ref)
    out_ref[...] = _ffn_stage(y, g2_ref, wgu_ref, wo_ref, act_ref)


def _merge_ffn_sample_kernel(x1_ref, a_ref, b_ref, gm_ref, wg_ref, bg_ref, wco_ref, wao_ref, wout_ref, g2_ref,
                             wgu_ref, wo_ref, out_ref, act_ref):
    a = jnp.dot(a_ref[...].astype(BF16), wco_ref[...], preferred_element_type=F32)
    y = _merge_stage(x1_ref, a, b_ref[...], gm_ref, wg_ref, bg_ref, wao_ref, wout_ref)
    out_ref[...] = _ffn_stage(y, g2_ref, wgu_ref, wo_ref, act_ref)


def _merge_weights(w):
    return [w["mix_norm"], w["w_in_g"], w["b_gate"], w["w_conv_out"], w["w_attn_out"], w["w_out"], w["ffn2_norm"],
            w["ffn2_wgu"], w["ffn2_wo"]]


def _merge_ffn_sample(x1, a_pre, b, w):
    n, d = x1.shape
    tm = TOKEN_TILE
    assert n % tm == 0
    tok = lambda width: pl.BlockSpec((tm, width), lambda i: (i, 0))
    weights = _merge_weights(w)
    return pl.pallas_call(
        _merge_ffn_sample_kernel, grid=(n // tm,),
        in_specs=[tok(d), tok(a_pre.shape[1]), tok(GROUP_W)] + [_resident(a.shape) for a in weights],
        out_specs=tok(d), out_shape=jax.ShapeDtypeStruct((n, d), F32),
        scratch_shapes=[pltpu.VMEM((tm, w["ffn2_wo"].shape[0]), BF16)],
        compiler_params=pltpu.CompilerParams(dimension_semantics=("arbitrary",), vmem_limit_bytes=VMEM_LIMIT),
        name="merge_ffn_sample",
    )(x1, a_pre, b, *weights)


def _merge_ffn_prompt(x1, a, outs, lses, w):
    n, d = x1.shape
    tm = TOKEN_TILE
    assert n % tm == 0
    tok = lambda rows, width: pl.BlockSpec((rows, width), lambda i: (i, 0))
    grp = [tok(tm // dl, dl * GROUP_W) for dl in DILATIONS]
    mw = _merge_weights(w)
    weights = [w["permt1"], w["permt2"]] + [x for x in mw if x is not w["w_conv_out"]]
    return pl.pallas_call(
        _merge_ffn_prompt_kernel, grid=(n // tm,),
        in_specs=[tok(tm, d), tok(tm, d)] + grp + grp + [_resident(x.shape) for x in weights],
        out_specs=tok(tm, d), out_shape=jax.ShapeDtypeStruct((n, d), F32),
        scratch_shapes=[pltpu.VMEM((tm, w["ffn2_wo"].shape[0]), BF16)],
        compiler_params=pltpu.CompilerParams(dimension_semantics=("arbitrary",), vmem_limit_bytes=VMEM_LIMIT),
        name="merge_ffn_prompt",
    )(x1, a, *outs, *lses, *weights)


def _rope_tables(pos):
    half = ROT_DIM // 2
    inv = jnp.float32(ROPE_THETA) ** (-jnp.arange(half, dtype=F32) * (2.0 / ROT_DIM))
    ang = pos.astype(F32)[:, None] * inv[None, :]
    cos, sin = jnp.cos(ang), jnp.sin(ang)
    n = pos.shape[0]
    rest = HEAD_DIM - ROT_DIM
    zh = jnp.zeros((n, half), F32)
    c = jnp.concatenate([cos, cos, jnp.ones((n, rest), F32)], axis=1)
    s1 = jnp.concatenate([zh, sin, jnp.zeros((n, rest), F32)], axis=1)
    s2 = jnp.concatenate([-sin, zh, jnp.zeros((n, rest), F32)], axis=1)
    rep = LANES // HEAD_DIM
    return tuple(jnp.tile(t, (1, rep)) for t in (c, s1, s2))


def _prepare_weights(l, ffn1_norm, ffn1_w_in, ffn1_w_out, mix_norm, w_in, b_gate, q_norm, k_norm, conv_w, conv_b,
                     conv_ln_g, conv_ln_b, w_conv_out, w_attn_out, w_out, ffn2_norm, ffn2_w_in, ffn2_w_out):
    cc = conv_w.shape[2]
    n_a = 2 * cc + 3 * ATTN_W
    lane = jnp.arange(GROUP_W)
    same_head = (lane[:, None] // HEAD_DIM) == (lane[None, :] // HEAD_DIM)
    row = lambda a: a.reshape(1, -1).astype(F32)
    tok = jnp.arange(TOKEN_TILE)

    def perm(d):
        src = (tok % (TOKEN_TILE // d)) * d + tok // (TOKEN_TILE // d)
        return (src[:, None] == tok[None, :]).astype(BF16)

    return {
        "perm1": perm(DILATIONS[1]), "perm2": perm(DILATIONS[2]),
        "permt1": perm(DILATIONS[1]).T, "permt2": perm(DILATIONS[2]).T,
        "ffn1_norm": row(ffn1_norm[l]), "ffn1_wgu": ffn1_w_in[l].astype(BF16), "ffn1_wo": ffn1_w_out[l].astype(BF16),
        "mix_norm": row(mix_norm[l]), "w_in_a": w_in[l][:, :n_a].astype(BF16), "w_in_g": w_in[l][:, n_a:].astype(BF16),
        "b_gate": row(b_gate[l]),
        "head_mean": jnp.where(same_head, 1.0 / HEAD_DIM, 0.0).astype(BF16),
        "q_gain": row(q_norm[l]), "k_gain": row(k_norm[l]),
        "conv_w": jnp.concatenate([conv_w[l], jnp.zeros((CONV_PAD - CONV_WIDTH, cc), F32)], axis=0),
        "conv_b": row(conv_b[l]), "conv_ln_g": row(conv_ln_g[l]), "conv_ln_b": row(conv_ln_b[l]),
        "w_conv_out": w_conv_out[l].astype(BF16), "w_attn_out": w_attn_out[l].astype(BF16),
        "w_out": w_out[l].astype(BF16),
        "ffn2_norm": row(ffn2_norm[l]), "ffn2_wgu": ffn2_w_in[l].astype(BF16), "ffn2_wo": ffn2_w_out[l].astype(BF16),
    }


def kernel(x_prompt, x_sample, cache_k_w128, cache_v_w128, cache_k_w512, cache_v_w512, cache_k_w2048, cache_v_w2048, state_conv, ffn1_norm, ffn1_w_in, ffn1_w_out, mix_norm, w_in, b_gate, q_norm, k_norm, conv_w, conv_b, conv_ln_g, conv_ln_b, w_conv_out, w_attn_out, w_out, ffn2_norm, ffn2_w_in, ffn2_w_out):
    batch, t, d = x_prompt.shape
    db, s_len, _ = x_sample.shape
    depth = w_in.shape[0]
    cc = conv_w.shape[2]
    hd = (HEADS_PER_GROUP, HEAD_DIM)
    cache_k = (cache_k_w128, cache_k_w512, cache_k_w2048)
    cache_v = (cache_v_w128, cache_v_w512, cache_v_w2048)
    for g, wdw in enumerate(WINDOWS):
        assert cache_k[g].shape == (depth, db, wdw, *hd), "sample caches must hold a full window"
        assert t >= wdw
    rope_p = _rope_tables(jnp.arange(t))
    rope_s = _rope_tables(PAST_LEN + jnp.arange(db * s_len) // db)
    to_time_minor = lambda c: jnp.transpose(c, (0, 2, 3, 1))
    from_time_minor = lambda c: jnp.transpose(c, (0, 3, 1, 2))

    xp = x_prompt.reshape(batch * t, d)
    xs = jnp.transpose(x_sample, (1, 0, 2)).reshape(s_len * db, d)
    outs_p = [[] for _ in range(2 * N_GROUPS)]
    outs_s = [[] for _ in range(2 * N_GROUPS)]
    new_conv_p, new_conv_s = [], []
    for l in range(depth):
        w = _prepare_weights(l, ffn1_norm, ffn1_w_in, ffn1_w_out, mix_norm, w_in, b_gate, q_norm, k_norm, conv_w,
                             conv_b, conv_ln_g, conv_ln_b, w_conv_out, w_attn_out, w_out, ffn2_norm, ffn2_w_in,
                             ffn2_w_out)
        x1p, *rest = _ffn_mix_prompt(xp, rope_p, w, t)
        qkv, (a_p, utail), kv_t = rest[:9], rest[9:11], rest[11:]
        x1s, qs, k32s, v32s, us = _ffn_mix_sample(xs, rope_s, w)
        a_s, state_new = _conv_sample(jnp.transpose(state_conv[l], (1, 0, 2)), us.reshape(s_len, db, cc), w)
        rows8 = lambda a: jnp.pad(jnp.transpose(a.reshape(s_len, db, ATTN_W), (1, 0, 2)),
                                  ((0, 0), (0, SAMPLE_ROWS - s_len), (0, 0)))
        ck = [to_time_minor(cache_k[g][l]) for g in range(N_GROUPS)]
        cv = [to_time_minor(cache_v[g][l]) for g in range(N_GROUPS)]
        if batch * (t // ATTN_TILE) * UNITS == db:
            attn_p, (b_s, *new_caches) = _attn_both(qkv[0:3], qkv[3:6], qkv[6:9], batch, t, rows8(qs), rows8(k32s),
                                                    rows8(v32s), ck, cv, s_len)
        else:
            attn_p = _attn_prompt(qkv[0:3], qkv[3:6], qkv[6:9], batch, t)
            b_s, *new_caches = _attn_sample(rows8(qs), rows8(k32s), rows8(v32s), ck, cv, s_len)
        xp = _merge_ffn_prompt(x1p, a_p, attn_p[0::2], attn_p[1::2], w)
        b_s = jnp.transpose(b_s[:, :s_len], (1, 0, 2)).reshape(s_len * db, GROUP_W)
        xs = _merge_ffn_sample(x1s, a_s.reshape(s_len * db, cc), b_s, w)
        new_conv_p.append(utail[:, CONV_PAD - (CONV_WIDTH - 1):, :])
        new_conv_s.append(jnp.transpose(state_new, (1, 0, 2)))
        for i, (g, c) in enumerate([(g, c) for c in range(2) for g in range(N_GROUPS)]):
            outs_p[2 * g + c].append(from_time_minor(kv_t[i].reshape(batch, *hd, WINDOWS[g])))
        for i, nc in enumerate(new_caches):
            outs_s[i].append(from_time_minor(nc))
    y_s = jnp.transpose(xs.reshape(s_len, db, d), (1, 0, 2))
    return (xp.reshape(batch, t, d), y_s, *[jnp.stack(o) for o in outs_p], jnp.stack(new_conv_p),
            *[jnp.stack(o) for o in outs_s], jnp.stack(new_conv_s))
```

```python
import functools

import jax
import jax.numpy as jnp
from jax import lax
from jax.experimental import pallas as pl
from jax.experimental.pallas import tpu as pltpu

F32 = jnp.float32
BF16 = jnp.bfloat16

HEAD_DIM = 64
WINDOWS = (128, 512, 2048)
DILATIONS = (1, 4, 16)
N_GROUPS = 3
HEADS_PER_GROUP = 4
GROUP_W = HEADS_PER_GROUP * HEAD_DIM
ATTN_W = N_GROUPS * GROUP_W
BLK = WINDOWS[0] // DILATIONS[0]
assert all(w // d == BLK for w, d in zip(WINDOWS, DILATIONS))
ATTN_TILE = max(WINDOWS)
UNITS = ATTN_TILE // BLK
ATTN_SCALE = HEAD_DIM ** -0.5
ROT_DIM = HEAD_DIM // 4
ROPE_THETA = 500000.0
CONV_WIDTH = 31
CONV_PAD = 32
NORM_EPS = 1e-6
PAST_LEN = 2048
NEG = -1e30
LANES = 128
SUBLANES = 8
FF_CHUNK = 256
TOKEN_TILE = 256
SAMPLE_ROWS = 8
CONV_SEQS = 16
VMEM_LIMIT = 56 * 1024 * 1024


def _resident(shape):
    nd = len(shape)
    return pl.BlockSpec(shape, lambda *_: (0,) * nd, pipeline_mode=pl.Buffered(1))


def _rms(x, g):
    ms = jnp.mean(x * x, axis=-1, keepdims=True)
    return x * lax.rsqrt(ms + NORM_EPS) * g


def _swiglu(h, wgu_ref, wo_ref, act_ref):
    f = wo_ref.shape[0]
    assert f % FF_CHUNK == 0
    for c in range(f // FF_CHUNK):
        g = jnp.dot(h, wgu_ref[:, c * FF_CHUNK:(c + 1) * FF_CHUNK], preferred_element_type=F32)
        u = jnp.dot(h, wgu_ref[:, f + c * FF_CHUNK:f + (c + 1) * FF_CHUNK], preferred_element_type=F32)
        act_ref[:, c * FF_CHUNK:(c + 1) * FF_CHUNK] = (g * jax.nn.sigmoid(g) * u).astype(BF16)
    return jnp.dot(act_ref[...], wo_ref[...], preferred_element_type=F32)


def _head_norm(y, hm_ref, gain):
    parts = []
    for g in range(N_GROUPS):
        yg = y[:, g * GROUP_W:(g + 1) * GROUP_W]
        ms = jnp.dot((yg * yg).astype(BF16), hm_ref[...], preferred_element_type=F32)
        parts.append(yg * lax.rsqrt(ms + NORM_EPS))
    return jnp.concatenate(parts, axis=1) * gain


def _rope(y, c, s1, s2):
    half = ROT_DIM // 2
    parts = []
    for i in range(y.shape[1] // LANES):
        yc = y[:, i * LANES:(i + 1) * LANES]
        parts.append(yc * c + pltpu.roll(yc, half, 1) * s1 + pltpu.roll(yc, LANES - half, 1) * s2)
    return jnp.concatenate(parts, axis=1)


def _layernorm_silu(y, g, b):
    mu = jnp.mean(y, axis=-1, keepdims=True)
    yc = y - mu
    var = jnp.mean(yc * yc, axis=-1, keepdims=True)
    yn = yc * lax.rsqrt(var + NORM_EPS) * g + b
    return yn * jax.nn.sigmoid(yn)


def _causal_conv(ext, cw_ref, cb_ref, lng_ref, lnb_ref):
    tm = ext.shape[0] - CONV_PAD
    first = CONV_PAD - (CONV_WIDTH - 1)
    rows = 64
    outs = []
    for r0 in range(0, tm, rows):
        acc = None
        base = ext[r0:r0 + rows + CONV_PAD, :]
        for r in range(SUBLANES):
            win = base if r == 0 else pltpu.roll(base, base.shape[0] - r, 0)
            for j in range(CONV_WIDTH):
                if (first + j) % SUBLANES != r:
                    continue
                o = first + j - r
                term = win[o:o + rows, :] * cw_ref[j:j + 1, :]
                acc = term if acc is None else acc + term
        outs.append(_layernorm_silu(acc + cb_ref[...], lng_ref[...], lnb_ref[...]))
    return jnp.concatenate(outs, axis=0)


def _proj_stage(x_ref, g1_ref, wgu_ref, wo_ref, gm_ref, x1_ref, act_ref):
    x = x_ref[...]
    h = _rms(x, g1_ref[...]).astype(BF16)
    x1 = x + 0.5 * _swiglu(h, wgu_ref, wo_ref, act_ref)
    x1_ref[...] = x1
    return _rms(x1, gm_ref[...]).astype(BF16)


def _mix_parts(h2, win_ref, rc_ref, rs1_ref, rs2_ref, hm_ref, qg_ref, kg_ref):
    cc = (win_ref.shape[1] - 3 * ATTN_W) // 2
    proj = lambda lo, hi: jnp.dot(h2, win_ref[:, lo:hi], preferred_element_type=F32)
    zu = proj(0, 2 * cc)
    yield "u", zu[:, :cc] * jax.nn.sigmoid(zu[:, cc:])
    c, s1, s2 = rc_ref[...], rs1_ref[...], rs2_ref[...]
    yield "q", _rope(_head_norm(proj(2 * cc, 2 * cc + ATTN_W), hm_ref, qg_ref[...]), c, s1, s2)
    yield "k", _rope(_head_norm(proj(2 * cc + ATTN_W, 2 * cc + 2 * ATTN_W), hm_ref, kg_ref[...]), c, s1, s2)
    yield "v", proj(2 * cc + 2 * ATTN_W, 2 * cc + 3 * ATTN_W)


def _store_dilated(y, perm_refs, out_refs):
    yb = y.astype(BF16)
    out_refs[0][...] = yb[:, :GROUP_W]
    for g in range(1, N_GROUPS):
        d = DILATIONS[g]
        rows = y.shape[0] // d
        yp = jnp.dot(perm_refs[g - 1][...], yb[:, g * GROUP_W:(g + 1) * GROUP_W],
                     preferred_element_type=F32).astype(BF16)
        for r in range(d):
            out_refs[g][:, r * GROUP_W:(r + 1) * GROUP_W] = yp[r * rows:(r + 1) * rows, :]


def _ffn_mix_prompt_kernel(x_ref, rc_ref, rs1_ref, rs2_ref, g1_ref, wgu_ref, wo_ref, gm_ref, win_ref, hm_ref, qg_ref,
                           kg_ref, p1_ref, p2_ref, cw_ref, cb_ref, lng_ref, lnb_ref, wco_ref,
                           x1_ref, q0_ref, q1_ref, q2_ref, k0_ref, k1_ref, k2_ref, v0_ref, v1_ref, v2_ref, a_ref,
                           utail_ref, kt0_ref, kt1_ref, kt2_ref, vt0_ref, vt1_ref, vt2_ref, act_ref, carry_ref, *,
                           tiles_per_seq):
    tm = x_ref.shape[0]
    i = pl.program_id(0)

    @pl.when(i == 0)
    def _():
        carry_ref[...] = jnp.zeros(carry_ref.shape, F32)

    h2 = _proj_stage(x_ref, g1_ref, wgu_ref, wo_ref, gm_ref, x1_ref, act_ref)
    perms = (p1_ref, p2_ref)
    dilated = {"q": (q0_ref, q1_ref, q2_ref), "k": (k0_ref, k1_ref, k2_ref), "v": (v0_ref, v1_ref, v2_ref)}
    windows = {"k": (kt0_ref, kt1_ref, kt2_ref), "v": (vt0_ref, vt1_ref, vt2_ref)}
    parts = dict(_mix_parts(h2, win_ref, rc_ref, rs1_ref, rs2_ref, hm_ref, qg_ref, kg_ref))
    u = parts.pop("u")
    for name, y in parts.items():
        _store_dilated(y, perms, dilated[name])
        if name in windows:
            yt = jnp.transpose(y)
            for g, ref in enumerate(windows[name]):
                ref[...] = yt[g * GROUP_W:(g + 1) * GROUP_W, tm - ref.shape[1]:]
    tile = jnp.zeros(carry_ref.shape, jnp.int32) + i % tiles_per_seq
    ext = jnp.concatenate([jnp.where(tile == 0, 0.0, carry_ref[...]), u], axis=0)
    a_pre = _causal_conv(ext, cw_ref, cb_ref, lng_ref, lnb_ref)
    a_ref[...] = jnp.dot(a_pre.astype(BF16), wco_ref[...], preferred_element_type=F32).astype(BF16)
    tail = u[tm - CONV_PAD:, :]
    utail_ref[...] = tail
    carry_ref[...] = tail


def _ffn_mix_sample_kernel(x_ref, rc_ref, rs1_ref, rs2_ref, g1_ref, wgu_ref, wo_ref, gm_ref, win_ref, hm_ref, qg_ref,
                           kg_ref, x1_ref, q_ref, k32_ref, v32_ref, u_ref, act_ref):
    h2 = _proj_stage(x_ref, g1_ref, wgu_ref, wo_ref, gm_ref, x1_ref, act_ref)
    outs = {"u": u_ref, "q": q_ref, "k": k32_ref, "v": v32_ref}
    for name, y in _mix_parts(h2, win_ref, rc_ref, rs1_ref, rs2_ref, hm_ref, qg_ref, kg_ref):
        outs[name][...] = y


def _ffn_mix_weights(w):
    return [w["ffn1_norm"], w["ffn1_wgu"], w["ffn1_wo"], w["mix_norm"], w["w_in_a"], w["head_mean"],
            w["q_gain"], w["k_gain"]]


def _ffn_mix_sample(x, rope, w):
    n, d = x.shape
    tm = TOKEN_TILE
    assert n % tm == 0 and rope[0].shape[0] == n
    cc = w["conv_w"].shape[1]
    tok = lambda width: pl.BlockSpec((tm, width), lambda i: (i, 0))
    weights = _ffn_mix_weights(w)
    widths = [d, ATTN_W, ATTN_W, ATTN_W, cc]
    return pl.pallas_call(
        _ffn_mix_sample_kernel, grid=(n // tm,),
        in_specs=[tok(d)] + [tok(LANES)] * 3 + [_resident(a.shape) for a in weights],
        out_specs=[tok(wd) for wd in widths], out_shape=[jax.ShapeDtypeStruct((n, wd), F32) for wd in widths],
        scratch_shapes=[pltpu.VMEM((tm, w["ffn1_wo"].shape[0]), BF16)],
        compiler_params=pltpu.CompilerParams(dimension_semantics=("arbitrary",), vmem_limit_bytes=VMEM_LIMIT),
        name="ffn_mix_sample",
    )(x, *rope, *weights)


def _ffn_mix_prompt(x, rope, w, seq_len):
    n, d = x.shape
    tm = TOKEN_TILE
    t = seq_len
    assert n % t == 0 and t % tm == 0 and rope[0].shape[0] == t
    assert all(wd % tm == 0 or tm % wd == 0 for wd in WINDOWS)
    tps = t // tm
    nt = n // tm
    nseq = n // t
    cc = w["conv_w"].shape[1]
    weights = _ffn_mix_weights(w) + [w["perm1"], w["perm2"], w["conv_w"], w["conv_b"], w["conv_ln_g"],
                                     w["conv_ln_b"], w["w_conv_out"]]
    tok = lambda rows, width: pl.BlockSpec((rows, width), lambda i: (i, 0))
    rope_spec = pl.BlockSpec((tm, LANES), lambda i: (i % tps, 0))
    in_specs = [tok(tm, d)] + [rope_spec] * 3 + [_resident(a.shape) for a in weights]
    out_shape = [jax.ShapeDtypeStruct((n, d), F32)]
    out_specs = [tok(tm, d)]
    for _ in range(3):
        for dl in DILATIONS:
            out_shape.append(jax.ShapeDtypeStruct((n // dl, dl * GROUP_W), BF16))
            out_specs.append(tok(tm // dl, dl * GROUP_W))
    out_shape += [jax.ShapeDtypeStruct((n, d), BF16), jax.ShapeDtypeStruct((nseq, CONV_PAD, cc), F32)]
    out_specs += [tok(tm, d), pl.BlockSpec((None, CONV_PAD, cc), lambda i: (i // tps, 0, 0))]
    for _ in range(2):
        for wd in WINDOWS:
            first = (t - wd) // tm if wd >= tm else tps - 1
            out_shape.append(jax.ShapeDtypeStruct((nseq, GROUP_W, wd), F32))
            out_specs.append(pl.BlockSpec(
                (None, GROUP_W, min(wd, tm)),
                lambda i, first=first: (i // tps, 0, jnp.maximum(i % tps - first, 0))))
    return pl.pallas_call(
        functools.partial(_ffn_mix_prompt_kernel, tiles_per_seq=tps), grid=(nt,), in_specs=in_specs,
        out_specs=out_specs, out_shape=out_shape,
        scratch_shapes=[pltpu.VMEM((tm, w["ffn1_wo"].shape[0]), BF16), pltpu.VMEM((CONV_PAD, cc), F32)],
        compiler_params=pltpu.CompilerParams(dimension_semantics=("arbitrary",), vmem_limit_bytes=VMEM_LIMIT),
        name="ffn_mix_prompt",
    )(x, *rope, *weights)


def _attn_unit(q, kp, kc, vp, vc, first):
    row = lax.broadcasted_iota(jnp.int32, (BLK, BLK), 0)
    col = lax.broadcasted_iota(jnp.int32, (BLK, BLK), 1)
    bias_p = jnp.where(col >= row, 0.0, NEG) + jnp.where(first, NEG, 0.0)
    bias_c = jnp.where(col <= row, 0.0, NEG)
    head = lax.broadcasted_iota(jnp.int32, (1, GROUP_W), 1) // HEAD_DIM
    contract_last = (((1,), (1,)), ((), ()))
    out = jnp.zeros((BLK, GROUP_W), F32)
    lse = jnp.zeros((BLK, GROUP_W), F32)
    zero = jnp.zeros((), BF16)
    for h in range(HEADS_PER_GROUP):
        hm = head == h
        qh = jnp.where(hm, q, zero)
        sp = lax.dot_general(qh, kp, contract_last, preferred_element_type=F32) * ATTN_SCALE + bias_p
        sc = lax.dot_general(qh, kc, contract_last, preferred_element_type=F32) * ATTN_SCALE + bias_c
        m = jnp.max(jnp.maximum(sp, sc), axis=1, keepdims=True)
        pp = jnp.exp(sp - m)
        pc = jnp.exp(sc - m)
        l = jnp.sum(pp, axis=1, keepdims=True) + jnp.sum(pc, axis=1, keepdims=True)
        o = jnp.dot(pp.astype(BF16), jnp.where(hm, vp, zero), preferred_element_type=F32) \
            + jnp.dot(pc.astype(BF16), jnp.where(hm, vc, zero), preferred_element_type=F32)
        out = out + o / l
        lse = lse + jnp.where(hm, m + jnp.log(l), 0.0)
    return out, lse


def _attn_prompt_body(ins, outs, t, u):
    for g in range(N_GROUPS):
        q_ref, kc_ref, kp_ref, vc_ref, vp_ref = ins[5 * g:5 * g + 5]
        d = DILATIONS[g]
        first = (t * (UNITS // d) + u // d) == 0
        o, lse = _attn_unit(q_ref[...], kp_ref[...], kc_ref[...], vp_ref[...], vc_ref[...], first)
        outs[2 * g][...] = o.astype(BF16)
        outs[2 * g + 1][...] = lse


def _attn_prompt_kernel(*refs):
    _attn_prompt_body(refs[:5 * N_GROUPS], refs[5 * N_GROUPS:], pl.program_id(1), pl.program_id(2))


def _attn_prompt_specs(qd, kd, vd, batch, seq_len, index):
    assert seq_len % ATTN_TILE == 0
    args, in_specs, out_shape, out_specs = [], [], [], []
    for g in range(N_GROUPS):
        d = DILATIONS[g]
        nb = UNITS // d

        def cur(*i, d=d, nb=nb):
            b, t, u = index(*i)
            return (b, t * nb + u // d, u % d)

        def prev(*i, d=d, nb=nb):
            b, t, u = index(*i)
            return (b, jnp.maximum(t * nb + u // d - 1, 0), u % d)

        view = lambda a, d=d: a.reshape(batch, seq_len // d, d * GROUP_W)
        blk = lambda im: pl.BlockSpec((None, BLK, GROUP_W), im)
        args += [view(qd[g]), view(kd[g]), view(kd[g]), view(vd[g]), view(vd[g])]
        in_specs += [blk(cur), blk(cur), blk(prev), blk(cur), blk(prev)]
        out_shape += [jax.ShapeDtypeStruct((batch, seq_len // d, d * GROUP_W), BF16),
                      jax.ShapeDtypeStruct((batch, seq_len // d, d * GROUP_W), F32)]
        out_specs += [blk(cur), blk(cur)]
    return args, in_specs, out_shape, out_specs


def _attn_prompt(qd, kd, vd, batch, seq_len):
    args, in_specs, out_shape, out_specs = _attn_prompt_specs(qd, kd, vd, batch, seq_len, lambda b, t, u: (b, t, u))
    res = pl.pallas_call(
        _attn_prompt_kernel, grid=(batch, seq_len // ATTN_TILE, UNITS), in_specs=in_specs, out_specs=out_specs,
        out_shape=out_shape,
        compiler_params=pltpu.CompilerParams(dimension_semantics=("arbitrary",) * 3),
        name="attn_prompt",
    )(*args)
    return [r.reshape(-1, r.shape[2]) for r in res]


def _conv_sample_kernel(state_ref, u_ref, cw_ref, cb_ref, lng_ref, lnb_ref, a_ref, new_state_ref):
    hist = state_ref.shape[0]
    s_len = u_ref.shape[0]
    row = lambda t: state_ref[t] if t < hist else u_ref[t - hist]
    for s in range(s_len):
        acc = None
        for j in range(CONV_WIDTH):
            term = row(s + j) * cw_ref[j:j + 1, :]
            acc = term if acc is None else acc + term
        a_ref[s] = _layernorm_silu(acc + cb_ref[...], lng_ref[...], lnb_ref[...])
    for t in range(hist):
        new_state_ref[t] = row(t + s_len)


def _conv_sample(state_t, u_t, w):
    hist, db, cc = state_t.shape
    s_len = u_t.shape[0]
    assert hist == CONV_WIDTH - 1 and db % CONV_SEQS == 0
    conv_w = [w["conv_w"], w["conv_b"], w["conv_ln_g"], w["conv_ln_b"]]
    slab = lambda rows: pl.BlockSpec((rows, CONV_SEQS, cc), lambda i: (0, i, 0))
    return pl.pallas_call(
        _conv_sample_kernel, grid=(db // CONV_SEQS,),
        in_specs=[slab(hist), slab(s_len)] + [_resident(a.shape) for a in conv_w],
        out_specs=[slab(s_len), slab(hist)],
        out_shape=[jax.ShapeDtypeStruct((s_len, db, cc), F32), jax.ShapeDtypeStruct((hist, db, cc), F32)],
        compiler_params=pltpu.CompilerParams(dimension_semantics=("arbitrary",)),
        name="conv_sample",
    )(state_t, u_t, *conv_w)


def _sample_bias(width, d, s_len):
    s = lax.broadcasted_iota(jnp.int32, (SAMPLE_ROWS, width), 0)
    t = lax.broadcasted_iota(jnp.int32, (SAMPLE_ROWS, width), 1)
    same_residue = jnp.bitwise_and(s - t, d - 1) == 0
    live = s < s_len
    old = jnp.where(live & same_residue & (t >= s), 0.0, NEG)
    sn = s[:, :LANES]
    tn = t[:, :LANES]
    new = jnp.where(live[:, :LANES] & same_residue[:, :LANES] & (tn <= sn), 0.0, NEG)
    return old, new


def _attn_sample_kernel(*refs, s_len):
    _attn_sample_body(*refs, s_len=s_len)


def _attn_sample_body(q_ref, kn_ref, vn_ref, kc0_ref, vc0_ref, kc1_ref, vc1_ref, kc2_ref, vc2_ref,
                      b_ref, nk0_ref, nv0_ref, nk1_ref, nv1_ref, nk2_ref, nv2_ref, *, s_len):
    kc_refs = (kc0_ref, kc1_ref, kc2_ref)
    vc_refs = (vc0_ref, vc1_ref, vc2_ref)
    nk_refs = (nk0_ref, nk1_ref, nk2_ref)
    nv_refs = (nv0_ref, nv1_ref, nv2_ref)
    q = q_ref[...]
    pad = jnp.zeros((LANES - SAMPLE_ROWS, ATTN_W), F32)
    knt = jnp.transpose(jnp.concatenate([kn_ref[...], pad], axis=0))
    vnt = jnp.transpose(jnp.concatenate([vn_ref[...], pad], axis=0))
    biases = [_sample_bias(WINDOWS[g], DILATIONS[g], s_len) for g in range(N_GROUPS)]
    lane = lax.broadcasted_iota(jnp.int32, (1, LANES), 1)
    keep = lane < LANES - s_len
    nt = (((1,), (1,)), ((), ()))

    def shifted(old, new_cols, out_ref, h):
        w = old.shape[1]
        cols = [old[:, j * LANES:(j + 1) * LANES] for j in range(w // LANES)] + [new_cols]
        rolled = [pltpu.roll(c, LANES - s_len, 1) for c in cols]
        for j in range(w // LANES):
            out_ref[h, :, j * LANES:(j + 1) * LANES] = jnp.where(keep, rolled[j], rolled[j + 1])

    heads = range(HEADS_PER_GROUP)
    lanes_of = lambda g, h: slice((g * HEADS_PER_GROUP + h) * HEAD_DIM, (g * HEADS_PER_GROUP + h + 1) * HEAD_DIM)
    s_old = [[None] * N_GROUPS for _ in heads]
    s_new = [[None] * N_GROUPS for _ in heads]
    for h in heads:
        for g in range(N_GROUPS):
            qh = q[:, lanes_of(g, h)].astype(BF16)
            k_old = kc_refs[g][h]
            k_new = knt[lanes_of(g, h), :]
            s_old[h][g] = jnp.dot(qh, k_old.astype(BF16), preferred_element_type=F32) * ATTN_SCALE + biases[g][0]
            s_new[h][g] = jnp.dot(qh, k_new.astype(BF16), preferred_element_type=F32) * ATTN_SCALE + biases[g][1]
            shifted(k_old, k_new, nk_refs[g], h)
    p_old = [[None] * N_GROUPS for _ in heads]
    p_new = [[None] * N_GROUPS for _ in heads]
    dens = []
    for h in heads:
        m = None
        for a in s_old[h] + s_new[h]:
            am = jnp.max(a, axis=1, keepdims=True)
            m = am if m is None else jnp.maximum(m, am)
        den = jnp.zeros((SAMPLE_ROWS, 1), F32)
        for g in range(N_GROUPS):
            p_old[h][g] = jnp.exp(s_old[h][g] - m)
            p_new[h][g] = jnp.exp(s_new[h][g] - m)
            den = den + jnp.sum(p_old[h][g], axis=1, keepdims=True) + jnp.sum(p_new[h][g], axis=1, keepdims=True)
        dens.append(den)
    for h in heads:
        num = jnp.zeros((SAMPLE_ROWS, HEAD_DIM), F32)
        for g in range(N_GROUPS):
            v_old = vc_refs[g][h]
            v_new = vnt[lanes_of(g, h), :]
            num = num + lax.dot_general(p_old[h][g].astype(BF16), v_old.astype(BF16), nt,
                                        preferred_element_type=F32) \
                + lax.dot_general(p_new[h][g].astype(BF16), v_new.astype(BF16), nt, preferred_element_type=F32)
            shifted(v_old, v_new, nv_refs[g], h)
        b_ref[:, h * HEAD_DIM:(h + 1) * HEAD_DIM] = num / dens[h]


def _attn_sample_specs(q8, kn8, vn8, caches_k, caches_v, s_len):
    db = q8.shape[0]
    assert q8.shape[1] == SAMPLE_ROWS and s_len <= min(SAMPLE_ROWS, DILATIONS[1])
    tok = pl.BlockSpec((None, SAMPLE_ROWS, ATTN_W), lambda b: (b, 0, 0))
    args, in_specs, out_shape, out_specs = [q8, kn8, vn8], [tok] * 3, [], []
    out_shape.append(jax.ShapeDtypeStruct((db, SAMPLE_ROWS, GROUP_W), F32))
    out_specs.append(pl.BlockSpec((None, SAMPLE_ROWS, GROUP_W), lambda b: (b, 0, 0)))
    for g in range(N_GROUPS):
        shape = (db, HEADS_PER_GROUP, HEAD_DIM, WINDOWS[g])
        spec = pl.BlockSpec((None,) + shape[1:], lambda b: (b, 0, 0, 0))
        for c in (caches_k[g], caches_v[g]):
            assert c.shape == shape
            args.append(c)
            in_specs.append(spec)
            out_shape.append(jax.ShapeDtypeStruct(shape, F32))
            out_specs.append(spec)
    return args, in_specs, out_shape, out_specs


def _attn_sample(q8, kn8, vn8, caches_k, caches_v, s_len):
    db = q8.shape[0]
    args, in_specs, out_shape, out_specs = _attn_sample_specs(q8, kn8, vn8, caches_k, caches_v, s_len)
    return pl.pallas_call(
        functools.partial(_attn_sample_kernel, s_len=s_len), grid=(db,), in_specs=in_specs, out_specs=out_specs,
        out_shape=out_shape,
        compiler_params=pltpu.CompilerParams(dimension_semantics=("arbitrary",), vmem_limit_bytes=VMEM_LIMIT),
        name="attn_sample",
    )(*args)


def _attn_both_kernel(*refs, s_len, tiles):
    n_pi, n_si = 5 * N_GROUPS, 3 + 2 * N_GROUPS
    n_po = 2 * N_GROUPS
    p_in, s_in = refs[:n_pi], refs[n_pi:n_pi + n_si]
    p_out, s_out = refs[n_pi + n_si:n_pi + n_si + n_po], refs[n_pi + n_si + n_po:]
    i = pl.program_id(0)
    _attn_prompt_body(p_in, p_out, (i // UNITS) % tiles, i % UNITS)
    _attn_sample_body(*s_in, *s_out, s_len=s_len)


def _attn_both(qd, kd, vd, batch, seq_len, q8, kn8, vn8, caches_k, caches_v, s_len):
    tiles = seq_len // ATTN_TILE
    n = batch * tiles * UNITS
    assert n == q8.shape[0]
    index = lambda i: (i // (tiles * UNITS), (i // UNITS) % tiles, i % UNITS)
    p_args, p_in, p_shape, p_out = _attn_prompt_specs(qd, kd, vd, batch, seq_len, index)
    s_args, s_in, s_shape, s_out = _attn_sample_specs(q8, kn8, vn8, caches_k, caches_v, s_len)
    res = pl.pallas_call(
        functools.partial(_attn_both_kernel, s_len=s_len, tiles=tiles), grid=(n,), in_specs=p_in + s_in,
        out_specs=p_out + s_out, out_shape=p_shape + s_shape,
        compiler_params=pltpu.CompilerParams(dimension_semantics=("arbitrary",), vmem_limit_bytes=VMEM_LIMIT),
        name="attn_both",
    )(*p_args, *s_args)
    return [r.reshape(-1, r.shape[2]) for r in res[:len(p_shape)]], res[len(p_shape):]


def _merge_stage(x1_ref, a, battn, gm_ref, wg_ref, bg_ref, wao_ref, wout_ref):
    x1 = x1_ref[...]
    d = x1.shape[1]
    h = _rms(x1, gm_ref[...]).astype(BF16)
    gates = jax.nn.sigmoid(jnp.dot(h, wg_ref[...], preferred_element_type=F32) + bg_ref[...])
    b = jnp.dot(battn.astype(BF16), wao_ref[...], preferred_element_type=F32)
    mix = (gates[:, :d] * a + gates[:, d:] * b).astype(BF16)
    return x1 + jnp.dot(mix, wout_ref[...], preferred_element_type=F32)


def _ffn_stage(y, g2_ref, wgu_ref, wo_ref, act_ref):
    return y + 0.5 * _swiglu(_rms(y, g2_ref[...]).astype(BF16), wgu_ref, wo_ref, act_ref)


def _undilate(blk, pt_ref, d):
    if d == 1:
        return blk.astype(F32)
    rows = jnp.concatenate([blk[:, r * GROUP_W:(r + 1) * GROUP_W] for r in range(d)], axis=0)
    pt = pt_ref[...]
    if rows.dtype == BF16:
        return jnp.dot(pt, rows, preferred_element_type=F32)
    hi = rows.astype(BF16)
    rest = rows - hi.astype(F32)
    mid = rest.astype(BF16)
    lo = (rest - mid.astype(F32)).astype(BF16)
    return jnp.dot(pt, hi, preferred_element_type=F32) + jnp.dot(pt, mid, preferred_element_type=F32) \
        + jnp.dot(pt, lo, preferred_element_type=F32)


def _merge_ffn_prompt_kernel(x1_ref, a_ref, o0_ref, o1_ref, o2_ref, l0_ref, l1_ref, l2_ref, pt1_ref, pt2_ref,
                             gm_ref, wg_ref, bg_ref, wao_ref, wout_ref, g2_ref, wgu_ref, wo_ref, out_ref, act_ref):
    pts = (None, pt1_ref, pt2_ref)
    lses = [_undilate(r[...], pts[g], DILATIONS[g]) for g, r in enumerate((l0_ref, l1_ref, l2_ref))]
    outs = [_undilate(r[...], pts[g], DILATIONS[g]) for g, r in enumerate((o0_ref, o1_ref, o2_ref))]
    mx = jnp.maximum(jnp.maximum(lses[0], lses[1]), lses[2])
    num = None
    den = None
    for o, l in zip(outs, lses):
        wgt = jnp.exp(l - mx)
        num = wgt * o if num is None else num + wgt * o
        den = wgt if den is None else den + wgt
    y = _merge_stage(x1_ref, a_ref[...].astype(F32), num / den, gm_ref, wg_ref, bg_ref, wao_ref, wout_ref)
    out_ref[...] = _ffn_stage(y, g2_ref, wgu_ref, wo_ref, act_ref)


def _merge_ffn_sample_kernel(x1_ref, a_ref, b_ref, gm_ref, wg_ref, bg_ref, wco_ref, wao_ref, wout_ref, g2_ref,
                             wgu_ref, wo_ref, out_ref, act_ref):
    a = jnp.dot(a_ref[...].astype(BF16), wco_ref[...], preferred_element_type=F32)
    y = _merge_stage(x1_ref, a, b_ref[...], gm_ref, wg_ref, bg_ref, wao_ref, wout_ref)
    out_ref[...] = _ffn_stage(y, g2_ref, wgu_ref, wo_ref, act_ref)


def _merge_weights(w):
    return [w["mix_norm"], w["w_in_g"], w["b_gate"], w["w_conv_out"], w["w_attn_out"], w["w_out"], w["ffn2_norm"],
            w["ffn2_wgu"], w["ffn2_wo"]]


def _merge_ffn_sample(x1, a_pre, b, w):
    n, d = x1.shape
    tm = TOKEN_TILE
    assert n % tm == 0
    tok = lambda width: pl.BlockSpec((tm, width), lambda i: (i, 0))
    weights = _merge_weights(w)
    return pl.pallas_call(
        _merge_ffn_sample_kernel, grid=(n // tm,),
        in_specs=[tok(d), tok(a_pre.shape[1]), tok(GROUP_W)] + [_resident(a.shape) for a in weights],
        out_specs=tok(d), out_shape=jax.ShapeDtypeStruct((n, d), F32),
        scratch_shapes=[pltpu.VMEM((tm, w["ffn2_wo"].shape[0]), BF16)],
        compiler_params=pltpu.CompilerParams(dimension_semantics=("arbitrary",), vmem_limit_bytes=VMEM_LIMIT),
        name="merge_ffn_sample",
    )(x1, a_pre, b, *weights)


def _merge_ffn_prompt(x1, a, outs, lses, w):
    n, d = x1.shape
    tm = TOKEN_TILE
    assert n % tm == 0
    tok = lambda rows, width: pl.BlockSpec((rows, width), lambda i: (i, 0))
    grp = [tok(tm // dl, dl * GROUP_W) for dl in DILATIONS]
    mw = _merge_weights(w)
    weights = [w["permt1"], w["permt2"]] + [x for x in mw if x is not w["w_conv_out"]]
    return pl.pallas_call(
        _merge_ffn_prompt_kernel, grid=(n // tm,),
        in_specs=[tok(tm, d), tok(tm, d)] + grp + grp + [_resident(x.shape) for x in weights],
        out_specs=tok(tm, d), out_shape=jax.ShapeDtypeStruct((n, d), F32),
        scratch_shapes=[pltpu.VMEM((tm, w["ffn2_wo"].shape[0]), BF16)],
        compiler_params=pltpu.CompilerParams(dimension_semantics=("arbitrary",), vmem_limit_bytes=VMEM_LIMIT),
        name="merge_ffn_prompt",
    )(x1, a, *outs, *lses, *weights)


def _rope_tables(pos):
    half = ROT_DIM // 2
    inv = jnp.float32(ROPE_THETA) ** (-jnp.arange(half, dtype=F32) * (2.0 / ROT_DIM))
    ang = pos.astype(F32)[:, None] * inv[None, :]
    cos, sin = jnp.cos(ang), jnp.sin(ang)
    n = pos.shape[0]
    rest = HEAD_DIM - ROT_DIM
    zh = jnp.zeros((n, half), F32)
    c = jnp.concatenate([cos, cos, jnp.ones((n, rest), F32)], axis=1)
    s1 = jnp.concatenate([zh, sin, jnp.zeros((n, rest), F32)], axis=1)
    s2 = jnp.concatenate([-sin, zh, jnp.zeros((n, rest), F32)], axis=1)
    rep = LANES // HEAD_DIM
    return tuple(jnp.tile(t, (1, rep)) for t in (c, s1, s2))


def _prepare_weights(l, ffn1_norm, ffn1_w_in, ffn1_w_out, mix_norm, w_in, b_gate, q_norm, k_norm, conv_w, conv_b,
                     conv_ln_g, conv_ln_b, w_conv_out, w_attn_out, w_out, ffn2_norm, ffn2_w_in, ffn2_w_out):
    cc = conv_w.shape[2]
    n_a = 2 * cc + 3 * ATTN_W
    lane = jnp.arange(GROUP_W)
    same_head = (lane[:, None] // HEAD_DIM) == (lane[None, :] // HEAD_DIM)
    row = lambda a: a.reshape(1, -1).astype(F32)
    tok = jnp.arange(TOKEN_TILE)

    def perm(d):
        src = (tok % (TOKEN_TILE // d)) * d + tok // (TOKEN_TILE // d)
        return (src[:, None] == tok[None, :]).astype(BF16)

    return {
        "perm1": perm(DILATIONS[1]), "perm2": perm(DILATIONS[2]),
        "permt1": perm(DILATIONS[1]).T, "permt2": perm(DILATIONS[2]).T,
        "ffn1_norm": row(ffn1_norm[l]), "ffn1_wgu": ffn1_w_in[l].astype(BF16), "ffn1_wo": ffn1_w_out[l].astype(BF16),
        "mix_norm": row(mix_norm[l]), "w_in_a": w_in[l][:, :n_a].astype(BF16), "w_in_g": w_in[l][:, n_a:].astype(BF16),
        "b_gate": row(b_gate[l]),
        "head_mean": jnp.where(same_head, 1.0 / HEAD_DIM, 0.0).astype(BF16),
        "q_gain": row(q_norm[l]), "k_gain": row(k_norm[l]),
        "conv_w": jnp.concatenate([conv_w[l], jnp.zeros((CONV_PAD - CONV_WIDTH, cc), F32)], axis=0),
        "conv_b": row(conv_b[l]), "conv_ln_g": row(conv_ln_g[l]), "conv_ln_b": row(conv_ln_b[l]),
        "w_conv_out": w_conv_out[l].astype(BF16), "w_attn_out": w_attn_out[l].astype(BF16),
        "w_out": w_out[l].astype(BF16),
        "ffn2_norm": row(ffn2_norm[l]), "ffn2_wgu": ffn2_w_in[l].astype(BF16), "ffn2_wo": ffn2_w_out[l].astype(BF16),
    }


def kernel(x_prompt, x_sample, cache_k_w128, cache_v_w128, cache_k_w512, cache_v_w512, cache_k_w2048, cache_v_w2048, state_conv, ffn1_norm, ffn1_w_in, ffn1_w_out, mix_norm, w_in, b_gate, q_norm, k_norm, conv_w, conv_b, conv_ln_g, conv_ln_b, w_conv_out, w_attn_out, w_out, ffn2_norm, ffn2_w_in, ffn2_w_out):
    batch, t, d = x_prompt.shape
    db, s_len, _ = x_sample.shape
    depth = w_in.shape[0]
    cc = conv_w.shape[2]
    hd = (HEADS_PER_GROUP, HEAD_DIM)
    cache_k = (cache_k_w128, cache_k_w512, cache_k_w2048)
    cache_v = (cache_v_w128, cache_v_w512, cache_v_w2048)
    for g, wdw in enumerate(WINDOWS):
        assert cache_k[g].shape == (depth, db, wdw, *hd), "sample caches must hold a full window"
        assert t >= wdw
    rope_p = _rope_tables(jnp.arange(t))
    rope_s = _rope_tables(PAST_LEN + jnp.arange(db * s_len) // db)
    to_time_minor = lambda c: jnp.transpose(c, (0, 2, 3, 1))
    from_time_minor = lambda c: jnp.transpose(c, (0, 3, 1, 2))

    xp = x_prompt.reshape(batch * t, d)
    xs = jnp.transpose(x_sample, (1, 0, 2)).reshape(s_len * db, d)
    outs_p = [[] for _ in range(2 * N_GROUPS)]
    outs_s = [[] for _ in range(2 * N_GROUPS)]
    new_conv_p, new_conv_s = [], []
    for l in range(depth):
        w = _prepare_weights(l, ffn1_norm, ffn1_w_in, ffn1_w_out, mix_norm, w_in, b_gate, q_norm, k_norm, conv_w,
                             conv_b, conv_ln_g, conv_ln_b, w_conv_out, w_attn_out, w_out, ffn2_norm, ffn2_w_in,
                             ffn2_w_out)
        x1p, *rest = _ffn_mix_prompt(xp, rope_p, w, t)
        qkv, (a_p, utail), kv_t = rest[:9], rest[9:11], rest[11:]
        x1s, qs, k32s, v32s, us = _ffn_mix_sample(xs, rope_s, w)
        a_s, state_new = _conv_sample(jnp.transpose(state_conv[l], (1, 0, 2)), us.reshape(s_len, db, cc), w)
        rows8 = lambda a: jnp.pad(jnp.transpose(a.reshape(s_len, db, ATTN_W), (1, 0, 2)),
                                  ((0, 0), (0, SAMPLE_ROWS - s_len), (0, 0)))
        ck = [to_time_minor(cache_k[g][l]) for g in range(N_GROUPS)]
        cv = [to_time_minor(cache_v[g][l]) for g in range(N_GROUPS)]
        if batch * (t // ATTN_TILE) * UNITS == db:
            attn_p, (b_s, *new_caches) = _attn_both(qkv[0:3], qkv[3:6], qkv[6:9], batch, t, rows8(qs), rows8(k32s),
                                                    rows8(v32s), ck, cv, s_len)
        else:
            attn_p = _attn_prompt(qkv[0:3], qkv[3:6], qkv[6:9], batch, t)
            b_s, *new_caches = _attn_sample(rows8(qs), rows8(k32s), rows8(v32s), ck, cv, s_len)
        xp = _merge_ffn_prompt(x1p, a_p, attn_p[0::2], attn_p[1::2], w)
        b_s = jnp.transpose(b_s[:, :s_len], (1, 0, 2)).reshape(s_len * db, GROUP_W)
        xs = _merge_ffn_sample(x1s, a_s.reshape(s_len * db, cc), b_s, w)
        new_conv_p.append(utail[:, CONV_PAD - (CONV_WIDTH - 1):, :])
        new_conv_s.append(jnp.transpose(state_new, (1, 0, 2)))
        for i, (g, c) in enumerate([(g, c) for c in range(2) for g in range(N_GROUPS)]):
            outs_p[2 * g + c].append(from_time_minor(kv_t[i].reshape(batch, *hd, WINDOWS[g])))
        for i, nc in enumerate(new_caches):
            outs_s[i].append(from_time_minor(nc))
    y_s = jnp.transpose(xs.reshape(s_len, db, d), (1, 0, 2))
    return (xp.reshape(batch, t, d), y_s, *[jnp.stack(o) for o in outs_p], jnp.stack(new_conv_p),
            *[jnp.stack(o) for o in outs_s], jnp.stack(new_conv_s))
```

```python
import functools

import jax
import jax.numpy as jnp
from jax import lax
from jax.experimental import pallas as pl
from jax.experimental.pallas import tpu as pltpu

F32 = jnp.float32
BF16 = jnp.bfloat16

HEAD_DIM = 64
WINDOWS = (128, 512, 2048)
DILATIONS = (1, 4, 16)
N_GROUPS = 3
HEADS_PER_GROUP = 4
GROUP_W = HEADS_PER_GROUP * HEAD_DIM
ATTN_W = N_GROUPS * GROUP_W
BLK = WINDOWS[0] // DILATIONS[0]
assert all(w // d == BLK for w, d in zip(WINDOWS, DILATIONS))
ATTN_TILE = max(WINDOWS)
UNITS = ATTN_TILE // BLK
ATTN_SCALE = HEAD_DIM ** -0.5
ROT_DIM = HEAD_DIM // 4
ROPE_THETA = 500000.0
CONV_WIDTH = 31
CONV_PAD = 32
NORM_EPS = 1e-6
PAST_LEN = 2048
NEG = -1e30
LANES = 128
SUBLANES = 8
FF_CHUNK = 256
TOKEN_TILE = 256
SAMPLE_ROWS = 8
CONV_SEQS = 16
VMEM_LIMIT = 56 * 1024 * 1024


def _resident(shape):
    nd = len(shape)
    return pl.BlockSpec(shape, lambda *_: (0,) * nd, pipeline_mode=pl.Buffered(1))


def _rms(x, g):
    ms = jnp.mean(x * x, axis=-1, keepdims=True)
    return x * lax.rsqrt(ms + NORM_EPS) * g


def _swiglu(h, wgu_ref, wo_ref, act_ref):
    f = wo_ref.shape[0]
    assert f % FF_CHUNK == 0
    for c in range(f // FF_CHUNK):
        g = jnp.dot(h, wgu_ref[:, c * FF_CHUNK:(c + 1) * FF_CHUNK], preferred_element_type=F32)
        u = jnp.dot(h, wgu_ref[:, f + c * FF_CHUNK:f + (c + 1) * FF_CHUNK], preferred_element_type=F32)
        act_ref[:, c * FF_CHUNK:(c + 1) * FF_CHUNK] = (g * jax.nn.sigmoid(g) * u).astype(BF16)
    return jnp.dot(act_ref[...], wo_ref[...], preferred_element_type=F32)


def _head_norm(y, hm_ref, gain):
    parts = []
    for g in range(N_GROUPS):
        yg = y[:, g * GROUP_W:(g + 1) * GROUP_W]
        ms = jnp.dot((yg * yg).astype(BF16), hm_ref[...], preferred_element_type=F32)
        parts.append(yg * lax.rsqrt(ms + NORM_EPS))
    return jnp.concatenate(parts, axis=1) * gain


def _rope(y, c, s1, s2):
    half = ROT_DIM // 2
    parts = []
    for i in range(y.shape[1] // LANES):
        yc = y[:, i * LANES:(i + 1) * LANES]
        parts.append(yc * c + pltpu.roll(yc, half, 1) * s1 + pltpu.roll(yc, LANES - half, 1) * s2)
    return jnp.concatenate(parts, axis=1)


def _layernorm_silu(y, g, b):
    mu = jnp.mean(y, axis=-1, keepdims=True)
    yc = y - mu
    var = jnp.mean(yc * yc, axis=-1, keepdims=True)
    yn = yc * lax.rsqrt(var + NORM_EPS) * g + b
    return yn * jax.nn.sigmoid(yn)


def _causal_conv(ext, cw_ref, cb_ref, lng_ref, lnb_ref):
    tm = ext.shape[0] - CONV_PAD
    first = CONV_PAD - (CONV_WIDTH - 1)
    rows = 64
    outs = []
    for r0 in range(0, tm, rows):
        acc = None
        base = ext[r0:r0 + rows + CONV_PAD, :]
        for r in range(SUBLANES):
            win = base if r == 0 else pltpu.roll(base, base.shape[0] - r, 0)
            for j in range(CONV_WIDTH):
                if (first + j) % SUBLANES != r:
                    continue
                o = first + j - r
                term = win[o:o + rows, :] * cw_ref[j:j + 1, :]
                acc = term if acc is None else acc + term
        outs.append(_layernorm_silu(acc + cb_ref[...], lng_ref[...], lnb_ref[...]))
    return jnp.concatenate(outs, axis=0)


def _proj_stage(x_ref, g1_ref, wgu_ref, wo_ref, gm_ref, x1_ref, act_ref):
    x = x_ref[...]
    h = _rms(x, g1_ref[...]).astype(BF16)
    x1 = x + 0.5 * _swiglu(h, wgu_ref, wo_ref, act_ref)
    x1_ref[...] = x1
    return _rms(x1, gm_ref[...]).astype(BF16)


def _mix_parts(h2, win_ref, rc_ref, rs1_ref, rs2_ref, hm_ref, qg_ref, kg_ref):
    cc = (win_ref.shape[1] - 3 * ATTN_W) // 2
    proj = lambda lo, hi: jnp.dot(h2, win_ref[:, lo:hi], preferred_element_type=F32)
    zu = proj(0, 2 * cc)
    yield "u", zu[:, :cc] * jax.nn.sigmoid(zu[:, cc:])
    c, s1, s2 = rc_ref[...], rs1_ref[...], rs2_ref[...]
    yield "q", _rope(_head_norm(proj(2 * cc, 2 * cc + ATTN_W), hm_ref, qg_ref[...]), c, s1, s2)
    yield "k", _rope(_head_norm(proj(2 * cc + ATTN_W, 2 * cc + 2 * ATTN_W), hm_ref, kg_ref[...]), c, s1, s2)
    yield "v", proj(2 * cc + 2 * ATTN_W, 2 * cc + 3 * ATTN_W)


def _store_dilated(y, part, perm_refs, out_refs):
    yb = y.astype(BF16)
    lo = part * GROUP_W
    out_refs[0][:, lo:lo + GROUP_W] = yb[:, :GROUP_W]
    for g in range(1, N_GROUPS):
        d = DILATIONS[g]
        rows = y.shape[0] // d
        yp = jnp.dot(perm_refs[g - 1][...], yb[:, g * GROUP_W:(g + 1) * GROUP_W],
                     preferred_element_type=F32).astype(BF16)
        for r in range(d):
            out_refs[g][:, r * ATTN_W + lo:r * ATTN_W + lo + GROUP_W] = yp[r * rows:(r + 1) * rows, :]


def _ffn_mix_prompt_kernel(x_ref, rc_ref, rs1_ref, rs2_ref, g1_ref, wgu_ref, wo_ref, gm_ref, win_ref, hm_ref, qg_ref,
                           kg_ref, p1_ref, p2_ref, cw_ref, cb_ref, lng_ref, lnb_ref, wco_ref,
                           x1_ref, qkv0_ref, qkv1_ref, qkv2_ref, a_ref,
                           utail_ref, kt0_ref, kt1_ref, kt2_ref, vt0_ref, vt1_ref, vt2_ref, act_ref, carry_ref, *,
                           tiles_per_seq):
    tm = x_ref.shape[0]
    i = pl.program_id(0)

    @pl.when(i == 0)
    def _():
        carry_ref[...] = jnp.zeros(carry_ref.shape, F32)

    h2 = _proj_stage(x_ref, g1_ref, wgu_ref, wo_ref, gm_ref, x1_ref, act_ref)
    perms = (p1_ref, p2_ref)
    packed = (qkv0_ref, qkv1_ref, qkv2_ref)
    windows = {"k": (kt0_ref, kt1_ref, kt2_ref), "v": (vt0_ref, vt1_ref, vt2_ref)}
    parts = dict(_mix_parts(h2, win_ref, rc_ref, rs1_ref, rs2_ref, hm_ref, qg_ref, kg_ref))
    u = parts.pop("u")
    for part, (name, y) in enumerate(parts.items()):
        _store_dilated(y, part, perms, packed)
        if name in windows:
            yt = jnp.transpose(y)
            for g, ref in enumerate(windows[name]):
                ref[...] = yt[g * GROUP_W:(g + 1) * GROUP_W, tm - ref.shape[1]:]
    tile = jnp.zeros(carry_ref.shape, jnp.int32) + i % tiles_per_seq
    ext = jnp.concatenate([jnp.where(tile == 0, 0.0, carry_ref[...]), u], axis=0)
    a_pre = _causal_conv(ext, cw_ref, cb_ref, lng_ref, lnb_ref)
    a_ref[...] = jnp.dot(a_pre.astype(BF16), wco_ref[...], preferred_element_type=F32).astype(BF16)
    tail = u[tm - CONV_PAD:, :]
    utail_ref[...] = tail
    carry_ref[...] = tail


def _ffn_mix_sample_kernel(x_ref, rc_ref, rs1_ref, rs2_ref, g1_ref, wgu_ref, wo_ref, gm_ref, win_ref, hm_ref, qg_ref,
                           kg_ref, x1_ref, q_ref, k32_ref, v32_ref, u_ref, act_ref):
    h2 = _proj_stage(x_ref, g1_ref, wgu_ref, wo_ref, gm_ref, x1_ref, act_ref)
    outs = {"u": u_ref, "q": q_ref, "k": k32_ref, "v": v32_ref}
    for name, y in _mix_parts(h2, win_ref, rc_ref, rs1_ref, rs2_ref, hm_ref, qg_ref, kg_ref):
        outs[name][...] = y


def _ffn_mix_weights(w):
    return [w["ffn1_norm"], w["ffn1_wgu"], w["ffn1_wo"], w["mix_norm"], w["w_in_a"], w["head_mean"],
            w["q_gain"], w["k_gain"]]


def _ffn_mix_sample(x, rope, w):
    n, d = x.shape
    tm = TOKEN_TILE
    assert n % tm == 0 and rope[0].shape[0] == n
    cc = w["conv_w"].shape[1]
    tok = lambda width: pl.BlockSpec((tm, width), lambda i: (i, 0))
    weights = _ffn_mix_weights(w)
    widths = [d, ATTN_W, ATTN_W, ATTN_W, cc]
    return pl.pallas_call(
        _ffn_mix_sample_kernel, grid=(n // tm,),
        in_specs=[tok(d)] + [tok(LANES)] * 3 + [_resident(a.shape) for a in weights],
        out_specs=[tok(wd) for wd in widths], out_shape=[jax.ShapeDtypeStruct((n, wd), F32) for wd in widths],
        scratch_shapes=[pltpu.VMEM((tm, w["ffn1_wo"].shape[0]), BF16)],
        compiler_params=pltpu.CompilerParams(dimension_semantics=("arbitrary",), vmem_limit_bytes=VMEM_LIMIT),
        name="ffn_mix_sample",
    )(x, *rope, *weights)


def _ffn_mix_prompt(x, rope, w, seq_len):
    n, d = x.shape
    tm = TOKEN_TILE
    t = seq_len
    assert n % t == 0 and t % tm == 0 and rope[0].shape[0] == t
    assert all(wd % tm == 0 or tm % wd == 0 for wd in WINDOWS)
    tps = t // tm
    nt = n // tm
    nseq = n // t
    cc = w["conv_w"].shape[1]
    weights = _ffn_mix_weights(w) + [w["perm1"], w["perm2"], w["conv_w"], w["conv_b"], w["conv_ln_g"],
                                     w["conv_ln_b"], w["w_conv_out"]]
    tok = lambda rows, width: pl.BlockSpec((rows, width), lambda i: (i, 0))
    rope_spec = pl.BlockSpec((tm, LANES), lambda i: (i % tps, 0))
    in_specs = [tok(tm, d)] + [rope_spec] * 3 + [_resident(a.shape) for a in weights]
    out_shape = [jax.ShapeDtypeStruct((n, d), F32)]
    out_specs = [tok(tm, d)]
    for dl in DILATIONS:
        out_shape.append(jax.ShapeDtypeStruct((n // dl, dl * ATTN_W), BF16))
        out_specs.append(tok(tm // dl, dl * ATTN_W))
    out_shape += [jax.ShapeDtypeStruct((n, d), BF16), jax.ShapeDtypeStruct((nseq, CONV_PAD, cc), F32)]
    out_specs += [tok(tm, d), pl.BlockSpec((None, CONV_PAD, cc), lambda i: (i // tps, 0, 0))]
    for _ in range(2):
        for wd in WINDOWS:
            first = (t - wd) // tm if wd >= tm else tps - 1
            out_shape.append(jax.ShapeDtypeStruct((nseq, GROUP_W, wd), F32))
            out_specs.append(pl.BlockSpec(
                (None, GROUP_W, min(wd, tm)),
                lambda i, first=first: (i // tps, 0, jnp.maximum(i % tps - first, 0))))
    return pl.pallas_call(
        functools.partial(_ffn_mix_prompt_kernel, tiles_per_seq=tps), grid=(nt,), in_specs=in_specs,
        out_specs=out_specs, out_shape=out_shape,
        scratch_shapes=[pltpu.VMEM((tm, w["ffn1_wo"].shape[0]), BF16), pltpu.VMEM((CONV_PAD, cc), F32)],
        compiler_params=pltpu.CompilerParams(dimension_semantics=("arbitrary",), vmem_limit_bytes=VMEM_LIMIT),
        name="ffn_mix_prompt",
    )(x, *rope, *weights)


def _attn_unit(q, kp, kc, vp, vc, first):
    row = lax.broadcasted_iota(jnp.int32, (BLK, BLK), 0)
    col = lax.broadcasted_iota(jnp.int32, (BLK, BLK), 1)
    bias_p = jnp.where(col >= row, 0.0, NEG) + jnp.where(first, NEG, 0.0)
    bias_c = jnp.where(col <= row, 0.0, NEG)
    head = lax.broadcasted_iota(jnp.int32, (1, GROUP_W), 1) // HEAD_DIM
    contract_last = (((1,), (1,)), ((), ()))
    out = jnp.zeros((BLK, GROUP_W), F32)
    lse = jnp.zeros((BLK, GROUP_W), F32)
    zero = jnp.zeros((), BF16)
    for h in range(HEADS_PER_GROUP):
        hm = head == h
        qh = jnp.where(hm, q, zero)
        sp = lax.dot_general(qh, kp, contract_last, preferred_element_type=F32) * ATTN_SCALE + bias_p
        sc = lax.dot_general(qh, kc, contract_last, preferred_element_type=F32) * ATTN_SCALE + bias_c
        m = jnp.max(jnp.maximum(sp, sc), axis=1, keepdims=True)
        pp = jnp.exp(sp - m)
        pc = jnp.exp(sc - m)
        l = jnp.sum(pp, axis=1, keepdims=True) + jnp.sum(pc, axis=1, keepdims=True)
        o = jnp.dot(pp.astype(BF16), jnp.where(hm, vp, zero), preferred_element_type=F32) \
            + jnp.dot(pc.astype(BF16), jnp.where(hm, vc, zero), preferred_element_type=F32)
        out = out + o / l
        lse = lse + jnp.where(hm, m + jnp.log(l), 0.0)
    return out, lse


def _attn_prompt_body(ins, outs, t, u):
    part = lambda ref, n: ref[:, n * GROUP_W:(n + 1) * GROUP_W]
    for g in range(N_GROUPS):
        own_ref, prev_ref = ins[2 * g:2 * g + 2]
        d = DILATIONS[g]
        first = (t * (UNITS // d) + u // d) == 0
        o, lse = _attn_unit(part(own_ref, 0), part(prev_ref, 1), part(own_ref, 1), part(prev_ref, 2),
                            part(own_ref, 2), first)
        outs[2 * g][...] = o.astype(BF16)
        outs[2 * g + 1][...] = lse


def _attn_prompt_kernel(*refs):
    _attn_prompt_body(refs[:2 * N_GROUPS], refs[2 * N_GROUPS:], pl.program_id(1), pl.program_id(2))


def _attn_prompt_specs(qkv, batch, seq_len, index):
    assert seq_len % ATTN_TILE == 0
    args, in_specs, out_shape, out_specs = [], [], [], []
    for g in range(N_GROUPS):
        d = DILATIONS[g]
        nb = UNITS // d

        def cur(*i, d=d, nb=nb):
            b, t, u = index(*i)
            return (b, t * nb + u // d, u % d)

        def prev(*i, d=d, nb=nb):
            b, t, u = index(*i)
            return (b, jnp.maximum(t * nb + u // d - 1, 0), u % d)

        packed = qkv[g].reshape(batch, seq_len // d, d * ATTN_W)
        blk = lambda im, width: pl.BlockSpec((None, BLK, width), im)
        args += [packed, packed]
        in_specs += [blk(cur, ATTN_W), blk(prev, ATTN_W)]
        out_shape += [jax.ShapeDtypeStruct((batch, seq_len // d, d * GROUP_W), BF16),
                      jax.ShapeDtypeStruct((batch, seq_len // d, d * GROUP_W), F32)]
        out_specs += [blk(cur, GROUP_W), blk(cur, GROUP_W)]
    return args, in_specs, out_shape, out_specs


def _attn_prompt(qkv, batch, seq_len):
    args, in_specs, out_shape, out_specs = _attn_prompt_specs(qkv, batch, seq_len, lambda b, t, u: (b, t, u))
    res = pl.pallas_call(
        _attn_prompt_kernel, grid=(batch, seq_len // ATTN_TILE, UNITS), in_specs=in_specs, out_specs=out_specs,
        out_shape=out_shape,
        compiler_params=pltpu.CompilerParams(dimension_semantics=("arbitrary",) * 3),
        name="attn_prompt",
    )(*args)
    return [r.reshape(-1, r.shape[2]) for r in res]


def _conv_sample_kernel(state_ref, u_ref, cw_ref, cb_ref, lng_ref, lnb_ref, a_ref, new_state_ref):
    hist = state_ref.shape[0]
    s_len = u_ref.shape[0]
    row = lambda t: state_ref[t] if t < hist else u_ref[t - hist]
    for s in range(s_len):
        acc = None
        for j in range(CONV_WIDTH):
            term = row(s + j) * cw_ref[j:j + 1, :]
            acc = term if acc is None else acc + term
        a_ref[s] = _layernorm_silu(acc + cb_ref[...], lng_ref[...], lnb_ref[...])
    for t in range(hist):
        new_state_ref[t] = row(t + s_len)


def _conv_sample(state_t, u_t, w):
    hist, db, cc = state_t.shape
    s_len = u_t.shape[0]
    assert hist == CONV_WIDTH - 1 and db % CONV_SEQS == 0
    conv_w = [w["conv_w"], w["conv_b"], w["conv_ln_g"], w["conv_ln_b"]]
    slab = lambda rows: pl.BlockSpec((rows, CONV_SEQS, cc), lambda i: (0, i, 0))
    return pl.pallas_call(
        _conv_sample_kernel, grid=(db // CONV_SEQS,),
        in_specs=[slab(hist), slab(s_len)] + [_resident(a.shape) for a in conv_w],
        out_specs=[slab(s_len), slab(hist)],
        out_shape=[jax.ShapeDtypeStruct((s_len, db, cc), F32), jax.ShapeDtypeStruct((hist, db, cc), F32)],
        compiler_params=pltpu.CompilerParams(dimension_semantics=("arbitrary",)),
        name="conv_sample",
    )(state_t, u_t, *conv_w)


def _sample_bias(width, d, s_len):
    s = lax.broadcasted_iota(jnp.int32, (SAMPLE_ROWS, width), 0)
    t = lax.broadcasted_iota(jnp.int32, (SAMPLE_ROWS, width), 1)
    same_residue = jnp.bitwise_and(s - t, d - 1) == 0
    live = s < s_len
    old = jnp.where(live & same_residue & (t >= s), 0.0, NEG)
    sn = s[:, :LANES]
    tn = t[:, :LANES]
    new = jnp.where(live[:, :LANES] & same_residue[:, :LANES] & (tn <= sn), 0.0, NEG)
    return old, new


def _attn_sample_kernel(*refs, s_len):
    _attn_sample_body(*refs, s_len=s_len)


def _attn_sample_body(q_ref, kn_ref, vn_ref, kc0_ref, vc0_ref, kc1_ref, vc1_ref, kc2_ref, vc2_ref,
                      b_ref, nk0_ref, nv0_ref, nk1_ref, nv1_ref, nk2_ref, nv2_ref, *, s_len):
    kc_refs = (kc0_ref, kc1_ref, kc2_ref)
    vc_refs = (vc0_ref, vc1_ref, vc2_ref)
    nk_refs = (nk0_ref, nk1_ref, nk2_ref)
    nv_refs = (nv0_ref, nv1_ref, nv2_ref)
    q = q_ref[...]
    pad = jnp.zeros((LANES - SAMPLE_ROWS, ATTN_W), F32)
    knt = jnp.transpose(jnp.concatenate([kn_ref[...], pad], axis=0))
    vnt = jnp.transpose(jnp.concatenate([vn_ref[...], pad], axis=0))
    biases = [_sample_bias(WINDOWS[g], DILATIONS[g], s_len) for g in range(N_GROUPS)]
    lane = lax.broadcasted_iota(jnp.int32, (1, LANES), 1)
    keep = lane < LANES - s_len
    nt = (((1,), (1,)), ((), ()))

    def shifted(old, new_cols, out_ref, h):
        w = old.shape[1]
        cols = [old[:, j * LANES:(j + 1) * LANES] for j in range(w // LANES)] + [new_cols]
        rolled = [pltpu.roll(c, LANES - s_len, 1) for c in cols]
        for j in range(w // LANES):
            out_ref[h, :, j * LANES:(j + 1) * LANES] = jnp.where(keep, rolled[j], rolled[j + 1])

    heads = range(HEADS_PER_GROUP)
    lanes_of = lambda g, h: slice((g * HEADS_PER_GROUP + h) * HEAD_DIM, (g * HEADS_PER_GROUP + h + 1) * HEAD_DIM)
    s_old = [[None] * N_GROUPS for _ in heads]
    s_new = [[None] * N_GROUPS for _ in heads]
    for h in heads:
        for g in range(N_GROUPS):
            qh = q[:, lanes_of(g, h)].astype(BF16)
            k_old = kc_refs[g][h]
            k_new = knt[lanes_of(g, h), :]
            s_old[h][g] = jnp.dot(qh, k_old.astype(BF16), preferred_element_type=F32) * ATTN_SCALE + biases[g][0]
            s_new[h][g] = jnp.dot(qh, k_new.astype(BF16), preferred_element_type=F32) * ATTN_SCALE + biases[g][1]
            shifted(k_old, k_new, nk_refs[g], h)
    p_old = [[None] * N_GROUPS for _ in heads]
    p_new = [[None] * N_GROUPS for _ in heads]
    dens = []
    for h in heads:
        m = None
        for a in s_old[h] + s_new[h]:
            am = jnp.max(a, axis=1, keepdims=True)
            m = am if m is None else jnp.maximum(m, am)
        den = jnp.zeros((SAMPLE_ROWS, 1), F32)
        for g in range(N_GROUPS):
            p_old[h][g] = jnp.exp(s_old[h][g] - m)
            p_new[h][g] = jnp.exp(s_new[h][g] - m)
            den = den + jnp.sum(p_old[h][g], axis=1, keepdims=True) + jnp.sum(p_new[h][g], axis=1, keepdims=True)
        dens.append(den)
    for h in heads:
        num = jnp.zeros((SAMPLE_ROWS, HEAD_DIM), F32)
        for g in range(N_GROUPS):
            v_old = vc_refs[g][h]
            v_new = vnt[lanes_of(g, h), :]
            num = num + lax.dot_general(p_old[h][g].astype(BF16), v_old.astype(BF16), nt,
                                        preferred_element_type=F32) \
                + lax.dot_general(p_new[h][g].astype(BF16), v_new.astype(BF16), nt, preferred_element_type=F32)
            shifted(v_old, v_new, nv_refs[g], h)
        b_ref[:, h * HEAD_DIM:(h + 1) * HEAD_DIM] = num / dens[h]


def _attn_sample_specs(q8, kn8, vn8, caches_k, caches_v, s_len):
    db = q8.shape[0]
    assert q8.shape[1] == SAMPLE_ROWS and s_len <= min(SAMPLE_ROWS, DILATIONS[1])
    tok = pl.BlockSpec((None, SAMPLE_ROWS, ATTN_W), lambda b: (b, 0, 0))
    args, in_specs, out_shape, out_specs = [q8, kn8, vn8], [tok] * 3, [], []
    out_shape.append(jax.ShapeDtypeStruct((db, SAMPLE_ROWS, GROUP_W), F32))
    out_specs.append(pl.BlockSpec((None, SAMPLE_ROWS, GROUP_W), lambda b: (b, 0, 0)))
    for g in range(N_GROUPS):
        shape = (db, HEADS_PER_GROUP, HEAD_DIM, WINDOWS[g])
        spec = pl.BlockSpec((None,) + shape[1:], lambda b: (b, 0, 0, 0))
        for c in (caches_k[g], caches_v[g]):
            assert c.shape == shape
            args.append(c)
            in_specs.append(spec)
            out_shape.append(jax.ShapeDtypeStruct(shape, F32))
            out_specs.append(spec)
    return args, in_specs, out_shape, out_specs


def _attn_sample(q8, kn8, vn8, caches_k, caches_v, s_len):
    db = q8.shape[0]
    args, in_specs, out_shape, out_specs = _attn_sample_specs(q8, kn8, vn8, caches_k, caches_v, s_len)
    return pl.pallas_call(
        functools.partial(_attn_sample_kernel, s_len=s_len), grid=(db,), in_specs=in_specs, out_specs=out_specs,
        out_shape=out_shape,
        compiler_params=pltpu.CompilerParams(dimension_semantics=("arbitrary",), vmem_limit_bytes=VMEM_LIMIT),
        name="attn_sample",
    )(*args)


def _attn_both_kernel(*refs, s_len, tiles):
    n_pi, n_si = 2 * N_GROUPS, 3 + 2 * N_GROUPS
    n_po = 2 * N_GROUPS
    p_in, s_in = refs[:n_pi], refs[n_pi:n_pi + n_si]
    p_out, s_out = refs[n_pi + n_si:n_pi + n_si + n_po], refs[n_pi + n_si + n_po:]
    i = pl.program_id(0)
    _attn_prompt_body(p_in, p_out, (i // UNITS) % tiles, i % UNITS)
    _attn_sample_body(*s_in, *s_out, s_len=s_len)


def _attn_both(qkv, batch, seq_len, q8, kn8, vn8, caches_k, caches_v, s_len):
    tiles = seq_len // ATTN_TILE
    n = batch * tiles * UNITS
    assert n == q8.shape[0]
    index = lambda i: (i // (tiles * UNITS), (i // UNITS) % tiles, i % UNITS)
    p_args, p_in, p_shape, p_out = _attn_prompt_specs(qkv, batch, seq_len, index)
    s_args, s_in, s_shape, s_out = _attn_sample_specs(q8, kn8, vn8, caches_k, caches_v, s_len)
    res = pl.pallas_call(
        functools.partial(_attn_both_kernel, s_len=s_len, tiles=tiles), grid=(n,), in_specs=p_in + s_in,
        out_specs=p_out + s_out, out_shape=p_shape + s_shape,
        compiler_params=pltpu.CompilerParams(dimension_semantics=("arbitrary",), vmem_limit_bytes=VMEM_LIMIT),
        name="attn_both",
    )(*p_args, *s_args)
    return [r.reshape(-1, r.shape[2]) for r in res[:len(p_shape)]], res[len(p_shape):]


def _merge_stage(x1_ref, a, battn, gm_ref, wg_ref, bg_ref, wao_ref, wout_ref):
    x1 = x1_ref[...]
    d = x1.shape[1]
    h = _rms(x1, gm_ref[...]).astype(BF16)
    gates = jax.nn.sigmoid(jnp.dot(h, wg_ref[...], preferred_element_type=F32) + bg_ref[...])
    b = jnp.dot(battn.astype(BF16), wao_ref[...], preferred_element_type=F32)
    mix = (gates[:, :d] * a + gates[:, d:] * b).astype(BF16)
    return x1 + jnp.dot(mix, wout_ref[...], preferred_element_type=F32)


def _ffn_stage(y, g2_ref, wgu_ref, wo_ref, act_ref):
    return y + 0.5 * _swiglu(_rms(y, g2_ref[...]).astype(BF16), wgu_ref, wo_ref, act_ref)


def _undilate(blk, pt_ref, d):
    if d == 1:
        return blk.astype(F32)
    rows = jnp.concatenate([blk[:, r * GROUP_W:(r + 1) * GROUP_W] for r in range(d)], axis=0)
    pt = pt_ref[...]
    if rows.dtype == BF16:
        return jnp.dot(pt, rows, preferred_element_type=F32)
    hi = rows.astype(BF16)
    rest = rows - hi.astype(F32)
    mid = rest.astype(BF16)
    lo = (rest - mid.astype(F32)).astype(BF16)
    return jnp.dot(pt, hi, preferred_element_type=F32) + jnp.dot(pt, mid, preferred_element_type=F32) \
        + jnp.dot(pt, lo, preferred_element_type=F32)


def _merge_ffn_prompt_kernel(x1_ref, a_ref, o0_ref, o1_ref, o2_ref, l0_ref, l1_ref, l2_ref, pt1_ref, pt2_ref,
                             gm_ref, wg_ref, bg_ref, wao_ref, wout_ref, g2_ref, wgu_ref, wo_ref, out_ref, act_ref):
    pts = (None, pt1_ref, pt2_ref)
    lses = [_undilate(r[...], pts[g], DILATIONS[g]) for g, r in enumerate((l0_ref, l1_ref, l2_ref))]
    outs = [_undilate(r[...], pts[g], DILATIONS[g]) for g, r in enumerate((o0_ref, o1_ref, o2_ref))]
    mx = jnp.maximum(jnp.maximum(lses[0], lses[1]), lses[2])
    num = None
    den = None
    for o, l in zip(outs, lses):
        wgt = jnp.exp(l - mx)
        num = wgt * o if num is None else num + wgt * o
        den = wgt if den is None else den + wgt
    y = _merge_stage(x1_ref, a_ref[...].astype(F32), num / den, gm_ref, wg_ref, bg_ref, wao_ref, wout_ref)
    out_ref[...] = _ffn_stage(y, g2_ref, wgu_ref, wo_ref, act_ref)


def _merge_ffn_sample_kernel(x1_ref, a_ref, b_ref, gm_ref, wg_ref, bg_ref, wco_ref, wao_ref, wout_ref, g2_ref,
                             wgu_ref, wo_ref, out_ref, act_ref):
    a = jnp.dot(a_ref[...].astype(BF16), wco_ref[...], preferred_element_type=F32)
    y = _merge_stage(x1_ref, a, b_ref[...], gm_ref, wg_ref, bg_ref, wao_ref, wout_ref)
    out_ref[...] = _ffn_stage(y, g2_ref, wgu_ref, wo_ref, act_ref)


def _merge_weights(w):
    return [w["mix_norm"], w["w_in_g"], w["b_gate"], w["w_conv_out"], w["w_attn_out"], w["w_out"], w["ffn2_norm"],
            w["ffn2_wgu"], w["ffn2_wo"]]


def _merge_ffn_sample(x1, a_pre, b, w):
    n, d = x1.shape
    tm = TOKEN_TILE
    assert n % tm == 0
    tok = lambda width: pl.BlockSpec((tm, width), lambda i: (i, 0))
    weights = _merge_weights(w)
    return pl.pallas_call(
        _merge_ffn_sample_kernel, grid=(n // tm,),
        in_specs=[tok(d), tok(a_pre.shape[1]), tok(GROUP_W)] + [_resident(a.shape) for a in weights],
        out_specs=tok(d), out_shape=jax.ShapeDtypeStruct((n, d), F32),
        scratch_shapes=[pltpu.VMEM((tm, w["ffn2_wo"].shape[0]), BF16)],
        compiler_params=pltpu.CompilerParams(dimension_semantics=("arbitrary",), vmem_limit_bytes=VMEM_LIMIT),
        name="merge_ffn_sample",
    )(x1, a_pre, b, *weights)


def _merge_ffn_prompt(x1, a, outs, lses, w):
    n, d = x1.shape
    tm = TOKEN_TILE
    assert n % tm == 0
    tok = lambda rows, width: pl.BlockSpec((rows, width), lambda i: (i, 0))
    grp = [tok(tm // dl, dl * GROUP_W) for dl in DILATIONS]
    mw = _merge_weights(w)
    weights = [w["permt1"], w["permt2"]] + [x for x in mw if x is not w["w_conv_out"]]
    return pl.pallas_call(
        _merge_ffn_prompt_kernel, grid=(n // tm,),
        in_specs=[tok(tm, d), tok(tm, d)] + grp + grp + [_resident(x.shape) for x in weights],
        out_specs=tok(tm, d), out_shape=jax.ShapeDtypeStruct((n, d), F32),
        scratch_shapes=[pltpu.VMEM((tm, w["ffn2_wo"].shape[0]), BF16)],
        compiler_params=pltpu.CompilerParams(dimension_semantics=("arbitrary",), vmem_limit_bytes=VMEM_LIMIT),
        name="merge_ffn_prompt",
    )(x1, a, *outs, *lses, *weights)


def _rope_tables(pos):
    half = ROT_DIM // 2
    inv = jnp.float32(ROPE_THETA) ** (-jnp.arange(half, dtype=F32) * (2.0 / ROT_DIM))
    ang = pos.astype(F32)[:, None] * inv[None, :]
    cos, sin = jnp.cos(ang), jnp.sin(ang)
    n = pos.shape[0]
    rest = HEAD_DIM - ROT_DIM
    zh = jnp.zeros((n, half), F32)
    c = jnp.concatenate([cos, cos, jnp.ones((n, rest), F32)], axis=1)
    s1 = jnp.concatenate([zh, sin, jnp.zeros((n, rest), F32)], axis=1)
    s2 = jnp.concatenate([-sin, zh, jnp.zeros((n, rest), F32)], axis=1)
    rep = LANES // HEAD_DIM
    return tuple(jnp.tile(t, (1, rep)) for t in (c, s1, s2))


def _prepare_weights(l, ffn1_norm, ffn1_w_in, ffn1_w_out, mix_norm, w_in, b_gate, q_norm, k_norm, conv_w, conv_b,
                     conv_ln_g, conv_ln_b, w_conv_out, w_attn_out, w_out, ffn2_norm, ffn2_w_in, ffn2_w_out):
    cc = conv_w.shape[2]
    n_a = 2 * cc + 3 * ATTN_W
    lane = jnp.arange(GROUP_W)
    same_head = (lane[:, None] // HEAD_DIM) == (lane[None, :] // HEAD_DIM)
    row = lambda a: a.reshape(1, -1).astype(F32)
    tok = jnp.arange(TOKEN_TILE)

    def perm(d):
        src = (tok % (TOKEN_TILE // d)) * d + tok // (TOKEN_TILE // d)
        return (src[:, None] == tok[None, :]).astype(BF16)

    return {
        "perm1": perm(DILATIONS[1]), "perm2": perm(DILATIONS[2]),
        "permt1": perm(DILATIONS[1]).T, "permt2": perm(DILATIONS[2]).T,
        "ffn1_norm": row(ffn1_norm[l]), "ffn1_wgu": ffn1_w_in[l].astype(BF16), "ffn1_wo": ffn1_w_out[l].astype(BF16),
        "mix_norm": row(mix_norm[l]), "w_in_a": w_in[l][:, :n_a].astype(BF16), "w_in_g": w_in[l][:, n_a:].astype(BF16),
        "b_gate": row(b_gate[l]),
        "head_mean": jnp.where(same_head, 1.0 / HEAD_DIM, 0.0).astype(BF16),
        "q_gain": row(q_norm[l]), "k_gain": row(k_norm[l]),
        "conv_w": jnp.concatenate([conv_w[l], jnp.zeros((CONV_PAD - CONV_WIDTH, cc), F32)], axis=0),
        "conv_b": row(conv_b[l]), "conv_ln_g": row(conv_ln_g[l]), "conv_ln_b": row(conv_ln_b[l]),
        "w_conv_out": w_conv_out[l].astype(BF16), "w_attn_out": w_attn_out[l].astype(BF16),
        "w_out": w_out[l].astype(BF16),
        "ffn2_norm": row(ffn2_norm[l]), "ffn2_wgu": ffn2_w_in[l].astype(BF16), "ffn2_wo": ffn2_w_out[l].astype(BF16),
    }


def kernel(x_prompt, x_sample, cache_k_w128, cache_v_w128, cache_k_w512, cache_v_w512, cache_k_w2048, cache_v_w2048, state_conv, ffn1_norm, ffn1_w_in, ffn1_w_out, mix_norm, w_in, b_gate, q_norm, k_norm, conv_w, conv_b, conv_ln_g, conv_ln_b, w_conv_out, w_attn_out, w_out, ffn2_norm, ffn2_w_in, ffn2_w_out):
    batch, t, d = x_prompt.shape
    db, s_len, _ = x_sample.shape
    depth = w_in.shape[0]
    cc = conv_w.shape[2]
    hd = (HEADS_PER_GROUP, HEAD_DIM)
    cache_k = (cache_k_w128, cache_k_w512, cache_k_w2048)
    cache_v = (cache_v_w128, cache_v_w512, cache_v_w2048)
    for g, wdw in enumerate(WINDOWS):
        assert cache_k[g].shape == (depth, db, wdw, *hd), "sample caches must hold a full window"
        assert t >= wdw
    rope_p = _rope_tables(jnp.arange(t))
    rope_s = _rope_tables(PAST_LEN + jnp.arange(db * s_len) // db)
    to_time_minor = lambda c: jnp.transpose(c, (0, 2, 3, 1))
    from_time_minor = lambda c: jnp.transpose(c, (0, 3, 1, 2))

    xp = x_prompt.reshape(batch * t, d)
    xs = jnp.transpose(x_sample, (1, 0, 2)).reshape(s_len * db, d)
    outs_p = [[] for _ in range(2 * N_GROUPS)]
    outs_s = [[] for _ in range(2 * N_GROUPS)]
    new_conv_p, new_conv_s = [], []
    for l in range(depth):
        w = _prepare_weights(l, ffn1_norm, ffn1_w_in, ffn1_w_out, mix_norm, w_in, b_gate, q_norm, k_norm, conv_w,
                             conv_b, conv_ln_g, conv_ln_b, w_conv_out, w_attn_out, w_out, ffn2_norm, ffn2_w_in,
                             ffn2_w_out)
        x1p, *rest = _ffn_mix_prompt(xp, rope_p, w, t)
        qkv, (a_p, utail), kv_t = rest[:3], rest[3:5], rest[5:]
        x1s, qs, k32s, v32s, us = _ffn_mix_sample(xs, rope_s, w)
        a_s, state_new = _conv_sample(jnp.transpose(state_conv[l], (1, 0, 2)), us.reshape(s_len, db, cc), w)
        rows8 = lambda a: jnp.pad(jnp.transpose(a.reshape(s_len, db, ATTN_W), (1, 0, 2)),
                                  ((0, 0), (0, SAMPLE_ROWS - s_len), (0, 0)))
        ck = [to_time_minor(cache_k[g][l]) for g in range(N_GROUPS)]
        cv = [to_time_minor(cache_v[g][l]) for g in range(N_GROUPS)]
        if batch * (t // ATTN_TILE) * UNITS == db:
            attn_p, (b_s, *new_caches) = _attn_both(qkv, batch, t, rows8(qs), rows8(k32s), rows8(v32s), ck, cv,
                                                    s_len)
        else:
            attn_p = _attn_prompt(qkv, batch, t)
            b_s, *new_caches = _attn_sample(rows8(qs), rows8(k32s), rows8(v32s), ck, cv, s_len)
        xp = _merge_ffn_prompt(x1p, a_p, attn_p[0::2], attn_p[1::2], w)
        b_s = jnp.transpose(b_s[:, :s_len], (1, 0, 2)).reshape(s_len * db, GROUP_W)
        xs = _merge_ffn_sample(x1s, a_s.reshape(s_len * db, cc), b_s, w)
        new_conv_p.append(utail[:, CONV_PAD - (CONV_WIDTH - 1):, :])
        new_conv_s.append(jnp.transpose(state_new, (1, 0, 2)))
        for i, (g, c) in enumerate([(g, c) for c in range(2) for g in range(N_GROUPS)]):
            outs_p[2 * g + c].append(from_time_minor(kv_t[i].reshape(batch, *hd, WINDOWS[g])))
        for i, nc in enumerate(new_caches):
            outs_s[i].append(from_time_minor(nc))
    y_s = jnp.transpose(xs.reshape(s_len, db, d), (1, 0, 2))
    return (xp.reshape(batch, t, d), y_s, *[jnp.stack(o) for o in outs_p], jnp.stack(new_conv_p),
            *[jnp.stack(o) for o in outs_s], jnp.stack(new_conv_s))
```

```python
import functools

import jax
import jax.numpy as jnp
from jax import lax
from jax.experimental import pallas as pl
from jax.experimental.pallas import tpu as pltpu

F32 = jnp.float32
BF16 = jnp.bfloat16

HEAD_DIM = 64
WINDOWS = (128, 512, 2048)
DILATIONS = (1, 4, 16)
N_GROUPS = 3
HEADS_PER_GROUP = 4
GROUP_W = HEADS_PER_GROUP * HEAD_DIM
ATTN_W = N_GROUPS * GROUP_W
BLK = WINDOWS[0] // DILATIONS[0]
assert all(w // d == BLK for w, d in zip(WINDOWS, DILATIONS))
ATTN_TILE = max(WINDOWS)
UNITS = ATTN_TILE // BLK
ATTN_SCALE = HEAD_DIM ** -0.5
ROT_DIM = HEAD_DIM // 4
ROPE_THETA = 500000.0
CONV_WIDTH = 31
CONV_PAD = 32
NORM_EPS = 1e-6
PAST_LEN = 2048
NEG = -1e30
LANES = 128
SUBLANES = 8
FF_CHUNK = 256
TOKEN_TILE = 256
MERGE_TILE = 512
SAMPLE_ROWS = 8
CONV_SEQS = 16
VMEM_LIMIT = 56 * 1024 * 1024


def _resident(shape):
    nd = len(shape)
    return pl.BlockSpec(shape, lambda *_: (0,) * nd, pipeline_mode=pl.Buffered(1))


def _rms(x, g):
    ms = jnp.mean(x * x, axis=-1, keepdims=True)
    return x * lax.rsqrt(ms + NORM_EPS) * g


def _swiglu(h, wgu_ref, wo_ref, act_ref):
    f = wo_ref.shape[0]
    assert f % FF_CHUNK == 0
    for c in range(f // FF_CHUNK):
        g = jnp.dot(h, wgu_ref[:, c * FF_CHUNK:(c + 1) * FF_CHUNK], preferred_element_type=F32)
        u = jnp.dot(h, wgu_ref[:, f + c * FF_CHUNK:f + (c + 1) * FF_CHUNK], preferred_element_type=F32)
        act_ref[:, c * FF_CHUNK:(c + 1) * FF_CHUNK] = (g * jax.nn.sigmoid(g) * u).astype(BF16)
    return jnp.dot(act_ref[...], wo_ref[...], preferred_element_type=F32)


def _head_norm(y, hm_ref, gain):
    parts = []
    for g in range(N_GROUPS):
        yg = y[:, g * GROUP_W:(g + 1) * GROUP_W]
        ms = jnp.dot((yg * yg).astype(BF16), hm_ref[...], preferred_element_type=F32)
        parts.append(yg * lax.rsqrt(ms + NORM_EPS))
    return jnp.concatenate(parts, axis=1) * gain


def _rope(y, c, s1, s2):
    half = ROT_DIM // 2
    parts = []
    for i in range(y.shape[1] // LANES):
        yc = y[:, i * LANES:(i + 1) * LANES]
        parts.append(yc * c + pltpu.roll(yc, half, 1) * s1 + pltpu.roll(yc, LANES - half, 1) * s2)
    return jnp.concatenate(parts, axis=1)


def _layernorm_silu(y, g, b):
    mu = jnp.mean(y, axis=-1, keepdims=True)
    yc = y - mu
    var = jnp.mean(yc * yc, axis=-1, keepdims=True)
    yn = yc * lax.rsqrt(var + NORM_EPS) * g + b
    return yn * jax.nn.sigmoid(yn)


def _causal_conv(ext, cw_ref, cb_ref, lng_ref, lnb_ref):
    tm = ext.shape[0] - CONV_PAD
    first = CONV_PAD - (CONV_WIDTH - 1)
    rows = 64
    outs = []
    for r0 in range(0, tm, rows):
        acc = None
        base = ext[r0:r0 + rows + CONV_PAD, :]
        for r in range(SUBLANES):
            win = base if r == 0 else pltpu.roll(base, base.shape[0] - r, 0)
            for j in range(CONV_WIDTH):
                if (first + j) % SUBLANES != r:
                    continue
                o = first + j - r
                term = win[o:o + rows, :] * cw_ref[j:j + 1, :]
                acc = term if acc is None else acc + term
        outs.append(_layernorm_silu(acc + cb_ref[...], lng_ref[...], lnb_ref[...]))
    return jnp.concatenate(outs, axis=0)


def _proj_stage(x_ref, g1_ref, wgu_ref, wo_ref, gm_ref, x1_ref, act_ref):
    x = x_ref[...]
    h = _rms(x, g1_ref[...]).astype(BF16)
    x1 = x + 0.5 * _swiglu(h, wgu_ref, wo_ref, act_ref)
    x1_ref[...] = x1
    return _rms(x1, gm_ref[...]).astype(BF16)


def _mix_parts(h2, win_ref, rc_ref, rs1_ref, rs2_ref, hm_ref, qg_ref, kg_ref):
    cc = (win_ref.shape[1] - 3 * ATTN_W) // 2
    proj = lambda lo, hi: jnp.dot(h2, win_ref[:, lo:hi], preferred_element_type=F32)
    zu = proj(0, 2 * cc)
    zq = proj(2 * cc, 2 * cc + ATTN_W)
    zk = proj(2 * cc + ATTN_W, 2 * cc + 2 * ATTN_W)
    zv = proj(2 * cc + 2 * ATTN_W, 2 * cc + 3 * ATTN_W)
    yield "u", zu[:, :cc] * jax.nn.sigmoid(zu[:, cc:])
    c, s1, s2 = rc_ref[...], rs1_ref[...], rs2_ref[...]
    yield "q", _rope(_head_norm(zq, hm_ref, qg_ref[...]), c, s1, s2)
    yield "k", _rope(_head_norm(zk, hm_ref, kg_ref[...]), c, s1, s2)
    yield "v", zv


def _store_dilated(y, perm_refs, out_refs):
    yb = y.astype(BF16)
    out_refs[0][...] = yb[:, :GROUP_W]
    for g in range(1, N_GROUPS):
        d = DILATIONS[g]
        rows = y.shape[0] // d
        yp = jnp.dot(perm_refs[g - 1][...], yb[:, g * GROUP_W:(g + 1) * GROUP_W],
                     preferred_element_type=F32).astype(BF16)
        for r in range(d):
            out_refs[g][:, r * GROUP_W:(r + 1) * GROUP_W] = yp[r * rows:(r + 1) * rows, :]


def _ffn_mix_prompt_kernel(x_ref, rc_ref, rs1_ref, rs2_ref, g1_ref, wgu_ref, wo_ref, gm_ref, win_ref, hm_ref, qg_ref,
                           kg_ref, p1_ref, p2_ref, cw_ref, cb_ref, lng_ref, lnb_ref, wco_ref,
                           x1_ref, q0_ref, q1_ref, q2_ref, k0_ref, k1_ref, k2_ref, v0_ref, v1_ref, v2_ref, a_ref,
                           utail_ref, kt0_ref, kt1_ref, kt2_ref, vt0_ref, vt1_ref, vt2_ref, act_ref, carry_ref, *,
                           tiles_per_seq):
    tm = x_ref.shape[0]
    i = pl.program_id(0)

    @pl.when(i == 0)
    def _():
        carry_ref[...] = jnp.zeros(carry_ref.shape, F32)

    h2 = _proj_stage(x_ref, g1_ref, wgu_ref, wo_ref, gm_ref, x1_ref, act_ref)
    perms = (p1_ref, p2_ref)
    dilated = {"q": (q0_ref, q1_ref, q2_ref), "k": (k0_ref, k1_ref, k2_ref), "v": (v0_ref, v1_ref, v2_ref)}
    windows = {"k": (kt0_ref, kt1_ref, kt2_ref), "v": (vt0_ref, vt1_ref, vt2_ref)}
    parts = dict(_mix_parts(h2, win_ref, rc_ref, rs1_ref, rs2_ref, hm_ref, qg_ref, kg_ref))
    u = parts.pop("u")
    for name, y in parts.items():
        _store_dilated(y, perms, dilated[name])
        if name in windows:
            yt = jnp.transpose(y)
            for g, ref in enumerate(windows[name]):
                ref[...] = yt[g * GROUP_W:(g + 1) * GROUP_W, tm - ref.shape[1]:]
    tile = jnp.zeros(carry_ref.shape, jnp.int32) + i % tiles_per_seq
    ext = jnp.concatenate([jnp.where(tile == 0, 0.0, carry_ref[...]), u], axis=0)
    a_pre = _causal_conv(ext, cw_ref, cb_ref, lng_ref, lnb_ref)
    a_ref[...] = jnp.dot(a_pre.astype(BF16), wco_ref[...], preferred_element_type=F32).astype(BF16)
    tail = u[tm - CONV_PAD:, :]
    utail_ref[...] = tail
    carry_ref[...] = tail


def _ffn_mix_sample_kernel(x_ref, rc_ref, rs1_ref, rs2_ref, g1_ref, wgu_ref, wo_ref, gm_ref, win_ref, hm_ref, qg_ref,
                           kg_ref, x1_ref, q_ref, k32_ref, v32_ref, u_ref, act_ref):
    h2 = _proj_stage(x_ref, g1_ref, wgu_ref, wo_ref, gm_ref, x1_ref, act_ref)
    outs = {"u": u_ref, "q": q_ref, "k": k32_ref, "v": v32_ref}
    for name, y in _mix_parts(h2, win_ref, rc_ref, rs1_ref, rs2_ref, hm_ref, qg_ref, kg_ref):
        outs[name][...] = y


def _ffn_mix_weights(w):
    return [w["ffn1_norm"], w["ffn1_wgu"], w["ffn1_wo"], w["mix_norm"], w["w_in_a"], w["head_mean"],
            w["q_gain"], w["k_gain"]]


def _ffn_mix_sample(x, rope, w):
    n, d = x.shape
    tm = TOKEN_TILE
    assert n % tm == 0 and rope[0].shape[0] == n
    cc = w["conv_w"].shape[1]
    tok = lambda width: pl.BlockSpec((tm, width), lambda i: (i, 0))
    weights = _ffn_mix_weights(w)
    widths = [d, ATTN_W, ATTN_W, ATTN_W, cc]
    return pl.pallas_call(
        _ffn_mix_sample_kernel, grid=(n // tm,),
        in_specs=[tok(d)] + [tok(LANES)] * 3 + [_resident(a.shape) for a in weights],
        out_specs=[tok(wd) for wd in widths], out_shape=[jax.ShapeDtypeStruct((n, wd), F32) for wd in widths],
        scratch_shapes=[pltpu.VMEM((tm, w["ffn1_wo"].shape[0]), BF16)],
        compiler_params=pltpu.CompilerParams(dimension_semantics=("arbitrary",), vmem_limit_bytes=VMEM_LIMIT),
        name="ffn_mix_sample",
    )(x, *rope, *weights)


def _ffn_mix_prompt(x, rope, w, seq_len):
    n, d = x.shape
    tm = TOKEN_TILE
    t = seq_len
    assert n % t == 0 and t % tm == 0 and rope[0].shape[0] == t
    assert all(wd % tm == 0 or tm % wd == 0 for wd in WINDOWS)
    tps = t // tm
    nt = n // tm
    nseq = n // t
    cc = w["conv_w"].shape[1]
    weights = _ffn_mix_weights(w) + [w["perm1"], w["perm2"], w["conv_w"], w["conv_b"], w["conv_ln_g"],
                                     w["conv_ln_b"], w["w_conv_out"]]
    tok = lambda rows, width: pl.BlockSpec((rows, width), lambda i: (i, 0))
    rope_spec = pl.BlockSpec((tm, LANES), lambda i: (i % tps, 0))
    in_specs = [tok(tm, d)] + [rope_spec] * 3 + [_resident(a.shape) for a in weights]
    out_shape = [jax.ShapeDtypeStruct((n, d), F32)]
    out_specs = [tok(tm, d)]
    for _ in range(3):
        for dl in DILATIONS:
            out_shape.append(jax.ShapeDtypeStruct((n // dl, dl * GROUP_W), BF16))
            out_specs.append(tok(tm // dl, dl * GROUP_W))
    out_shape += [jax.ShapeDtypeStruct((n, d), BF16), jax.ShapeDtypeStruct((nseq, CONV_PAD, cc), F32)]
    out_specs += [tok(tm, d), pl.BlockSpec((None, CONV_PAD, cc), lambda i: (i // tps, 0, 0))]
    for _ in range(2):
        for wd in WINDOWS:
            first = (t - wd) // tm if wd >= tm else tps - 1
            out_shape.append(jax.ShapeDtypeStruct((nseq, GROUP_W, wd), F32))
            out_specs.append(pl.BlockSpec(
                (None, GROUP_W, min(wd, tm)),
                lambda i, first=first: (i // tps, 0, jnp.maximum(i % tps - first, 0))))
    return pl.pallas_call(
        functools.partial(_ffn_mix_prompt_kernel, tiles_per_seq=tps), grid=(nt,), in_specs=in_specs,
        out_specs=out_specs, out_shape=out_shape,
        scratch_shapes=[pltpu.VMEM((tm, w["ffn1_wo"].shape[0]), BF16), pltpu.VMEM((CONV_PAD, cc), F32)],
        compiler_params=pltpu.CompilerParams(dimension_semantics=("arbitrary",), vmem_limit_bytes=VMEM_LIMIT),
        name="ffn_mix_prompt",
    )(x, *rope, *weights)


def _attn_unit(q, kp, kc, vp, vc, first):
    row = lax.broadcasted_iota(jnp.int32, (BLK, BLK), 0)
    col = lax.broadcasted_iota(jnp.int32, (BLK, BLK), 1)
    bias_p = jnp.where(col >= row, 0.0, NEG) + jnp.where(first, NEG, 0.0)
    bias_c = jnp.where(col <= row, 0.0, NEG)
    head = lax.broadcasted_iota(jnp.int32, (1, GROUP_W), 1) // HEAD_DIM
    contract_last = (((1,), (1,)), ((), ()))
    out = jnp.zeros((BLK, GROUP_W), F32)
    lse = jnp.zeros((BLK, GROUP_W), F32)
    zero = jnp.zeros((), BF16)
    for h in range(HEADS_PER_GROUP):
        hm = head == h
        qh = jnp.where(hm, q, zero)
        sp = lax.dot_general(qh, kp, contract_last, preferred_element_type=F32) * ATTN_SCALE + bias_p
        sc = lax.dot_general(qh, kc, contract_last, preferred_element_type=F32) * ATTN_SCALE + bias_c
        m = jnp.max(jnp.maximum(sp, sc), axis=1, keepdims=True)
        pp = jnp.exp(sp - m)
        pc = jnp.exp(sc - m)
        l = jnp.sum(pp, axis=1, keepdims=True) + jnp.sum(pc, axis=1, keepdims=True)
        o = jnp.dot(pp.astype(BF16), jnp.where(hm, vp, zero), preferred_element_type=F32) \
            + jnp.dot(pc.astype(BF16), jnp.where(hm, vc, zero), preferred_element_type=F32)
        out = out + o / l
        lse = lse + jnp.where(hm, m + jnp.log(l), 0.0)
    return out, lse


def _attn_prompt_body(ins, outs, t, u):
    for g in range(N_GROUPS):
        q_ref, kc_ref, kp_ref, vc_ref, vp_ref = ins[5 * g:5 * g + 5]
        d = DILATIONS[g]
        first = (t * (UNITS // d) + u // d) == 0
        o, lse = _attn_unit(q_ref[...], kp_ref[...], kc_ref[...], vp_ref[...], vc_ref[...], first)
        outs[2 * g][...] = o.astype(BF16)
        outs[2 * g + 1][...] = lse


def _attn_prompt_kernel(*refs):
    _attn_prompt_body(refs[:5 * N_GROUPS], refs[5 * N_GROUPS:], pl.program_id(1), pl.program_id(2))


def _attn_prompt_specs(qd, kd, vd, batch, seq_len, index):
    assert seq_len % ATTN_TILE == 0
    args, in_specs, out_shape, out_specs = [], [], [], []
    for g in range(N_GROUPS):
        d = DILATIONS[g]
        nb = UNITS // d

        def cur(*i, d=d, nb=nb):
            b, t, u = index(*i)
            return (b, t * nb + u // d, u % d)

        def prev(*i, d=d, nb=nb):
            b, t, u = index(*i)
            return (b, jnp.maximum(t * nb + u // d - 1, 0), u % d)

        view = lambda a, d=d: a.reshape(batch, seq_len // d, d * GROUP_W)
        blk = lambda im: pl.BlockSpec((None, BLK, GROUP_W), im)
        args += [view(qd[g]), view(kd[g]), view(kd[g]), view(vd[g]), view(vd[g])]
        in_specs += [blk(cur), blk(cur), blk(prev), blk(cur), blk(prev)]
        out_shape += [jax.ShapeDtypeStruct((batch, seq_len // d, d * GROUP_W), BF16),
                      jax.ShapeDtypeStruct((batch, seq_len // d, d * GROUP_W), F32)]
        out_specs += [blk(cur), blk(cur)]
    return args, in_specs, out_shape, out_specs


def _attn_prompt(qd, kd, vd, batch, seq_len):
    args, in_specs, out_shape, out_specs = _attn_prompt_specs(qd, kd, vd, batch, seq_len, lambda b, t, u: (b, t, u))
    res = pl.pallas_call(
        _attn_prompt_kernel, grid=(batch, seq_len // ATTN_TILE, UNITS), in_specs=in_specs, out_specs=out_specs,
        out_shape=out_shape,
        compiler_params=pltpu.CompilerParams(dimension_semantics=("arbitrary",) * 3),
        name="attn_prompt",
    )(*args)
    return [r.reshape(-1, r.shape[2]) for r in res]


def _conv_sample_kernel(state_ref, u_ref, cw_ref, cb_ref, lng_ref, lnb_ref, a_ref, new_state_ref):
    hist = state_ref.shape[0]
    s_len = u_ref.shape[0]
    row = lambda t: state_ref[t] if t < hist else u_ref[t - hist]
    for s in range(s_len):
        acc = None
        for j in range(CONV_WIDTH):
            term = row(s + j) * cw_ref[j:j + 1, :]
            acc = term if acc is None else acc + term
        a_ref[s] = _layernorm_silu(acc + cb_ref[...], lng_ref[...], lnb_ref[...])
    for t in range(hist):
        new_state_ref[t] = row(t + s_len)


def _conv_sample(state_t, u_t, w):
    hist, db, cc = state_t.shape
    s_len = u_t.shape[0]
    assert hist == CONV_WIDTH - 1 and db % CONV_SEQS == 0
    conv_w = [w["conv_w"], w["conv_b"], w["conv_ln_g"], w["conv_ln_b"]]
    slab = lambda rows: pl.BlockSpec((rows, CONV_SEQS, cc), lambda i: (0, i, 0))
    return pl.pallas_call(
        _conv_sample_kernel, grid=(db // CONV_SEQS,),
        in_specs=[slab(hist), slab(s_len)] + [_resident(a.shape) for a in conv_w],
        out_specs=[slab(s_len), slab(hist)],
        out_shape=[jax.ShapeDtypeStruct((s_len, db, cc), F32), jax.ShapeDtypeStruct((hist, db, cc), F32)],
        compiler_params=pltpu.CompilerParams(dimension_semantics=("arbitrary",)),
        name="conv_sample",
    )(state_t, u_t, *conv_w)


def _sample_bias(width, d, s_len):
    s = lax.broadcasted_iota(jnp.int32, (SAMPLE_ROWS, width), 0)
    t = lax.broadcasted_iota(jnp.int32, (SAMPLE_ROWS, width), 1)
    same_residue = jnp.bitwise_and(s - t, d - 1) == 0
    live = s < s_len
    old = jnp.where(live & same_residue & (t >= s), 0.0, NEG)
    sn = s[:, :LANES]
    tn = t[:, :LANES]
    new = jnp.where(live[:, :LANES] & same_residue[:, :LANES] & (tn <= sn), 0.0, NEG)
    return old, new


def _attn_sample_kernel(*refs, s_len):
    _attn_sample_body(*refs, s_len=s_len)


def _attn_sample_body(q_ref, kn_ref, vn_ref, kc0_ref, vc0_ref, kc1_ref, vc1_ref, kc2_ref, vc2_ref,
                      b_ref, nk0_ref, nv0_ref, nk1_ref, nv1_ref, nk2_ref, nv2_ref, *, s_len):
    kc_refs = (kc0_ref, kc1_ref, kc2_ref)
    vc_refs = (vc0_ref, vc1_ref, vc2_ref)
    nk_refs = (nk0_ref, nk1_ref, nk2_ref)
    nv_refs = (nv0_ref, nv1_ref, nv2_ref)
    q = q_ref[...]
    pad = jnp.zeros((LANES - SAMPLE_ROWS, ATTN_W), F32)
    knt = jnp.transpose(jnp.concatenate([kn_ref[...], pad], axis=0))
    vnt = jnp.transpose(jnp.concatenate([vn_ref[...], pad], axis=0))
    biases = [_sample_bias(WINDOWS[g], DILATIONS[g], s_len) for g in range(N_GROUPS)]
    lane = lax.broadcasted_iota(jnp.int32, (1, LANES), 1)
    keep = lane < LANES - s_len
    nt = (((1,), (1,)), ((), ()))

    def shifted(old, new_cols, out_ref, h):
        w = old.shape[1]
        cols = [old[:, j * LANES:(j + 1) * LANES] for j in range(w // LANES)] + [new_cols]
        rolled = [pltpu.roll(c, LANES - s_len, 1) for c in cols]
        for j in range(w // LANES):
            out_ref[h, :, j * LANES:(j + 1) * LANES] = jnp.where(keep, rolled[j], rolled[j + 1])

    heads = range(HEADS_PER_GROUP)
    lanes_of = lambda g, h: slice((g * HEADS_PER_GROUP + h) * HEAD_DIM, (g * HEADS_PER_GROUP + h + 1) * HEAD_DIM)
    s_old = [[None] * N_GROUPS for _ in heads]
    s_new = [[None] * N_GROUPS for _ in heads]
    for h in heads:
        for g in range(N_GROUPS):
            qh = q[:, lanes_of(g, h)].astype(BF16)
            k_old = kc_refs[g][h]
            k_new = knt[lanes_of(g, h), :]
            s_old[h][g] = jnp.dot(qh, k_old.astype(BF16), preferred_element_type=F32) * ATTN_SCALE + biases[g][0]
            s_new[h][g] = jnp.dot(qh, k_new.astype(BF16), preferred_element_type=F32) * ATTN_SCALE + biases[g][1]
            shifted(k_old, k_new, nk_refs[g], h)
    p_old = [[None] * N_GROUPS for _ in heads]
    p_new = [[None] * N_GROUPS for _ in heads]
    dens = []
    for h in heads:
        m = None
        for a in s_old[h] + s_new[h]:
            am = jnp.max(a, axis=1, keepdims=True)
            m = am if m is None else jnp.maximum(m, am)
        den = jnp.zeros((SAMPLE_ROWS, 1), F32)
        for g in range(N_GROUPS):
            p_old[h][g] = jnp.exp(s_old[h][g] - m)
            p_new[h][g] = jnp.exp(s_new[h][g] - m)
            den = den + jnp.sum(p_old[h][g], axis=1, keepdims=True) + jnp.sum(p_new[h][g], axis=1, keepdims=True)
        dens.append(den)
    for h in heads:
        num = jnp.zeros((SAMPLE_ROWS, HEAD_DIM), F32)
        for g in range(N_GROUPS):
            v_old = vc_refs[g][h]
            v_new = vnt[lanes_of(g, h), :]
            num = num + lax.dot_general(p_old[h][g].astype(BF16), v_old.astype(BF16), nt,
                                        preferred_element_type=F32) \
                + lax.dot_general(p_new[h][g].astype(BF16), v_new.astype(BF16), nt, preferred_element_type=F32)
            shifted(v_old, v_new, nv_refs[g], h)
        b_ref[:, h * HEAD_DIM:(h + 1) * HEAD_DIM] = num / dens[h]


def _attn_sample_specs(q8, kn8, vn8, caches_k, caches_v, s_len):
    db = q8.shape[0]
    assert q8.shape[1] == SAMPLE_ROWS and s_len <= min(SAMPLE_ROWS, DILATIONS[1])
    tok = pl.BlockSpec((None, SAMPLE_ROWS, ATTN_W), lambda b: (b, 0, 0))
    args, in_specs, out_shape, out_specs = [q8, kn8, vn8], [tok] * 3, [], []
    out_shape.append(jax.ShapeDtypeStruct((db, SAMPLE_ROWS, GROUP_W), F32))
    out_specs.append(pl.BlockSpec((None, SAMPLE_ROWS, GROUP_W), lambda b: (b, 0, 0)))
    for g in range(N_GROUPS):
        shape = (db, HEADS_PER_GROUP, HEAD_DIM, WINDOWS[g])
        spec = pl.BlockSpec((None,) + shape[1:], lambda b: (b, 0, 0, 0))
        for c in (caches_k[g], caches_v[g]):
            assert c.shape == shape
            args.append(c)
            in_specs.append(spec)
            out_shape.append(jax.ShapeDtypeStruct(shape, F32))
            out_specs.append(spec)
    return args, in_specs, out_shape, out_specs


def _attn_sample(q8, kn8, vn8, caches_k, caches_v, s_len):
    db = q8.shape[0]
    args, in_specs, out_shape, out_specs = _attn_sample_specs(q8, kn8, vn8, caches_k, caches_v, s_len)
    return pl.pallas_call(
        functools.partial(_attn_sample_kernel, s_len=s_len), grid=(db,), in_specs=in_specs, out_specs=out_specs,
        out_shape=out_shape,
        compiler_params=pltpu.CompilerParams(dimension_semantics=("arbitrary",), vmem_limit_bytes=VMEM_LIMIT),
        name="attn_sample",
    )(*args)


def _attn_both_kernel(*refs, s_len, tiles):
    n_pi, n_si = 5 * N_GROUPS, 3 + 2 * N_GROUPS
    n_po = 2 * N_GROUPS
    p_in, s_in = refs[:n_pi], refs[n_pi:n_pi + n_si]
    p_out, s_out = refs[n_pi + n_si:n_pi + n_si + n_po], refs[n_pi + n_si + n_po:]
    i = pl.program_id(0)
    _attn_prompt_body(p_in, p_out, (i // UNITS) % tiles, i % UNITS)
    _attn_sample_body(*s_in, *s_out, s_len=s_len)


def _attn_both(qd, kd, vd, batch, seq_len, q8, kn8, vn8, caches_k, caches_v, s_len):
    tiles = seq_len // ATTN_TILE
    n = batch * tiles * UNITS
    assert n == q8.shape[0]
    index = lambda i: (i // (tiles * UNITS), (i // UNITS) % tiles, i % UNITS)
    p_args, p_in, p_shape, p_out = _attn_prompt_specs(qd, kd, vd, batch, seq_len, index)
    s_args, s_in, s_shape, s_out = _attn_sample_specs(q8, kn8, vn8, caches_k, caches_v, s_len)
    res = pl.pallas_call(
        functools.partial(_attn_both_kernel, s_len=s_len, tiles=tiles), grid=(n,), in_specs=p_in + s_in,
        out_specs=p_out + s_out, out_shape=p_shape + s_shape,
        compiler_params=pltpu.CompilerParams(dimension_semantics=("arbitrary",), vmem_limit_bytes=VMEM_LIMIT),
        name="attn_both",
    )(*p_args, *s_args)
    return [r.reshape(-1, r.shape[2]) for r in res[:len(p_shape)]], res[len(p_shape):]


def _merge_stage(x1_ref, a, battn, gm_ref, wg_ref, bg_ref, wao_ref, wout_ref):
    x1 = x1_ref[...]
    d = x1.shape[1]
    h = _rms(x1, gm_ref[...]).astype(BF16)
    gates = jax.nn.sigmoid(jnp.dot(h, wg_ref[...], preferred_element_type=F32) + bg_ref[...])
    b = jnp.dot(battn.astype(BF16), wao_ref[...], preferred_element_type=F32)
    mix = (gates[:, :d] * a + gates[:, d:] * b).astype(BF16)
    return x1 + jnp.dot(mix, wout_ref[...], preferred_element_type=F32)


def _ffn_stage(y, g2_ref, wgu_ref, wo_ref, act_ref):
    return y + 0.5 * _swiglu(_rms(y, g2_ref[...]).astype(BF16), wgu_ref, wo_ref, act_ref)


def _undilate(blk, pt_ref, d):
    if d == 1:
        return blk.astype(F32)
    per = TOKEN_TILE // d
    if blk.shape[0] > per:
        return jnp.concatenate([_undilate(blk[s:s + per, :], pt_ref, d) for s in range(0, blk.shape[0], per)],
                               axis=0)
    rows = jnp.concatenate([blk[:, r * GROUP_W:(r + 1) * GROUP_W] for r in range(d)], axis=0)
    pt = pt_ref[...]
    if rows.dtype == BF16:
        return jnp.dot(pt, rows, preferred_element_type=F32)
    hi = rows.astype(BF16)
    rest = rows - hi.astype(F32)
    mid = rest.astype(BF16)
    lo = (rest - mid.astype(F32)).astype(BF16)
    return jnp.dot(pt, hi, preferred_element_type=F32) + jnp.dot(pt, mid, preferred_element_type=F32) \
        + jnp.dot(pt, lo, preferred_element_type=F32)


def _merge_ffn_prompt_kernel(x1_ref, a_ref, o0_ref, o1_ref, o2_ref, l0_ref, l1_ref, l2_ref, pt1_ref, pt2_ref,
                             gm_ref, wg_ref, bg_ref, wao_ref, wout_ref, g2_ref, wgu_ref, wo_ref, out_ref, act_ref):
    pts = (None, pt1_ref, pt2_ref)
    lses = [_undilate(r[...], pts[g], DILATIONS[g]) for g, r in enumerate((l0_ref, l1_ref, l2_ref))]
    outs = [_undilate(r[...], pts[g], DILATIONS[g]) for g, r in enumerate((o0_ref, o1_ref, o2_ref))]
    mx = jnp.maximum(jnp.maximum(lses[0], lses[1]), lses[2])
    num = None
    den = None
    for o, l in zip(outs, lses):
        wgt = jnp.exp(l - mx)
        num = wgt * o if num is None else num + wgt * o
        den = wgt if den is None else den + wgt
    y = _merge_stage(x1_ref, a_ref[...].astype(F32), num / den, gm_ref, wg_ref, bg_ref, wao_ref, wout_ref)
    out_ref[...] = _ffn_stage(y, g2_ref, wgu_ref, wo_ref, act_ref)


def _merge_ffn_sample_kernel(x1_ref, a_ref, b_ref, gm_ref, wg_ref, bg_ref, wco_ref, wao_ref, wout_ref, g2_ref,
                             wgu_ref, wo_ref, out_ref, act_ref):
    a = jnp.dot(a_ref[...].astype(BF16), wco_ref[...], preferred_element_type=F32)
    y = _merge_stage(x1_ref, a, b_ref[...], gm_ref, wg_ref, bg_ref, wao_ref, wout_ref)
    out_ref[...] = _ffn_stage(y, g2_ref, wgu_ref, wo_ref, act_ref)


def _merge_weights(w):
    return [w["mix_norm"], w["w_in_g"], w["b_gate"], w["w_conv_out"], w["w_attn_out"], w["w_out"], w["ffn2_norm"],
            w["ffn2_wgu"], w["ffn2_wo"]]


def _merge_ffn_sample(x1, a_pre, b, w):
    n, d = x1.shape
    tm = TOKEN_TILE
    assert n % tm == 0
    tok = lambda width: pl.BlockSpec((tm, width), lambda i: (i, 0))
    weights = _merge_weights(w)
    return pl.pallas_call(
        _merge_ffn_sample_kernel, grid=(n // tm,),
        in_specs=[tok(d), tok(a_pre.shape[1]), tok(GROUP_W)] + [_resident(a.shape) for a in weights],
        out_specs=tok(d), out_shape=jax.ShapeDtypeStruct((n, d), F32),
        scratch_shapes=[pltpu.VMEM((tm, w["ffn2_wo"].shape[0]), BF16)],
        compiler_params=pltpu.CompilerParams(dimension_semantics=("arbitrary",), vmem_limit_bytes=VMEM_LIMIT),
        name="merge_ffn_sample",
    )(x1, a_pre, b, *weights)


def _merge_ffn_prompt(x1, a, outs, lses, w):
    n, d = x1.shape
    tm = MERGE_TILE
    assert n % tm == 0 and tm % TOKEN_TILE == 0
    tok = lambda rows, width: pl.BlockSpec((rows, width), lambda i: (i, 0))
    grp = [tok(tm // dl, dl * GROUP_W) for dl in DILATIONS]
    mw = _merge_weights(w)
    weights = [w["permt1"], w["permt2"]] + [x for x in mw if x is not w["w_conv_out"]]
    return pl.pallas_call(
        _merge_ffn_prompt_kernel, grid=(n // tm,),
        in_specs=[tok(tm, d), tok(tm, d)] + grp + grp + [_resident(x.shape) for x in weights],
        out_specs=tok(tm, d), out_shape=jax.ShapeDtypeStruct((n, d), F32),
        scratch_shapes=[pltpu.VMEM((tm, w["ffn2_wo"].shape[0]), BF16)],
        compiler_params=pltpu.CompilerParams(dimension_semantics=("arbitrary",), vmem_limit_bytes=VMEM_LIMIT),
        name="merge_ffn_prompt",
    )(x1, a, *outs, *lses, *weights)


def _rope_tables(pos):
    half = ROT_DIM // 2
    inv = jnp.float32(ROPE_THETA) ** (-jnp.arange(half, dtype=F32) * (2.0 / ROT_DIM))
    ang = pos.astype(F32)[:, None] * inv[None, :]
    cos, sin = jnp.cos(ang), jnp.sin(ang)
    n = pos.shape[0]
    rest = HEAD_DIM - ROT_DIM
    zh = jnp.zeros((n, half), F32)
    c = jnp.concatenate([cos, cos, jnp.ones((n, rest), F32)], axis=1)
    s1 = jnp.concatenate([zh, sin, jnp.zeros((n, rest), F32)], axis=1)
    s2 = jnp.concatenate([-sin, zh, jnp.zeros((n, rest), F32)], axis=1)
    rep = LANES // HEAD_DIM
    return tuple(jnp.tile(t, (1, rep)) for t in (c, s1, s2))


def _prepare_weights(l, ffn1_norm, ffn1_w_in, ffn1_w_out, mix_norm, w_in, b_gate, q_norm, k_norm, conv_w, conv_b,
                     conv_ln_g, conv_ln_b, w_conv_out, w_attn_out, w_out, ffn2_norm, ffn2_w_in, ffn2_w_out):
    cc = conv_w.shape[2]
    n_a = 2 * cc + 3 * ATTN_W
    lane = jnp.arange(GROUP_W)
    same_head = (lane[:, None] // HEAD_DIM) == (lane[None, :] // HEAD_DIM)
    row = lambda a: a.reshape(1, -1).astype(F32)
    tok = jnp.arange(TOKEN_TILE)

    def perm(d):
        src = (tok % (TOKEN_TILE // d)) * d + tok // (TOKEN_TILE // d)
        return (src[:, None] == tok[None, :]).astype(BF16)

    return {
        "perm1": perm(DILATIONS[1]), "perm2": perm(DILATIONS[2]),
        "permt1": perm(DILATIONS[1]).T, "permt2": perm(DILATIONS[2]).T,
        "ffn1_norm": row(ffn1_norm[l]), "ffn1_wgu": ffn1_w_in[l].astype(BF16), "ffn1_wo": ffn1_w_out[l].astype(BF16),
        "mix_norm": row(mix_norm[l]), "w_in_a": w_in[l][:, :n_a].astype(BF16), "w_in_g": w_in[l][:, n_a:].astype(BF16),
        "b_gate": row(b_gate[l]),
        "head_mean": jnp.where(same_head, 1.0 / HEAD_DIM, 0.0).astype(BF16),
        "q_gain": row(q_norm[l]), "k_gain": row(k_norm[l]),
        "conv_w": jnp.concatenate([conv_w[l], jnp.zeros((CONV_PAD - CONV_WIDTH, cc), F32)], axis=0),
        "conv_b": row(conv_b[l]), "conv_ln_g": row(conv_ln_g[l]), "conv_ln_b": row(conv_ln_b[l]),
        "w_conv_out": w_conv_out[l].astype(BF16), "w_attn_out": w_attn_out[l].astype(BF16),
        "w_out": w_out[l].astype(BF16),
        "ffn2_norm": row(ffn2_norm[l]), "ffn2_wgu": ffn2_w_in[l].astype(BF16), "ffn2_wo": ffn2_w_out[l].astype(BF16),
    }


def kernel(x_prompt, x_sample, cache_k_w128, cache_v_w128, cache_k_w512, cache_v_w512, cache_k_w2048, cache_v_w2048, state_conv, ffn1_norm, ffn1_w_in, ffn1_w_out, mix_norm, w_in, b_gate, q_norm, k_norm, conv_w, conv_b, conv_ln_g, conv_ln_b, w_conv_out, w_attn_out, w_out, ffn2_norm, ffn2_w_in, ffn2_w_out):
    batch, t, d = x_prompt.shape
    db, s_len, _ = x_sample.shape
    depth = w_in.shape[0]
    cc = conv_w.shape[2]
    hd = (HEADS_PER_GROUP, HEAD_DIM)
    cache_k = (cache_k_w128, cache_k_w512, cache_k_w2048)
    cache_v = (cache_v_w128, cache_v_w512, cache_v_w2048)
    for g, wdw in enumerate(WINDOWS):
        assert cache_k[g].shape == (depth, db, wdw, *hd), "sample caches must hold a full window"
        assert t >= wdw
    rope_p = _rope_tables(jnp.arange(t))
    rope_s = _rope_tables(PAST_LEN + jnp.arange(db * s_len) // db)
    to_time_minor = lambda c: jnp.transpose(c, (0, 2, 3, 1))
    from_time_minor = lambda c: jnp.transpose(c, (0, 3, 1, 2))

    xp = x_prompt.reshape(batch * t, d)
    xs = jnp.transpose(x_sample, (1, 0, 2)).reshape(s_len * db, d)
    outs_p = [[] for _ in range(2 * N_GROUPS)]
    outs_s = [[] for _ in range(2 * N_GROUPS)]
    new_conv_p, new_conv_s = [], []
    for l in range(depth):
        w = _prepare_weights(l, ffn1_norm, ffn1_w_in, ffn1_w_out, mix_norm, w_in, b_gate, q_norm, k_norm, conv_w,
                             conv_b, conv_ln_g, conv_ln_b, w_conv_out, w_attn_out, w_out, ffn2_norm, ffn2_w_in,
                             ffn2_w_out)
        x1p, *rest = _ffn_mix_prompt(xp, rope_p, w, t)
        qkv, (a_p, utail), kv_t = rest[:9], rest[9:11], rest[11:]
        x1s, qs, k32s, v32s, us = _ffn_mix_sample(xs, rope_s, w)
        a_s, state_new = _conv_sample(jnp.transpose(state_conv[l], (1, 0, 2)), us.reshape(s_len, db, cc), w)
        rows8 = lambda a: jnp.pad(jnp.transpose(a.reshape(s_len, db, ATTN_W), (1, 0, 2)),
                                  ((0, 0), (0, SAMPLE_ROWS - s_len), (0, 0)))
        ck = [to_time_minor(cache_k[g][l]) for g in range(N_GROUPS)]
        cv = [to_time_minor(cache_v[g][l]) for g in range(N_GROUPS)]
        if batch * (t // ATTN_TILE) * UNITS == db:
            attn_p, (b_s, *new_caches) = _attn_both(qkv[0:3], qkv[3:6], qkv[6:9], batch, t, rows8(qs), rows8(k32s),
                                                    rows8(v32s), ck, cv, s_len)
        else:
            attn_p = _attn_prompt(qkv[0:3], qkv[3:6], qkv[6:9], batch, t)
            b_s, *new_caches = _attn_sample(rows8(qs), rows8(k32s), rows8(v32s), ck, cv, s_len)
        xp = _merge_ffn_prompt(x1p, a_p, attn_p[0::2], attn_p[1::2], w)
        b_s = jnp.transpose(b_s[:, :s_len], (1, 0, 2)).reshape(s_len * db, GROUP_W)
        xs = _merge_ffn_sample(x1s, a_s.reshape(s_len * db, cc), b_s, w)
        new_conv_p.append(utail[:, CONV_PAD - (CONV_WIDTH - 1):, :])
        new_conv_s.append(jnp.transpose(state_new, (1, 0, 2)))
        for i, (g, c) in enumerate([(g, c) for c in range(2) for g in range(N_GROUPS)]):
            outs_p[2 * g + c].append(from_time_minor(kv_t[i].reshape(batch, *hd, WINDOWS[g])))
        for i, nc in enumerate(new_caches):
            outs_s[i].append(from_time_minor(nc))
    y_s = jnp.transpose(xs.reshape(s_len, db, d), (1, 0, 2))
    return (xp.reshape(batch, t, d), y_s, *[jnp.stack(o) for o in outs_p], jnp.stack(new_conv_p),
            *[jnp.stack(o) for o in outs_s], jnp.stack(new_conv_s))
```

```python
import functools

import jax
import jax.numpy as jnp
from jax import lax
from jax.experimental import pallas as pl
from jax.experimental.pallas import tpu as pltpu

F32 = jnp.float32
BF16 = jnp.bfloat16

HEAD_DIM = 64
WINDOWS = (128, 512, 2048)
DILATIONS = (1, 4, 16)
N_GROUPS = 3
HEADS_PER_GROUP = 4
GROUP_W = HEADS_PER_GROUP * HEAD_DIM
ATTN_W = N_GROUPS * GROUP_W
BLK = WINDOWS[0] // DILATIONS[0]
assert all(w // d == BLK for w, d in zip(WINDOWS, DILATIONS))
ATTN_TILE = max(WINDOWS)
UNITS = ATTN_TILE // BLK
ATTN_SCALE = HEAD_DIM ** -0.5
ROT_DIM = HEAD_DIM // 4
ROPE_THETA = 500000.0
CONV_WIDTH = 31
CONV_PAD = 32
NORM_EPS = 1e-6
PAST_LEN = 2048
NEG = -1e30
LANES = 128
SUBLANES = 8
FF_CHUNK = 256
TOKEN_TILE = 256
SAMPLE_ROWS = 8
CONV_SEQS = 16
VMEM_LIMIT = 56 * 1024 * 1024


def _resident(shape):
    nd = len(shape)
    return pl.BlockSpec(shape, lambda *_: (0,) * nd, pipeline_mode=pl.Buffered(1))


def _rms(x, g):
    ms = jnp.mean(x * x, axis=-1, keepdims=True)
    return x * lax.rsqrt(ms + NORM_EPS) * g


def _swiglu(h, wgu_ref, wo_ref, act_ref):
    f = wo_ref.shape[0]
    assert f % FF_CHUNK == 0
    for c in range(f // FF_CHUNK):
        g = jnp.dot(h, wgu_ref[:, c * FF_CHUNK:(c + 1) * FF_CHUNK], preferred_element_type=F32)
        u = jnp.dot(h, wgu_ref[:, f + c * FF_CHUNK:f + (c + 1) * FF_CHUNK], preferred_element_type=F32)
        act_ref[:, c * FF_CHUNK:(c + 1) * FF_CHUNK] = (g * jax.nn.sigmoid(g) * u).astype(BF16)
    return jnp.dot(act_ref[...], wo_ref[...], preferred_element_type=F32)


def _head_norm(y, hm_ref, gain):
    parts = []
    for g in range(N_GROUPS):
        yg = y[:, g * GROUP_W:(g + 1) * GROUP_W]
        ms = jnp.dot((yg * yg).astype(BF16), hm_ref[...], preferred_element_type=F32)
        parts.append(yg * lax.rsqrt(ms + NORM_EPS))
    return jnp.concatenate(parts, axis=1) * gain


def _rope(y, c, s1, s2):
    half = ROT_DIM // 2
    parts = []
    for i in range(y.shape[1] // LANES):
        yc = y[:, i * LANES:(i + 1) * LANES]
        parts.append(yc * c + pltpu.roll(yc, half, 1) * s1 + pltpu.roll(yc, LANES - half, 1) * s2)
    return jnp.concatenate(parts, axis=1)


def _layernorm_silu(y, g, b):
    mu = jnp.mean(y, axis=-1, keepdims=True)
    yc = y - mu
    var = jnp.mean(yc * yc, axis=-1, keepdims=True)
    yn = yc * lax.rsqrt(var + NORM_EPS) * g + b
    return yn * jax.nn.sigmoid(yn)


def _causal_conv(ext, cw_ref, cb_ref, lng_ref, lnb_ref):
    tm = ext.shape[0] - CONV_PAD
    first = CONV_PAD - (CONV_WIDTH - 1)
    rows = 64
    outs = []
    for r0 in range(0, tm, rows):
        acc = None
        base = ext[r0:r0 + rows + CONV_PAD, :]
        for r in range(SUBLANES):
            win = base if r == 0 else pltpu.roll(base, base.shape[0] - r, 0)
            for j in range(CONV_WIDTH):
                if (first + j) % SUBLANES != r:
                    continue
                o = first + j - r
                term = win[o:o + rows, :] * cw_ref[j:j + 1, :]
                acc = term if acc is None else acc + term
        outs.append(_layernorm_silu(acc + cb_ref[...], lng_ref[...], lnb_ref[...]))
    return jnp.concatenate(outs, axis=0)


def _proj_stage(x_ref, g1_ref, wgu_ref, wo_ref, gm_ref, x1_ref, act_ref):
    x = x_ref[...]
    h = _rms(x, g1_ref[...]).astype(BF16)
    x1 = x + 0.5 * _swiglu(h, wgu_ref, wo_ref, act_ref)
    x1_ref[...] = x1
    return _rms(x1, gm_ref[...]).astype(BF16)


def _mix_parts(h2, win_ref, rc_ref, rs1_ref, rs2_ref, hm_ref, qg_ref, kg_ref):
    cc = (win_ref.shape[1] - 3 * ATTN_W) // 2
    proj = lambda lo, hi: jnp.dot(h2, win_ref[:, lo:hi], preferred_element_type=F32)
    zu = proj(0, 2 * cc)
    yield "u", zu[:, :cc] * jax.nn.sigmoid(zu[:, cc:])
    c, s1, s2 = rc_ref[...], rs1_ref[...], rs2_ref[...]
    yield "q", _rope(_head_norm(proj(2 * cc, 2 * cc + ATTN_W), hm_ref, qg_ref[...]), c, s1, s2)
    yield "k", _rope(_head_norm(proj(2 * cc + ATTN_W, 2 * cc + 2 * ATTN_W), hm_ref, kg_ref[...]), c, s1, s2)
    yield "v", proj(2 * cc + 2 * ATTN_W, 2 * cc + 3 * ATTN_W)


def _store_dilated(y, perm_refs, out_refs):
    yb = y.astype(BF16)
    out_refs[0][...] = yb[:, :GROUP_W]
    for g in range(1, N_GROUPS):
        d = DILATIONS[g]
        rows = y.shape[0] // d
        yp = jnp.dot(perm_refs[g - 1][...], yb[:, g * GROUP_W:(g + 1) * GROUP_W],
                     preferred_element_type=F32).astype(BF16)
        for r in range(d):
            out_refs[g][:, r * GROUP_W:(r + 1) * GROUP_W] = yp[r * rows:(r + 1) * rows, :]


def _ffn_mix_prompt_kernel(x_ref, rc_ref, rs1_ref, rs2_ref, g1_ref, wgu_ref, wo_ref, gm_ref, win_ref, hm_ref, qg_ref,
                           kg_ref, p1_ref, p2_ref, cw_ref, cb_ref, lng_ref, lnb_ref, wco_ref,
                           x1_ref, q0_ref, q1_ref, q2_ref, k0_ref, k1_ref, k2_ref, v0_ref, v1_ref, v2_ref, a_ref,
                           utail_ref, kt0_ref, kt1_ref, kt2_ref, vt0_ref, vt1_ref, vt2_ref, act_ref, carry_ref, *,
                           tiles_per_seq):
    tm = x_ref.shape[0]
    i = pl.program_id(0)

    @pl.when(i == 0)
    def _():
        carry_ref[...] = jnp.zeros(carry_ref.shape, F32)

    h2 = _proj_stage(x_ref, g1_ref, wgu_ref, wo_ref, gm_ref, x1_ref, act_ref)
    perms = (p1_ref, p2_ref)
    dilated = {"q": (q0_ref, q1_ref, q2_ref), "k": (k0_ref, k1_ref, k2_ref), "v": (v0_ref, v1_ref, v2_ref)}
    windows = {"k": (kt0_ref, kt1_ref, kt2_ref), "v": (vt0_ref, vt1_ref, vt2_ref)}
    parts = dict(_mix_parts(h2, win_ref, rc_ref, rs1_ref, rs2_ref, hm_ref, qg_ref, kg_ref))
    u = parts.pop("u")
    for name, y in parts.items():
        _store_dilated(y, perms, dilated[name])
        if name in windows:
            yt = jnp.transpose(y)
            for g, ref in enumerate(windows[name]):
                ref[...] = yt[g * GROUP_W:(g + 1) * GROUP_W, tm - ref.shape[1]:]
    tile = jnp.zeros(carry_ref.shape, jnp.int32) + i % tiles_per_seq
    ext = jnp.concatenate([jnp.where(tile == 0, 0.0, carry_ref[...]), u], axis=0)
    a_pre = _causal_conv(ext, cw_ref, cb_ref, lng_ref, lnb_ref)
    a_ref[...] = jnp.dot(a_pre.astype(BF16), wco_ref[...], preferred_element_type=F32).astype(BF16)
    tail = u[tm - CONV_PAD:, :]
    utail_ref[...] = tail
    carry_ref[...] = tail


def _ffn_mix_sample_kernel(x_ref, rc_ref, rs1_ref, rs2_ref, g1_ref, wgu_ref, wo_ref, gm_ref, win_ref, hm_ref, qg_ref,
                           kg_ref, x1_ref, q_ref, k32_ref, v32_ref, u_ref, act_ref):
    h2 = _proj_stage(x_ref, g1_ref, wgu_ref, wo_ref, gm_ref, x1_ref, act_ref)
    outs = {"u": u_ref, "q": q_ref, "k": k32_ref, "v": v32_ref}
    for name, y in _mix_parts(h2, win_ref, rc_ref, rs1_ref, rs2_ref, hm_ref, qg_ref, kg_ref):
        outs[name][...] = y


def _ffn_mix_weights(w):
    return [w["ffn1_norm"], w["ffn1_wgu"], w["ffn1_wo"], w["mix_norm"], w["w_in_a"], w["head_mean"],
            w["q_gain"], w["k_gain"]]


def _ffn_mix_sample(x, rope, w):
    n, d = x.shape
    tm = TOKEN_TILE
    assert n % tm == 0 and rope[0].shape[0] == n
    cc = w["conv_w"].shape[1]
    tok = lambda width: pl.BlockSpec((tm, width), lambda i: (i, 0))
    weights = _ffn_mix_weights(w)
    widths = [d, ATTN_W, ATTN_W, ATTN_W, cc]
    return pl.pallas_call(
        _ffn_mix_sample_kernel, grid=(n // tm,),
        in_specs=[tok(d)] + [tok(LANES)] * 3 + [_resident(a.shape) for a in weights],
        out_specs=[tok(wd) for wd in widths], out_shape=[jax.ShapeDtypeStruct((n, wd), F32) for wd in widths],
        scratch_shapes=[pltpu.VMEM((tm, w["ffn1_wo"].shape[0]), BF16)],
        compiler_params=pltpu.CompilerParams(dimension_semantics=("arbitrary",), vmem_limit_bytes=VMEM_LIMIT),
        name="ffn_mix_sample",
    )(x, *rope, *weights)


def _ffn_mix_prompt(x, rope, w, seq_len):
    n, d = x.shape
    tm = TOKEN_TILE
    t = seq_len
    assert n % t == 0 and t % tm == 0 and rope[0].shape[0] == t
    assert all(wd % tm == 0 or tm % wd == 0 for wd in WINDOWS)
    tps = t // tm
    nt = n // tm
    nseq = n // t
    cc = w["conv_w"].shape[1]
    weights = _ffn_mix_weights(w) + [w["perm1"], w["perm2"], w["conv_w"], w["conv_b"], w["conv_ln_g"],
                                     w["conv_ln_b"], w["w_conv_out"]]
    tok = lambda rows, width: pl.BlockSpec((rows, width), lambda i: (i, 0))
    rope_spec = pl.BlockSpec((tm, LANES), lambda i: (i % tps, 0))
    in_specs = [tok(tm, d)] + [rope_spec] * 3 + [_resident(a.shape) for a in weights]
    out_shape = [jax.ShapeDtypeStruct((n, d), F32)]
    out_specs = [tok(tm, d)]
    for _ in range(3):
        for dl in DILATIONS:
            out_shape.append(jax.ShapeDtypeStruct((n // dl, dl * GROUP_W), BF16))
            out_specs.append(tok(tm // dl, dl * GROUP_W))
    out_shape += [jax.ShapeDtypeStruct((n, d), BF16), jax.ShapeDtypeStruct((nseq, CONV_PAD, cc), F32)]
    out_specs += [tok(tm, d), pl.BlockSpec((None, CONV_PAD, cc), lambda i: (i // tps, 0, 0))]
    for _ in range(2):
        for wd in WINDOWS:
            first = (t - wd) // tm if wd >= tm else tps - 1
            out_shape.append(jax.ShapeDtypeStruct((nseq, GROUP_W, wd), F32))
            out_specs.append(pl.BlockSpec(
                (None, GROUP_W, min(wd, tm)),
                lambda i, first=first: (i // tps, 0, jnp.maximum(i % tps - first, 0))))
    return pl.pallas_call(
        functools.partial(_ffn_mix_prompt_kernel, tiles_per_seq=tps), grid=(nt,), in_specs=in_specs,
        out_specs=out_specs, out_shape=out_shape,
        scratch_shapes=[pltpu.VMEM((tm, w["ffn1_wo"].shape[0]), BF16), pltpu.VMEM((CONV_PAD, cc), F32)],
        compiler_params=pltpu.CompilerParams(dimension_semantics=("arbitrary",), vmem_limit_bytes=VMEM_LIMIT),
        name="ffn_mix_prompt",
    )(x, *rope, *weights)


def _attn_unit(q, kp, kc, vp, vc, first):
    row = lax.broadcasted_iota(jnp.int32, (BLK, BLK), 0)
    col = lax.broadcasted_iota(jnp.int32, (BLK, BLK), 1)
    bias_p = jnp.where(col >= row, 0.0, NEG) + jnp.where(first, NEG, 0.0)
    bias_c = jnp.where(col <= row, 0.0, NEG)
    head = lax.broadcasted_iota(jnp.int32, (1, GROUP_W), 1) // HEAD_DIM
    contract_last = (((1,), (1,)), ((), ()))
    out = jnp.zeros((BLK, GROUP_W), F32)
    lse = jnp.zeros((BLK, GROUP_W), F32)
    zero = jnp.zeros((), BF16)
    for h in range(HEADS_PER_GROUP):
        hm = head == h
        qh = jnp.where(hm, q, zero)
        sp = lax.dot_general(qh, kp, contract_last, preferred_element_type=F32) * ATTN_SCALE + bias_p
        sc = lax.dot_general(qh, kc, contract_last, preferred_element_type=F32) * ATTN_SCALE + bias_c
        m = jnp.max(jnp.maximum(sp, sc), axis=1, keepdims=True)
        pp = jnp.exp(sp - m)
        pc = jnp.exp(sc - m)
        l = jnp.sum(pp, axis=1, keepdims=True) + jnp.sum(pc, axis=1, keepdims=True)
        o = jnp.dot(pp.astype(BF16), jnp.where(hm, vp, zero), preferred_element_type=F32) \
            + jnp.dot(pc.astype(BF16), jnp.where(hm, vc, zero), preferred_element_type=F32)
        out = out + o / l
        lse = lse + jnp.where(hm, m + jnp.log(l), 0.0)
    return out, lse


def _attn_prompt_body(ins, outs, t, u):
    for g in range(N_GROUPS):
        q_ref, kc_ref, kp_ref, vc_ref, vp_ref = ins[5 * g:5 * g + 5]
        d = DILATIONS[g]
        first = (t * (UNITS // d) + u // d) == 0
        o, lse = _attn_unit(q_ref[...], kp_ref[...], kc_ref[...], vp_ref[...], vc_ref[...], first)
        outs[2 * g][...] = o.astype(BF16)
        outs[2 * g + 1][...] = lse


def _attn_prompt_kernel(*refs):
    _attn_prompt_body(refs[:5 * N_GROUPS], refs[5 * N_GROUPS:], pl.program_id(1), pl.program_id(2))


def _attn_prompt_specs(qd, kd, vd, batch, seq_len, index):
    assert seq_len % ATTN_TILE == 0
    args, in_specs, out_shape, out_specs = [], [], [], []
    for g in range(N_GROUPS):
        d = DILATIONS[g]
        nb = UNITS // d

        def cur(*i, d=d, nb=nb):
            b, t, u = index(*i)
            return (b, t * nb + u // d, u % d)

        def prev(*i, d=d, nb=nb):
            b, t, u = index(*i)
            return (b, jnp.maximum(t * nb + u // d - 1, 0), u % d)

        view = lambda a, d=d: a.reshape(batch, seq_len // d, d * GROUP_W)
        blk = lambda im: pl.BlockSpec((None, BLK, GROUP_W), im)
        args += [view(qd[g]), view(kd[g]), view(kd[g]), view(vd[g]), view(vd[g])]
        in_specs += [blk(cur), blk(cur), blk(prev), blk(cur), blk(prev)]
        out_shape += [jax.ShapeDtypeStruct((batch, seq_len // d, d * GROUP_W), BF16),
                      jax.ShapeDtypeStruct((batch, seq_len // d, d * GROUP_W), F32)]
        out_specs += [blk(cur), blk(cur)]
    return args, in_specs, out_shape, out_specs


def _attn_prompt(qd, kd, vd, batch, seq_len):
    args, in_specs, out_shape, out_specs = _attn_prompt_specs(qd, kd, vd, batch, seq_len, lambda b, t, u: (b, t, u))
    res = pl.pallas_call(
        _attn_prompt_kernel, grid=(batch, seq_len // ATTN_TILE, UNITS), in_specs=in_specs, out_specs=out_specs,
        out_shape=out_shape,
        compiler_params=pltpu.CompilerParams(dimension_semantics=("arbitrary",) * 3),
        name="attn_prompt",
    )(*args)
    return [r.reshape(-1, r.shape[2]) for r in res]


def _conv_sample_kernel(state_ref, u_ref, cw_ref, cb_ref, lng_ref, lnb_ref, a_ref, new_state_ref):
    hist = state_ref.shape[0]
    s_len = u_ref.shape[0]
    row = lambda t: state_ref[t] if t < hist else u_ref[t - hist]
    for s in range(s_len):
        acc = None
        for j in range(CONV_WIDTH):
            term = row(s + j) * cw_ref[j:j + 1, :]
            acc = term if acc is None else acc + term
        a_ref[s] = _layernorm_silu(acc + cb_ref[...], lng_ref[...], lnb_ref[...])
    for t in range(hist):
        new_state_ref[t] = row(t + s_len)


def _conv_sample(state_t, u_t, w):
    hist, db, cc = state_t.shape
    s_len = u_t.shape[0]
    assert hist == CONV_WIDTH - 1 and db % CONV_SEQS == 0
    conv_w = [w["conv_w"], w["conv_b"], w["conv_ln_g"], w["conv_ln_b"]]
    slab = lambda rows: pl.BlockSpec((rows, CONV_SEQS, cc), lambda i: (0, i, 0))
    return pl.pallas_call(
        _conv_sample_kernel, grid=(db // CONV_SEQS,),
        in_specs=[slab(hist), slab(s_len)] + [_resident(a.shape) for a in conv_w],
        out_specs=[slab(s_len), slab(hist)],
        out_shape=[jax.ShapeDtypeStruct((s_len, db, cc), F32), jax.ShapeDtypeStruct((hist, db, cc), F32)],
        compiler_params=pltpu.CompilerParams(dimension_semantics=("arbitrary",)),
        name="conv_sample",
    )(state_t, u_t, *conv_w)


def _sample_bias(width, d, s_len):
    s = lax.broadcasted_iota(jnp.int32, (SAMPLE_ROWS, width), 0)
    t = lax.broadcasted_iota(jnp.int32, (SAMPLE_ROWS, width), 1)
    same_residue = jnp.bitwise_and(s - t, d - 1) == 0
    live = s < s_len
    old = jnp.where(live & same_residue & (t >= s), 0.0, NEG)
    sn = s[:, :LANES]
    tn = t[:, :LANES]
    new = jnp.where(live[:, :LANES] & same_residue[:, :LANES] & (tn <= sn), 0.0, NEG)
    return old, new


def _attn_sample_kernel(*refs, s_len):
    _attn_sample_body(*refs, s_len=s_len)


def _attn_sample_body(q_ref, kn_ref, vn_ref, kc0_ref, vc0_ref, kc1_ref, vc1_ref, kc2_ref, vc2_ref,
                      b_ref, nk0_ref, nv0_ref, nk1_ref, nv1_ref, nk2_ref, nv2_ref, *, s_len):
    kc_refs = (kc0_ref, kc1_ref, kc2_ref)
    vc_refs = (vc0_ref, vc1_ref, vc2_ref)
    nk_refs = (nk0_ref, nk1_ref, nk2_ref)
    nv_refs = (nv0_ref, nv1_ref, nv2_ref)
    q = q_ref[...]
    pad = jnp.zeros((LANES - SAMPLE_ROWS, ATTN_W), F32)
    knt = jnp.transpose(jnp.concatenate([kn_ref[...], pad], axis=0))
    vnt = jnp.transpose(jnp.concatenate([vn_ref[...], pad], axis=0))
    biases = [_sample_bias(WINDOWS[g], DILATIONS[g], s_len) for g in range(N_GROUPS)]
    lane = lax.broadcasted_iota(jnp.int32, (1, LANES), 1)
    keep = lane < LANES - s_len
    nt = (((1,), (1,)), ((), ()))

    def shifted(old, new_cols, out_ref, h):
        w = old.shape[1]
        cols = [old[:, j * LANES:(j + 1) * LANES] for j in range(w // LANES)] + [new_cols]
        rolled = [pltpu.roll(c, LANES - s_len, 1) for c in cols]
        for j in range(w // LANES):
            out_ref[h, :, j * LANES:(j + 1) * LANES] = jnp.where(keep, rolled[j], rolled[j + 1])

    heads = range(HEADS_PER_GROUP)
    lanes_of = lambda g, h: slice((g * HEADS_PER_GROUP + h) * HEAD_DIM, (g * HEADS_PER_GROUP + h + 1) * HEAD_DIM)
    s_old = [[None] * N_GROUPS for _ in heads]
    s_new = [[None] * N_GROUPS for _ in heads]
    for h in heads:
        for g in range(N_GROUPS):
            qh = q[:, lanes_of(g, h)].astype(BF16)
            k_old = kc_refs[g][h]
            k_new = knt[lanes_of(g, h), :]
            s_old[h][g] = jnp.dot(qh, k_old.astype(BF16), preferred_element_type=F32) * ATTN_SCALE + biases[g][0]
            s_new[h][g] = jnp.dot(qh, k_new.astype(BF16), preferred_element_type=F32) * ATTN_SCALE + biases[g][1]
            shifted(k_old, k_new, nk_refs[g], h)
    p_old = [[None] * N_GROUPS for _ in heads]
    p_new = [[None] * N_GROUPS for _ in heads]
    dens = []
    for h in heads:
        m = None
        for a in s_old[h] + s_new[h]:
            am = jnp.max(a, axis=1, keepdims=True)
            m = am if m is None else jnp.maximum(m, am)
        den = jnp.zeros((SAMPLE_ROWS, 1), F32)
        for g in range(N_GROUPS):
            p_old[h][g] = jnp.exp(s_old[h][g] - m)
            p_new[h][g] = jnp.exp(s_new[h][g] - m)
            den = den + jnp.sum(p_old[h][g], axis=1, keepdims=True) + jnp.sum(p_new[h][g], axis=1, keepdims=True)
        dens.append(den)
    for h in heads:
        num = jnp.zeros((SAMPLE_ROWS, HEAD_DIM), F32)
        for g in range(N_GROUPS):
            v_old = vc_refs[g][h]
            v_new = vnt[lanes_of(g, h), :]
            num = num + lax.dot_general(p_old[h][g].astype(BF16), v_old.astype(BF16), nt,
                                        preferred_element_type=F32) \
                + lax.dot_general(p_new[h][g].astype(BF16), v_new.astype(BF16), nt, preferred_element_type=F32)
            shifted(v_old, v_new, nv_refs[g], h)
        b_ref[:, h * HEAD_DIM:(h + 1) * HEAD_DIM] = num / dens[h]


def _attn_sample_specs(q8, kn8, vn8, caches_k, caches_v, s_len):
    db = q8.shape[0]
    assert q8.shape[1] == SAMPLE_ROWS and s_len <= min(SAMPLE_ROWS, DILATIONS[1])
    tok = pl.BlockSpec((None, SAMPLE_ROWS, ATTN_W), lambda b: (b, 0, 0))
    args, in_specs, out_shape, out_specs = [q8, kn8, vn8], [tok] * 3, [], []
    out_shape.append(jax.ShapeDtypeStruct((db, SAMPLE_ROWS, GROUP_W), F32))
    out_specs.append(pl.BlockSpec((None, SAMPLE_ROWS, GROUP_W), lambda b: (b, 0, 0)))
    for g in range(N_GROUPS):
        shape = (db, HEADS_PER_GROUP, HEAD_DIM, WINDOWS[g])
        spec = pl.BlockSpec((None,) + shape[1:], lambda b: (b, 0, 0, 0))
        for c in (caches_k[g], caches_v[g]):
            assert c.shape == shape
            args.append(c)
            in_specs.append(spec)
            out_shape.append(jax.ShapeDtypeStruct(shape, F32))
            out_specs.append(spec)
    return args, in_specs, out_shape, out_specs


def _attn_sample(q8, kn8, vn8, caches_k, caches_v, s_len):
    db = q8.shape[0]
    args, in_specs, out_shape, out_specs = _attn_sample_specs(q8, kn8, vn8, caches_k, caches_v, s_len)
    return pl.pallas_call(
        functools.partial(_attn_sample_kernel, s_len=s_len), grid=(db,), in_specs=in_specs, out_specs=out_specs,
        out_shape=out_shape,
        compiler_params=pltpu.CompilerParams(dimension_semantics=("arbitrary",), vmem_limit_bytes=VMEM_LIMIT),
        name="attn_sample",
    )(*args)


CACHE_RING = 3


def _attn_both_kernel(*refs, s_len, tiles, steps):
    n_pi, n_si = 5 * N_GROUPS, 3 + 2 * N_GROUPS
    n_po, n_c = 2 * N_GROUPS, 2 * N_GROUPS
    p_in, s_in = refs[:n_pi], refs[n_pi:n_pi + n_si]
    p_out, s_out = refs[n_pi + n_si:n_pi + n_si + n_po], refs[n_pi + n_si + n_po:-(n_c + 1)]
    bufs, sem = refs[-(n_c + 1):-1], refs[-1]
    caches = s_in[3:]
    i = pl.program_id(0)

    def copy(c, step):
        slot = step % CACHE_RING
        return pltpu.make_async_copy(caches[c].at[step], bufs[c].at[slot], sem.at[c, slot])

    def start(step):
        for c in range(n_c):
            copy(c, step).start()

    @pl.when(i == 0)
    def _():
        for step in range(min(CACHE_RING - 1, steps)):
            start(step)

    @pl.when(i + (CACHE_RING - 1) < steps)
    def _():
        start(i + (CACHE_RING - 1))

    for c in range(n_c):
        copy(c, i).wait()
    slot = i % CACHE_RING
    _attn_prompt_body(p_in, p_out, (i // UNITS) % tiles, i % UNITS)
    _attn_sample_body(*s_in[:3], *[b.at[slot] for b in bufs], *s_out, s_len=s_len)


def _attn_both(qd, kd, vd, batch, seq_len, q8, kn8, vn8, caches_k, caches_v, s_len):
    tiles = seq_len // ATTN_TILE
    n = batch * tiles * UNITS
    assert n == q8.shape[0]
    index = lambda i: (i // (tiles * UNITS), (i // UNITS) % tiles, i % UNITS)
    p_args, p_in, p_shape, p_out = _attn_prompt_specs(qd, kd, vd, batch, seq_len, index)
    s_args, s_in, s_shape, s_out = _attn_sample_specs(q8, kn8, vn8, caches_k, caches_v, s_len)
    caches = s_args[3:]
    s_in = s_in[:3] + [pl.BlockSpec(memory_space=pl.ANY)] * len(caches)
    scratch = [pltpu.VMEM((CACHE_RING,) + c.shape[1:], F32) for c in caches]
    scratch.append(pltpu.SemaphoreType.DMA((len(caches), CACHE_RING)))
    res = pl.pallas_call(
        functools.partial(_attn_both_kernel, s_len=s_len, tiles=tiles, steps=n), grid=(n,), in_specs=p_in + s_in,
        out_specs=p_out + s_out, out_shape=p_shape + s_shape, scratch_shapes=scratch,
        compiler_params=pltpu.CompilerParams(dimension_semantics=("arbitrary",), vmem_limit_bytes=VMEM_LIMIT),
        name="attn_both",
    )(*p_args, *s_args)
    return [r.reshape(-1, r.shape[2]) for r in res[:len(p_shape)]], res[len(p_shape):]


def _merge_stage(x1_ref, a, battn, gm_ref, wg_ref, bg_ref, wao_ref, wout_ref):
    x1 = x1_ref[...]
    d = x1.shape[1]
    h = _rms(x1, gm_ref[...]).astype(BF16)
    gates = jax.nn.sigmoid(jnp.dot(h, wg_ref[...], preferred_element_type=F32) + bg_ref[...])
    b = jnp.dot(battn.astype(BF16), wao_ref[...], preferred_element_type=F32)
    mix = (gates[:, :d] * a + gates[:, d:] * b).astype(BF16)
    return x1 + jnp.dot(mix, wout_ref[...], preferred_element_type=F32)


def _ffn_stage(y, g2_ref, wgu_ref, wo_ref, act_ref):
    return y + 0.5 * _swiglu(_rms(y, g2_ref[...]).astype(BF16), wgu_ref, wo_ref, act_ref)


def _undilate(blk, pt_ref, d):
    if d == 1:
        return blk.astype(F32)
    rows = jnp.concatenate([blk[:, r * GROUP_W:(r + 1) * GROUP_W] for r in range(d)], axis=0)
    pt = pt_ref[...]
    if rows.dtype == BF16:
        return jnp.dot(pt, rows, preferred_element_type=F32)
    hi = rows.astype(BF16)
    rest = rows - hi.astype(F32)
    mid = rest.astype(BF16)
    lo = (rest - mid.astype(F32)).astype(BF16)
    return jnp.dot(pt, hi, preferred_element_type=F32) + jnp.dot(pt, mid, preferred_element_type=F32) \
        + jnp.dot(pt, lo, preferred_element_type=F32)


def _merge_ffn_prompt_kernel(x1_ref, a_ref, o0_ref, o1_ref, o2_ref, l0_ref, l1_ref, l2_ref, pt1_ref, pt2_ref,
                             gm_ref, wg_ref, bg_ref, wao_ref, wout_ref, g2_ref, wgu_ref, wo_ref, out_ref, act_ref):
    pts = (None, pt1_ref, pt2_ref)
    lses = [_undilate(r[...], pts[g], DILATIONS[g]) for g, r in enumerate((l0_ref, l1_ref, l2_ref))]
    outs = [_undilate(r[...], pts[g], DILATIONS[g]) for g, r in enumerate((o0_ref, o1_ref, o2_ref))]
    mx = jnp.maximum(jnp.maximum(lses[0], lses[1]), lses[2])
    num = None
    den = None
    for o, l in zip(outs, lses):
        wgt = jnp.exp(l - mx)
        num = wgt * o if num is None else num + wgt * o
        den = wgt if den is None else den + wgt
    y = _merge_stage(x1_ref, a_ref[...].astype(F32), num / den, gm_ref, wg_ref, bg_ref, wao_ref, wout_ref)
    out_ref[...] = _ffn_stage(y, g2_ref, wgu_ref, wo_ref, act_ref)


def _merge_ffn_sample_kernel(x1_ref, a_ref, b_ref, gm_ref, wg_ref, bg_ref, wco_ref, wao_ref, wout_ref, g2_ref,
                             wgu_ref, wo_ref, out_ref, act_ref):
    a = jnp.dot(a_ref[...].astype(BF16), wco_ref[...], preferred_element_type=F32)
    y = _merge_stage(x1_ref, a, b_ref[...], gm_ref, wg_ref, bg_ref, wao_ref, wout_ref)
    out_ref[...] = _ffn_stage(y, g2_ref, wgu_ref, wo_ref, act_ref)


def _merge_weights(w):
    return [w["mix_norm"], w["w_in_g"], w["b_gate"], w["w_conv_out"], w["w_attn_out"], w["w_out"], w["ffn2_norm"],
            w["ffn2_wgu"], w["ffn2_wo"]]


def _merge_ffn_sample(x1, a_pre, b, w):
    n, d = x1.shape
    tm = TOKEN_TILE
    assert n % tm == 0
    tok = lambda width: pl.BlockSpec((tm, width), lambda i: (i, 0))
    weights = _merge_weights(w)
    return pl.pallas_call(
        _merge_ffn_sample_kernel, grid=(n // tm,),
        in_specs=[tok(d), tok(a_pre.shape[1]), tok(GROUP_W)] + [_resident(a.shape) for a in weights],
        out_specs=tok(d), out_shape=jax.ShapeDtypeStruct((n, d), F32),
        scratch_shapes=[pltpu.VMEM((tm, w["ffn2_wo"].shape[0]), BF16)],
        compiler_params=pltpu.CompilerParams(dimension_semantics=("arbitrary",), vmem_limit_bytes=VMEM_LIMIT),
        name="merge_ffn_sample",
    )(x1, a_pre, b, *weights)


def _merge_ffn_prompt(x1, a, outs, lses, w):
    n, d = x1.shape
    tm = TOKEN_TILE
    assert n % tm == 0
    tok = lambda rows, width: pl.BlockSpec((rows, width), lambda i: (i, 0))
    grp = [tok(tm // dl, dl * GROUP_W) for dl in DILATIONS]
    mw = _merge_weights(w)
    weights = [w["permt1"], w["permt2"]] + [x for x in mw if x is not w["w_conv_out"]]
    return pl.pallas_call(
        _merge_ffn_prompt_kernel, grid=(n // tm,),
        in_specs=[tok(tm, d), tok(tm, d)] + grp + grp + [_resident(x.shape) for x in weights],
        out_specs=tok(tm, d), out_shape=jax.ShapeDtypeStruct((n, d), F32),
        scratch_shapes=[pltpu.VMEM((tm, w["ffn2_wo"].shape[0]), BF16)],
        compiler_params=pltpu.CompilerParams(dimension_semantics=("arbitrary",), vmem_limit_bytes=VMEM_LIMIT),
        name="merge_ffn_prompt",
    )(x1, a, *outs, *lses, *weights)


def _rope_tables(pos):
    half = ROT_DIM // 2
    inv = jnp.float32(ROPE_THETA) ** (-jnp.arange(half, dtype=F32) * (2.0 / ROT_DIM))
    ang = pos.astype(F32)[:, None] * inv[None, :]
    cos, sin = jnp.cos(ang), jnp.sin(ang)
    n = pos.shape[0]
    rest = HEAD_DIM - ROT_DIM
    zh = jnp.zeros((n, half), F32)
    c = jnp.concatenate([cos, cos, jnp.ones((n, rest), F32)], axis=1)
    s1 = jnp.concatenate([zh, sin, jnp.zeros((n, rest), F32)], axis=1)
    s2 = jnp.concatenate([-sin, zh, jnp.zeros((n, rest), F32)], axis=1)
    rep = LANES // HEAD_DIM
    return tuple(jnp.tile(t, (1, rep)) for t in (c, s1, s2))


def _prepare_weights(l, ffn1_norm, ffn1_w_in, ffn1_w_out, mix_norm, w_in, b_gate, q_norm, k_norm, conv_w, conv_b,
                     conv_ln_g, conv_ln_b, w_conv_out, w_attn_out, w_out, ffn2_norm, ffn2_w_in, ffn2_w_out):
    cc = conv_w.shape[2]
    n_a = 2 * cc + 3 * ATTN_W
    lane = jnp.arange(GROUP_W)
    same_head = (lane[:, None] // HEAD_DIM) == (lane[None, :] // HEAD_DIM)
    row = lambda a: a.reshape(1, -1).astype(F32)
    tok = jnp.arange(TOKEN_TILE)

    def perm(d):
        src = (tok % (TOKEN_TILE // d)) * d + tok // (TOKEN_TILE // d)
        return (src[:, None] == tok[None, :]).astype(BF16)

    return {
        "perm1": perm(DILATIONS[1]), "perm2": perm(DILATIONS[2]),
        "permt1": perm(DILATIONS[1]).T, "permt2": perm(DILATIONS[2]).T,
        "ffn1_norm": row(ffn1_norm[l]), "ffn1_wgu": ffn1_w_in[l].astype(BF16), "ffn1_wo": ffn1_w_out[l].astype(BF16),
        "mix_norm": row(mix_norm[l]), "w_in_a": w_in[l][:, :n_a].astype(BF16), "w_in_g": w_in[l][:, n_a:].astype(BF16),
        "b_gate": row(b_gate[l]),
        "head_mean": jnp.where(same_head, 1.0 / HEAD_DIM, 0.0).astype(BF16),
        "q_gain": row(q_norm[l]), "k_gain": row(k_norm[l]),
        "conv_w": jnp.concatenate([conv_w[l], jnp.zeros((CONV_PAD - CONV_WIDTH, cc), F32)], axis=0),
        "conv_b": row(conv_b[l]), "conv_ln_g": row(conv_ln_g[l]), "conv_ln_b": row(conv_ln_b[l]),
        "w_conv_out": w_conv_out[l].astype(BF16), "w_attn_out": w_attn_out[l].astype(BF16),
        "w_out": w_out[l].astype(BF16),
        "ffn2_norm": row(ffn2_norm[l]), "ffn2_wgu": ffn2_w_in[l].astype(BF16), "ffn2_wo": ffn2_w_out[l].astype(BF16),
    }


def kernel(x_prompt, x_sample, cache_k_w128, cache_v_w128, cache_k_w512, cache_v_w512, cache_k_w2048, cache_v_w2048, state_conv, ffn1_norm, ffn1_w_in, ffn1_w_out, mix_norm, w_in, b_gate, q_norm, k_norm, conv_w, conv_b, conv_ln_g, conv_ln_b, w_conv_out, w_attn_out, w_out, ffn2_norm, ffn2_w_in, ffn2_w_out):
    batch, t, d = x_prompt.shape
    db, s_len, _ = x_sample.shape
    depth = w_in.shape[0]
    cc = conv_w.shape[2]
    hd = (HEADS_PER_GROUP, HEAD_DIM)
    cache_k = (cache_k_w128, cache_k_w512, cache_k_w2048)
    cache_v = (cache_v_w128, cache_v_w512, cache_v_w2048)
    for g, wdw in enumerate(WINDOWS):
        assert cache_k[g].shape == (depth, db, wdw, *hd), "sample caches must hold a full window"
        assert t >= wdw
    rope_p = _rope_tables(jnp.arange(t))
    rope_s = _rope_tables(PAST_LEN + jnp.arange(db * s_len) // db)
    to_time_minor = lambda c: jnp.transpose(c, (0, 2, 3, 1))
    from_time_minor = lambda c: jnp.transpose(c, (0, 3, 1, 2))

    xp = x_prompt.reshape(batch * t, d)
    xs = jnp.transpose(x_sample, (1, 0, 2)).reshape(s_len * db, d)
    outs_p = [[] for _ in range(2 * N_GROUPS)]
    outs_s = [[] for _ in range(2 * N_GROUPS)]
    new_conv_p, new_conv_s = [], []
    for l in range(depth):
        w = _prepare_weights(l, ffn1_norm, ffn1_w_in, ffn1_w_out, mix_norm, w_in, b_gate, q_norm, k_norm, conv_w,
                             conv_b, conv_ln_g, conv_ln_b, w_conv_out, w_attn_out, w_out, ffn2_norm, ffn2_w_in,
                             ffn2_w_out)
        x1p, *rest = _ffn_mix_prompt(xp, rope_p, w, t)
        qkv, (a_p, utail), kv_t = rest[:9], rest[9:11], rest[11:]
        x1s, qs, k32s, v32s, us = _ffn_mix_sample(xs, rope_s, w)
        a_s, state_new = _conv_sample(jnp.transpose(state_conv[l], (1, 0, 2)), us.reshape(s_len, db, cc), w)
        rows8 = lambda a: jnp.pad(jnp.transpose(a.reshape(s_len, db, ATTN_W), (1, 0, 2)),
                                  ((0, 0), (0, SAMPLE_ROWS - s_len), (0, 0)))
        ck = [to_time_minor(cache_k[g][l]) for g in range(N_GROUPS)]
        cv = [to_time_minor(cache_v[g][l]) for g in range(N_GROUPS)]
        if batch * (t // ATTN_TILE) * UNITS == db:
            attn_p, (b_s, *new_caches) = _attn_both(qkv[0:3], qkv[3:6], qkv[6:9], batch, t, rows8(qs), rows8(k32s),
                                                    rows8(v32s), ck, cv, s_len)
        else:
            attn_p = _attn_prompt(qkv[0:3], qkv[3:6], qkv[6:9], batch, t)
            b_s, *new_caches = _attn_sample(rows8(qs), rows8(k32s), rows8(v32s), ck, cv, s_len)
        xp = _merge_ffn_prompt(x1p, a_p, attn_p[0::2], attn_p[1::2], w)
        b_s = jnp.transpose(b_s[:, :s_len], (1, 0, 2)).reshape(s_len * db, GROUP_W)
        xs = _merge_ffn_sample(x1s, a_s.reshape(s_len * db, cc), b_s, w)
        new_conv_p.append(utail[:, CONV_PAD - (CONV_WIDTH - 1):, :])
        new_conv_s.append(jnp.transpose(state_new, (1, 0, 2)))
        for i, (g, c) in enumerate([(g, c) for c in range(2) for g in range(N_GROUPS)]):
            outs_p[2 * g + c].append(from_time_minor(kv_t[i].reshape(batch, *hd, WINDOWS[g])))
        for i, nc in enumerate(new_caches):
            outs_s[i].append(from_time_minor(nc))
    y_s = jnp.transpose(xs.reshape(s_len, db, d), (1, 0, 2))
    return (xp.reshape(batch, t, d), y_s, *[jnp.stack(o) for o in outs_p], jnp.stack(new_conv_p),
            *[jnp.stack(o) for o in outs_s], jnp.stack(new_conv_s))
```
